```python
import math
import jax, jax.numpy as jnp
from jax import lax
import numpy as np

D_MODEL = 1024
BATCH = 8
SEQ = 4096
DEPTH = 1

D_LRU = D_MODEL // 2
D_POOL = D_MODEL - D_LRU
D_MIX = D_LRU + D_POOL
LRU_HEADS = 8
LRU_HEAD_DIM = D_LRU // LRU_HEADS
CONV_WIDTH = 4
LRU_C = 8.0
POOL_WINDOWS = (2, 4, 8, 16)
POOL_GROUPS = len(POOL_WINDOWS)
POOL_GROUP_DIM = D_POOL // POOL_GROUPS
N_EXPERTS = 256
TOP_K = 8
N_EXPERT_GROUPS = 8
TOPK_GROUPS = 4
D_EXPERT = D_MODEL // 4
ROUTED_SCALE = 2.5
DISPATCH_BLOCK = 256
LN_EPS = 1e-5
DEEPNORM_ALPHA = (2.0 * DEPTH) ** 0.25
DEEPNORM_BETA = (8.0 * DEPTH) ** -0.25

kernel_name = "hybrid_rglru_pool_moe_deepnorm"


def layer_norm(x, g, b):
    xf = x.astype(jnp.float32)
    mu = xf.mean(-1, keepdims=True)
    var = jnp.square(xf - mu).mean(-1, keepdims=True)
    y = (xf - mu) * lax.rsqrt(var + LN_EPS) * g.astype(jnp.float32) + b.astype(jnp.float32)
    return y.astype(x.dtype)


def causal_depthwise_conv(u, w, b):
    y = lax.conv_general_dilated(
        u, w[:, None, :].astype(u.dtype), window_strides=(1,),
        padding=[(CONV_WIDTH - 1, 0)],
        dimension_numbers=("NWC", "WIO", "NWC"),
        feature_group_count=u.shape[-1])
    return y + b.astype(u.dtype)


def rg_lru(u, wa, ba, wx, bx, lam):
    B, S, C = u.shape
    uh = u.reshape(B, S, LRU_HEADS, LRU_HEAD_DIM)
    r = jax.nn.sigmoid((jnp.einsum("bshi,hij->bshj", uh, wa).reshape(B, S, C) + ba).astype(jnp.float32))
    i = jax.nn.sigmoid((jnp.einsum("bshi,hij->bshj", uh, wx).reshape(B, S, C) + bx).astype(jnp.float32))
    log_a = -LRU_C * r * jax.nn.softplus(-lam.astype(jnp.float32))
    a = jnp.exp(log_a)
    mult = jnp.sqrt(-jnp.expm1(2.0 * log_a))
    mult = mult.at[:, 0].set(1.0)
    xin = mult * i * u.astype(jnp.float32)

    def combine(left, right):
        a1, b1 = left
        a2, b2 = right
        return a1 * a2, a2 * b1 + b2

    _, h = lax.associative_scan(combine, (a, xin), axis=1)
    return h.astype(u.dtype)


def causal_multiscale_pool(v, pool_w, pool_scale):
    B, S, C = v.shape
    vf = v.astype(jnp.float32).reshape(B, S, POOL_GROUPS, POOL_GROUP_DIM)
    cs = jnp.concatenate([jnp.zeros((B, 1, POOL_GROUPS, POOL_GROUP_DIM), jnp.float32),
                          jnp.cumsum(vf, axis=1)], axis=1)
    t = jnp.arange(S, dtype=jnp.int32)[:, None]
    win = jnp.array(POOL_WINDOWS, dtype=jnp.int32)[None, :]
    lo = jnp.maximum(t + 1 - win, 0)
    gidx = jnp.arange(POOL_GROUPS, dtype=jnp.int32)[None, :]
    window_sum = cs[:, 1:] - cs[:, lo, gidx]
    count = (t + 1 - lo).astype(jnp.float32)
    pooled = (window_sum / count[None, :, :, None] - vf).astype(v.dtype)
    y = jnp.einsum("bsgi,gij->bsgj", pooled, pool_w).reshape(B, S, C)
    return y * pool_scale


def hybrid_mixer(x, w_in, conv_w, conv_b, lru_wa, lru_ba, lru_wx, lru_bx, lru_lambda, pool_w, pool_scale, w_out):
    proj = jnp.einsum("bsd,dc->bsc", x, w_in)
    u = proj[..., :D_LRU]
    gate = proj[..., D_LRU:2 * D_LRU]
    v = proj[..., 2 * D_LRU:]
    u = causal_depthwise_conv(u, conv_w, conv_b)
    y_lru = rg_lru(u, lru_wa, lru_ba, lru_wx, lru_bx, lru_lambda) * jax.nn.gelu(gate)
    y_pool = causal_multiscale_pool(v, pool_w, pool_scale)
    y = jnp.concatenate([y_lru, y_pool], axis=-1)
    return jnp.einsum("bsc,cd->bsd", y, w_out)


def moe_ffn(x2, router_w, router_bias, w1, w3, w2, sw1, sw3, sw2):
    T, D = x2.shape
    E = N_EXPERTS
    scores = jax.nn.sigmoid(jnp.dot(x2.astype(jnp.float32), router_w.astype(jnp.float32)))
    biased = scores + router_bias.astype(jnp.float32)
    grouped = biased.reshape(T, N_EXPERT_GROUPS, E // N_EXPERT_GROUPS)
    group_score = lax.top_k(grouped, 2)[0].sum(-1)
    _, top_groups = lax.top_k(group_score, TOPK_GROUPS)
    group_keep = (top_groups[:, :, None] == jnp.arange(N_EXPERT_GROUPS)[None, None, :]).any(1)
    masked = jnp.where(group_keep[:, :, None], grouped, -jnp.inf).reshape(T, E)
    _, idx = lax.top_k(masked, TOP_K)
    gate = jnp.take_along_axis(scores, idx, axis=-1)
    gate = gate / gate.sum(-1, keepdims=True) * ROUTED_SCALE

    A = T * TOP_K
    n_blocks = -(-A // DISPATCH_BLOCK) + E
    e_flat = idx.reshape(A).astype(jnp.int32)
    order = jnp.argsort(e_flat)
    e_sorted = e_flat[order]
    tok_sorted = (order // TOP_K).astype(jnp.int32)
    gate_sorted = gate.reshape(A)[order]
    counts = jnp.bincount(e_flat, length=E).astype(jnp.int32)
    padded = (counts + DISPATCH_BLOCK - 1) // DISPATCH_BLOCK * DISPATCH_BLOCK
    pad_end = jnp.cumsum(padded)
    pad_start = pad_end - padded
    grp_start = jnp.cumsum(counts) - counts
    dest = pad_start[e_sorted] + jnp.arange(A, dtype=jnp.int32) - grp_start[e_sorted]
    P = n_blocks * DISPATCH_BLOCK
    slot_tok = jnp.zeros((P,), jnp.int32).at[dest].set(tok_sorted)
    slot_gate = jnp.zeros((P,), jnp.float32).at[dest].set(gate_sorted)
    block_start = jnp.arange(n_blocks, dtype=jnp.int32) * DISPATCH_BLOCK
    block_expert = jnp.minimum(jnp.searchsorted(pad_end, block_start, side="right"), E - 1)

    def expert_block(acc, blk):
        tok, g, e = blk
        xb = x2[tok]
        hb = jax.nn.silu(xb @ w1[e]) * (xb @ w3[e])
        yb = (hb @ w2[e]).astype(jnp.float32) * g[:, None]
        return acc.at[tok].add(yb), None

    routed, _ = lax.scan(expert_block, jnp.zeros((T, D), jnp.float32),
                         (slot_tok.reshape(n_blocks, DISPATCH_BLOCK),
                          slot_gate.reshape(n_blocks, DISPATCH_BLOCK),
                          block_expert))
    shared = (jax.nn.silu(x2 @ sw1) * (x2 @ sw3)) @ sw2
    return (routed + shared.astype(jnp.float32)).astype(x2.dtype)


def setup_inputs(seed: int = 0) -> dict:
    key = jax.random.key(seed)
    ks = jax.random.split(key, 24)
    L = DEPTH
    f32 = jnp.float32

    def nrm(k, shape, scale):
        return jax.random.normal(k, shape, f32) * scale

    u = jax.random.uniform(ks[9], (L, D_LRU), f32, minval=0.9, maxval=0.999)
    s = u ** (1.0 / LRU_C)
    lru_lambda = jnp.log(s) - jnp.log1p(-s)
    return {
        "x": nrm(ks[0], (BATCH, SEQ, D_MODEL), 1.0),
        "w_in": nrm(ks[1], (L, D_MODEL, 2 * D_LRU + D_POOL), D_MODEL ** -0.5),
        "conv_w": nrm(ks[2], (L, CONV_WIDTH, D_LRU), CONV_WIDTH ** -0.5),
        "conv_b": nrm(ks[3], (L, D_LRU), 0.02),
        "lru_wa": nrm(ks[4], (L, LRU_HEADS, LRU_HEAD_DIM, LRU_HEAD_DIM), LRU_HEAD_DIM ** -0.5),
        "lru_ba": nrm(ks[5], (L, D_LRU), 0.02),
        "lru_wx": nrm(ks[6], (L, LRU_HEADS, LRU_HEAD_DIM, LRU_HEAD_DIM), LRU_HEAD_DIM ** -0.5),
        "lru_bx": nrm(ks[7], (L, D_LRU), 0.02),
        "lru_lambda": lru_lambda,
        "pool_w": nrm(ks[10], (L, POOL_GROUPS, POOL_GROUP_DIM, POOL_GROUP_DIM), POOL_GROUP_DIM ** -0.5),
        "pool_scale": 1.0 + nrm(ks[11], (L, D_POOL), 0.02),
        "w_out": nrm(ks[12], (L, D_MIX, D_MODEL), DEEPNORM_BETA * D_MIX ** -0.5),
        "ln1_g": 1.0 + nrm(ks[13], (L, D_MODEL), 0.02),
        "ln1_b": nrm(ks[14], (L, D_MODEL), 0.02),
        "router_w": nrm(ks[15], (L, D_MODEL, N_EXPERTS), D_MODEL ** -0.5),
        "router_bias": nrm(ks[16], (L, N_EXPERTS), 0.01),
        "exp_w1": nrm(ks[17], (L, N_EXPERTS, D_MODEL, D_EXPERT), D_MODEL ** -0.5),
        "exp_w3": nrm(ks[18], (L, N_EXPERTS, D_MODEL, D_EXPERT), D_MODEL ** -0.5),
        "exp_w2": nrm(ks[19], (L, N_EXPERTS, D_EXPERT, D_MODEL), DEEPNORM_BETA * D_EXPERT ** -0.5),
        "sh_w1": nrm(ks[20], (L, D_MODEL, D_EXPERT), D_MODEL ** -0.5),
        "sh_w3": nrm(ks[21], (L, D_MODEL, D_EXPERT), D_MODEL ** -0.5),
        "sh_w2": nrm(ks[22], (L, D_EXPERT, D_MODEL), DEEPNORM_BETA * D_EXPERT ** -0.5),
        "ln2_g": 1.0 + nrm(ks[23], (L, D_MODEL), 0.02),
        "ln2_b": nrm(ks[8], (L, D_MODEL), 0.02),
    }


def reference(x, w_in, conv_w, conv_b, lru_wa, lru_ba, lru_wx, lru_bx, lru_lambda, pool_w, pool_scale, w_out,
              ln1_g, ln1_b, router_w, router_bias, exp_w1, exp_w3, exp_w2, sh_w1, sh_w3, sh_w2, ln2_g, ln2_b):
    B, S, D = x.shape
    for l in range(DEPTH):
        mix = hybrid_mixer(x, w_in[l], conv_w[l], conv_b[l], lru_wa[l], lru_ba[l], lru_wx[l], lru_bx[l],
                           lru_lambda[l], pool_w[l], pool_scale[l], w_out[l])
        x = layer_norm(DEEPNORM_ALPHA * x + mix, ln1_g[l], ln1_b[l])
        ffn = moe_ffn(x.reshape(B * S, D), router_w[l], router_bias[l], exp_w1[l], exp_w3[l], exp_w2[l],
                      sh_w1[l], sh_w3[l], sh_w2[l]).reshape(B, S, D)
        x = layer_norm(DEEPNORM_ALPHA * x + ffn, ln2_g[l], ln2_b[l])
    return x
```

```python
import functools
import math

import jax
import jax.numpy as jnp
from jax import lax
from jax.experimental import pallas as pl
from jax.experimental.pallas import tpu as pltpu

LRU_HEADS = 8
CONV_WIDTH = 4
LRU_C = 8.0
POOL_WINDOWS = (2, 4, 8, 16)
N_EXPERT_GROUPS = 8
TOPK_GROUPS = 4
TOP_K = 8
ROUTED_SCALE = 2.5
LN_EPS = 1e-5
DEPTH = 1
DEEPNORM_ALPHA = (2.0 * DEPTH) ** 0.25

MXU_DIM = 256
POOL_GROUP_DIM = 128
CONV_HIST = 8
POOL_HIST = 16
GATE_LANES = 128
VMEM_LIMIT = 56 * 1024 * 1024

F32 = jnp.float32
BF16 = jnp.bfloat16


def _dot(a, b):
    return jnp.dot(a, b, preferred_element_type=F32)


def _layer_norm(z, g, b):
    mu = jnp.mean(z, axis=-1, keepdims=True)
    zc = z - mu
    var = jnp.mean(zc * zc, axis=-1, keepdims=True)
    return zc * lax.rsqrt(var + LN_EPS) * g + b


def _block_diag_dot(xb, w_ref):
    n = w_ref.shape[0]
    return jnp.concatenate(
        [_dot(xb[:, i * MXU_DIM:(i + 1) * MXU_DIM], w_ref[i]) for i in range(n)], axis=1)


def _mixer_kernel(x_ref, w_in_ref, conv_w_ref, conv_b_ref, wa_ref, ba_ref, wx_ref, bx_ref, lam_ref,
                  pool_w_ref, pool_scale_ref, w_out_ref, g1_ref, b1_ref, rw_hi_ref, rw_lo_ref,
                  x1_ref, x1p_ref, logit_ref, hist_u, hist_v, carry):
    j = pl.program_id(0)
    nb, ts, d = x_ref.shape
    rows = nb * ts
    c = hist_u.shape[-1]

    @pl.when(j == 0)
    def _():
        hist_u[...] = jnp.zeros_like(hist_u)
        hist_v[...] = jnp.zeros_like(hist_v)
        carry[...] = jnp.zeros_like(carry)

    x = x_ref[...].reshape(rows, d)
    proj = _dot(x.astype(BF16), w_in_ref[...])
    u = proj[:, :c].reshape(nb, ts, c)
    gate = proj[:, c:2 * c]
    v = proj[:, 2 * c:].reshape(nb, ts, c)

    ext_u = jnp.concatenate([hist_u[...], u], axis=1)
    hist_u[...] = u[:, ts - CONV_HIST:, :]
    cw = conv_w_ref[...]
    uc = jnp.broadcast_to(conv_b_ref[...].reshape(1, 1, c), (nb, ts, c))
    for k in range(CONV_WIDTH):
        off = CONV_HIST - (CONV_WIDTH - 1) + k
        uc = uc + cw[k:k + 1, :].reshape(1, 1, c) * ext_u[:, off:off + ts, :]
    ucf = uc.reshape(rows, c)
    ucb = ucf.astype(BF16)

    r = jax.nn.sigmoid(_block_diag_dot(ucb, wa_ref) + ba_ref[...])
    i_gate = jax.nn.sigmoid(_block_diag_dot(ucb, wx_ref) + bx_ref[...])
    lam = lam_ref[...]
    softplus_neg_lam = jnp.maximum(-lam, 0.0) + jnp.log1p(jnp.exp(-jnp.abs(lam)))
    log_a = (-LRU_C) * r * softplus_neg_lam
    a = jnp.exp(log_a)
    mult = jnp.sqrt(jnp.tanh(-log_a) * (1.0 + a * a))
    t_local = lax.broadcasted_iota(jnp.int32, (nb, ts, c), 1)
    first = (t_local + j * ts) == 0
    a3 = a.reshape(nb, ts, c)
    x3 = jnp.where(first, 1.0, mult.reshape(nb, ts, c)) * i_gate.reshape(nb, ts, c) * uc

    shift = 1
    while shift < ts:
        valid = t_local >= shift
        a_prev = pltpu.roll(a3, shift, axis=1)
        x_prev = pltpu.roll(x3, shift, axis=1)
        x3 = jnp.where(valid, a3 * x_prev, 0.0) + x3
        a3 = jnp.where(valid, a3 * a_prev, a3)
        shift *= 2
    h = x3 + a3 * carry[...][:, None, :]
    carry[...] = h[:, ts - 1, :]

    gelu_gate = 0.5 * gate * (1.0 + jnp.tanh(math.sqrt(2.0 / math.pi) * (gate + 0.044715 * gate * gate * gate)))
    y_lru = h.reshape(rows, c) * gelu_gate

    ext_v = jnp.concatenate([hist_v[...], v], axis=1)
    hist_v[...] = v[:, ts - POOL_HIST:, :]
    t_glob = lax.broadcasted_iota(jnp.int32, (1, ts, POOL_GROUP_DIM), 1) + j * ts
    pooled = []
    for g, win in enumerate(POOL_WINDOWS):
        lanes = slice(g * POOL_GROUP_DIM, (g + 1) * POOL_GROUP_DIM)
        s = ext_v[:, :, lanes]
        shift = 1
        while shift < win:
            s = s + pltpu.roll(s, shift, axis=1)
            shift *= 2
        count = jnp.minimum(t_glob + 1, win).astype(F32)
        pooled.append(s[:, POOL_HIST:, :] / count - v[:, :, lanes])
    pooled = jnp.concatenate(pooled, axis=-1).reshape(rows, c).astype(BF16)
    y_pool = _block_diag_dot(pooled, pool_w_ref) * pool_scale_ref[...]

    y = jnp.concatenate([y_lru, y_pool], axis=1).astype(BF16)
    mix = _dot(y, w_out_ref[...])
    x1 = _layer_norm(DEEPNORM_ALPHA * x + mix, g1_ref[...], b1_ref[...])
    x1_ref[...] = x1.reshape(nb, ts, d)

    hi = x1.astype(BF16)
    hi_f = hi.astype(F32)
    lo = (x1 - hi_f).astype(BF16)
    logits = _dot(hi, rw_hi_ref[...]) + _dot(lo, rw_hi_ref[...]) + _dot(hi, rw_lo_ref[...])
    logit_ref[...] = logits.reshape(nb, ts, logits.shape[-1])

    bits = lax.bitcast_convert_type(hi_f, jnp.uint32)
    packed = bits[:, :d // 2] | (bits[:, d // 2:] >> 16)
    x1p_ref[...] = packed.reshape(nb, ts, d // 2)


def _const_spec(shape):
    return pl.BlockSpec(shape, lambda j: (0,) * len(shape), pipeline_mode=pl.Buffered(1))


def _regroup_block_diag(w, width):
    heads, hd, _ = w.shape
    per = width // hd
    w = w.reshape(heads // per, per, hd, hd)
    eye = jnp.eye(per, dtype=w.dtype)
    return jnp.einsum("gpij,pq->gpiqj", w, eye).reshape(heads // per, width, width)


def _mixer(x, w_in, conv_w, conv_b, wa, ba, wx, bx, lam, pool_w, pool_scale, w_out, g1, b1, router_w, ts=64):
    nb, s, d = x.shape
    c = conv_w.shape[-1]
    n_exp = router_w.shape[-1]
    rw_hi = router_w.astype(BF16)
    rw_lo = (router_w - rw_hi.astype(F32)).astype(BF16)
    row = lambda p: p.reshape(1, -1)
    operands = (
        x, w_in.astype(BF16), conv_w, row(conv_b),
        _regroup_block_diag(wa, MXU_DIM).astype(BF16), row(ba),
        _regroup_block_diag(wx, MXU_DIM).astype(BF16), row(bx), row(lam),
        _regroup_block_diag(pool_w, MXU_DIM).astype(BF16), row(pool_scale),
        w_out.astype(BF16), row(g1), row(b1), rw_hi, rw_lo)
    tile = lambda width: pl.BlockSpec((nb, ts, width), lambda j: (0, j, 0))
    in_specs = [tile(d)] + [_const_spec(op.shape) for op in operands[1:]]
    return pl.pallas_call(
        _mixer_kernel,
        grid=(s // ts,),
        in_specs=in_specs,
        out_specs=[tile(d), tile(d // 2), tile(n_exp)],
        out_shape=[jax.ShapeDtypeStruct((nb, s, d), F32),
                   jax.ShapeDtypeStruct((nb, s, d // 2), jnp.uint32),
                   jax.ShapeDtypeStruct((nb, s, n_exp), F32)],
        scratch_shapes=[pltpu.VMEM((nb, CONV_HIST, c), F32),
                        pltpu.VMEM((nb, POOL_HIST, c), F32),
                        pltpu.VMEM((nb, c), F32)],
        compiler_params=pltpu.CompilerParams(dimension_semantics=("arbitrary",),
                                             vmem_limit_bytes=VMEM_LIMIT),
        name="mixer_ln_router",
    )(*operands)


def _first_row_of(cond, rows_rev, n):
    return n - jnp.max(jnp.where(cond, rows_rev, 0.0), axis=0, keepdims=True)


def _route_kernel(logit_ref, bias_ref, idx_ref, gate_ref, rank_ref, count_ref, count_scr):
    i = pl.program_id(0)

    @pl.when(i == 0)
    def _():
        count_scr[...] = jnp.zeros_like(count_scr)

    scores = jax.nn.sigmoid(logit_ref[...].T)
    n_exp, tn = scores.shape
    gsz = n_exp // N_EXPERT_GROUPS
    neg = -jnp.inf
    biased = scores + bias_ref[...]
    row = lax.broadcasted_iota(jnp.int32, (n_exp, tn), 0).astype(F32)
    row_rev = n_exp - row

    g_row = lax.broadcasted_iota(jnp.int32, (N_EXPERT_GROUPS, tn), 0)
    group_score = jnp.zeros((N_EXPERT_GROUPS, tn), F32)
    grp_row = lax.broadcasted_iota(jnp.int32, (gsz, tn), 0).astype(F32)
    grp_rev = gsz - grp_row
    for g in range(N_EXPERT_GROUPS):
        blk = biased[g * gsz:(g + 1) * gsz]
        m1 = jnp.max(blk, axis=0, keepdims=True)
        i1 = _first_row_of(blk == m1, grp_rev, gsz)
        m2 = jnp.max(jnp.where(grp_row == i1, neg, blk), axis=0, keepdims=True)
        group_score = jnp.where(g_row == g, m1 + m2, group_score)

    beaten_by = jnp.zeros((N_EXPERT_GROUPS, tn), F32)
    for g in range(N_EXPERT_GROUPS):
        other = jnp.max(jnp.where(g_row == g, group_score, neg), axis=0, keepdims=True)
        wins = (other > group_score) | ((other == group_score) & (g < g_row))
        beaten_by = beaten_by + jnp.where(wins, 1.0, 0.0)
    masked = []
    for g in range(N_EXPERT_GROUPS):
        beaten_g = jnp.max(jnp.where(g_row == g, beaten_by, 0.0), axis=0, keepdims=True)
        masked.append(jnp.where(beaten_g < TOPK_GROUPS, biased[g * gsz:(g + 1) * gsz], neg))
    masked = jnp.concatenate(masked, axis=0)

    k_row = lax.broadcasted_iota(jnp.int32, (TOP_K, tn), 0)
    selected = jnp.zeros((n_exp, tn), F32)
    idx = jnp.zeros((TOP_K, tn), F32)
    gate = jnp.zeros((TOP_K, tn), F32)
    hits = []
    for k in range(TOP_K):
        m = jnp.max(masked, axis=0, keepdims=True)
        ik = _first_row_of(masked == m, row_rev, n_exp)
        hit = row == ik
        gk = jnp.sum(jnp.where(hit, scores, 0.0), axis=0, keepdims=True)
        masked = jnp.where(hit, neg, masked)
        selected = jnp.where(hit, 1.0, selected)
        idx = jnp.where(k_row == k, ik, idx)
        gate = jnp.where(k_row == k, gk, gate)
        hits.append(hit)
    gate = gate / jnp.sum(gate, axis=0, keepdims=True) * ROUTED_SCALE

    earlier = (lax.broadcasted_iota(jnp.int32, (tn, tn), 0) < lax.broadcasted_iota(jnp.int32, (tn, tn), 1))
    before = _dot(selected.astype(BF16), jnp.where(earlier, 1.0, 0.0).astype(BF16)) + count_scr[...]
    rank = jnp.zeros((TOP_K, tn), F32)
    for k in range(TOP_K):
        rk = jnp.sum(jnp.where(hits[k], before, 0.0), axis=0, keepdims=True)
        rank = jnp.where(k_row == k, rk, rank)
    count_scr[...] = count_scr[...] + jnp.sum(selected, axis=1, keepdims=True)

    idx_ref[...] = idx.astype(jnp.int32)
    pad = jnp.zeros((gate_ref.shape[1] - TOP_K, tn), F32)
    gate_ref[...] = jnp.concatenate([gate, pad], axis=0).T
    rank_ref[...] = rank.astype(jnp.int32)
    count_ref[...] = count_scr[...].astype(jnp.int32)


def _route(logits, bias, tn=512):
    t, n_exp = logits.shape
    return pl.pallas_call(
        _route_kernel,
        grid=(t // tn,),
        in_specs=[pl.BlockSpec((tn, n_exp), lambda i: (i, 0)),
                  pl.BlockSpec((n_exp, 1), lambda i: (0, 0))],
        out_specs=[pl.BlockSpec((TOP_K, tn), lambda i: (0, i)),
                   pl.BlockSpec((tn, GATE_LANES), lambda i: (i, 0)),
                   pl.BlockSpec((TOP_K, tn), lambda i: (0, i)),
                   pl.BlockSpec((n_exp, 1), lambda i: (0, 0))],
        out_shape=[jax.ShapeDtypeStruct((TOP_K, t), jnp.int32),
                   jax.ShapeDtypeStruct((t, GATE_LANES), F32),
                   jax.ShapeDtypeStruct((TOP_K, t), jnp.int32),
                   jax.ShapeDtypeStruct((n_exp, 1), jnp.int32)],
        scratch_shapes=[pltpu.VMEM((n_exp, 1), F32)],
        compiler_params=pltpu.CompilerParams(dimension_semantics=("arbitrary",),
                                             vmem_limit_bytes=VMEM_LIMIT),
        name="route_topk_rank",
    )(logits, bias.reshape(n_exp, 1))


def _slot_kernel(idx_ref, rank_ref, start_ref, slot_ref):
    idx = idx_ref[...]
    n_exp = start_ref.shape[0]
    tn = idx.shape[1]
    row = lax.broadcasted_iota(jnp.int32, (n_exp, tn), 0)
    start = start_ref[...]
    first = [jnp.sum(jnp.where(row == idx[k:k + 1], start, 0), axis=0, keepdims=True) for k in range(TOP_K)]
    slot_ref[...] = (jnp.concatenate(first, axis=0) + rank_ref[...]).reshape(slot_ref.shape)


def _slots(idx, rank, expert_start, tn):
    k, t = idx.shape
    n_exp = expert_start.shape[0]
    return pl.pallas_call(
        _slot_kernel,
        grid=(t // tn,),
        in_specs=[pl.BlockSpec((k, tn), lambda i: (0, i)),
                  pl.BlockSpec((k, tn), lambda i: (0, i)),
                  pl.BlockSpec((n_exp, 1), lambda i: (0, 0))],
        out_specs=pl.BlockSpec((1, k, tn), lambda i: (i, 0, 0)),
        out_shape=jax.ShapeDtypeStruct((t // tn, k, tn), jnp.int32),
        compiler_params=pltpu.CompilerParams(dimension_semantics=("arbitrary",)),
        name="dispatch_slots",
    )(idx, rank, expert_start.reshape(n_exp, 1))


def _unpack_bf16_pairs(p):
    hi = lax.bitcast_convert_type(p & jnp.uint32(0xFFFF0000), F32)
    lo = lax.bitcast_convert_type(p << 16, F32)
    return jnp.concatenate([hi, lo], axis=1)


def _pack_bf16_pairs(y):
    n = y.shape[1] // 2
    bits = lax.bitcast_convert_type(y.astype(BF16).astype(F32), jnp.uint32)
    return bits[:, :n] | (bits[:, n:] >> 16)


def _row_copy(src_ref, src_row, dst_ref, dst_row, sem):
    return pltpu.make_async_copy(src_ref.at[pl.ds(src_row, 1)], dst_ref.at[pl.ds(dst_row, 1)], sem)


def _load_slots(slot_hbm, slot_smem, sem):
    cp = pltpu.make_async_copy(slot_hbm.at[pl.program_id(0)], slot_smem, sem)
    cp.start()
    cp.wait()


def _dispatch_kernel(slot_hbm, x_ref, xg_hbm, slot_smem, slot_sem, row_sem):
    _load_slots(slot_hbm, slot_smem, slot_sem)
    n_k, tn = slot_smem.shape

    def copies(t):
        return [_row_copy(x_ref, t, xg_hbm, slot_smem[k, t], row_sem) for k in range(n_k)]

    @pl.loop(0, tn)
    def _(t):
        for cp in copies(t):
            cp.start()

    @pl.loop(0, tn)
    def _(t):
        for cp in copies(t):
            cp.wait()


def _dispatch(slots, x1p, n_rows):
    nt, n_k, tn = slots.shape
    t, w = x1p.shape
    return pl.pallas_call(
        _dispatch_kernel,
        grid=(nt,),
        in_specs=[pl.BlockSpec(memory_space=pl.ANY),
                  pl.BlockSpec((tn, w), lambda i: (i, 0))],
        out_specs=pl.BlockSpec(memory_space=pl.ANY),
        out_shape=jax.ShapeDtypeStruct((n_rows, w), x1p.dtype),
        scratch_shapes=[pltpu.SMEM((n_k, tn), jnp.int32),
                        pltpu.SemaphoreType.DMA, pltpu.SemaphoreType.DMA],
        compiler_params=pltpu.CompilerParams(dimension_semantics=("arbitrary",)),
        name="dispatch_rows",
    )(slots, x1p)


def _expert_kernel(block_expert, n_used, xg_ref, w1_ref, w3_ref, w2_ref, yg_ref):
    del block_expert

    @pl.when(pl.program_id(0) < n_used[0])
    def _():
        xb = _unpack_bf16_pairs(xg_ref[...]).astype(BF16)
        h1 = _dot(xb, w1_ref[0].astype(BF16))
        h3 = _dot(xb, w3_ref[0].astype(BF16))
        h = (h1 * jax.nn.sigmoid(h1) * h3).astype(BF16)
        yg_ref[...] = _pack_bf16_pairs(_dot(h, w2_ref[0].astype(BF16)))


def _experts(block_expert, n_used, xg, w1, w3, w2, bm):
    n_rows, w = xg.shape
    _, d, de = w1.shape
    rows = lambda i, be, nu: (jnp.minimum(i, nu[0] - 1), 0)
    weight = lambda i, be, nu: (be[i], 0, 0)
    return pl.pallas_call(
        _expert_kernel,
        grid_spec=pltpu.PrefetchScalarGridSpec(
            num_scalar_prefetch=2,
            grid=(n_rows // bm,),
            in_specs=[pl.BlockSpec((bm, w), rows),
                      pl.BlockSpec((1, d, de), weight),
                      pl.BlockSpec((1, d, de), weight),
                      pl.BlockSpec((1, de, d), weight)],
            out_specs=pl.BlockSpec((bm, w), rows)),
        out_shape=jax.ShapeDtypeStruct((n_rows, w), jnp.uint32),
        compiler_params=pltpu.CompilerParams(dimension_semantics=("arbitrary",),
                                             vmem_limit_bytes=VMEM_LIMIT),
        name="routed_experts",
    )(block_expert, n_used, xg, w1, w3, w2)


def _combine_kernel(slot_hbm, yg_hbm, x1_ref, gate_ref, sw1_ref, sw3_ref, sw2_ref, g2_ref, b2_ref,
                    out_ref, slot_smem, rows_vmem, slot_sem, row_sem):
    _load_slots(slot_hbm, slot_smem, slot_sem)
    n_k, tn = slot_smem.shape

    def copies(t):
        return [_row_copy(yg_hbm, slot_smem[k, t], rows_vmem.at[k], t, row_sem) for k in range(n_k)]

    @pl.loop(0, tn)
    def _(t):
        for cp in copies(t):
            cp.start()

    x1 = x1_ref[...]
    xb = x1.astype(BF16)
    h1 = _dot(xb, sw1_ref[...])
    h3 = _dot(xb, sw3_ref[...])
    ffn = _dot((h1 * jax.nn.sigmoid(h1) * h3).astype(BF16), sw2_ref[...])

    @pl.loop(0, tn)
    def _(t):
        for cp in copies(t):
            cp.wait()

    gate = gate_ref[...]
    for k in range(n_k):
        ffn = ffn + gate[:, k:k + 1] * _unpack_bf16_pairs(rows_vmem[k])
    out_ref[...] = _layer_norm(DEEPNORM_ALPHA * x1 + ffn, g2_ref[...], b2_ref[...])


def _combine(slots, yg, x1, gate, sw1, sw3, sw2, g2, b2):
    nt, n_k, tn = slots.shape
    t, d = x1.shape
    w = yg.shape[1]
    row = lambda p: p.reshape(1, -1)
    operands = (slots, yg, x1, gate, sw1.astype(BF16), sw3.astype(BF16), sw2.astype(BF16), row(g2), row(b2))
    in_specs = [pl.BlockSpec(memory_space=pl.ANY), pl.BlockSpec(memory_space=pl.ANY),
                pl.BlockSpec((tn, d), lambda i: (i, 0)),
                pl.BlockSpec((tn, gate.shape[1]), lambda i: (i, 0))]
    in_specs += [_const_spec(op.shape) for op in operands[4:]]
    return pl.pallas_call(
        _combine_kernel,
        grid=(nt,),
        in_specs=in_specs,
        out_specs=pl.BlockSpec((tn, d), lambda i: (i, 0)),
        out_shape=jax.ShapeDtypeStruct((t, d), F32),
        scratch_shapes=[pltpu.SMEM((n_k, tn), jnp.int32),
                        pltpu.VMEM((n_k, tn, w), jnp.uint32),
                        pltpu.SemaphoreType.DMA, pltpu.SemaphoreType.DMA],
        compiler_params=pltpu.CompilerParams(dimension_semantics=("arbitrary",),
                                             vmem_limit_bytes=VMEM_LIMIT),
        name="combine_shared_ln",
    )(*operands)


def _block_table(counts, bm, n_blocks):
    n_exp = counts.shape[0]
    padded = (counts + bm - 1) // bm * bm
    ends = jnp.cumsum(padded)
    starts = ends - padded
    n_used = ends[-1] // bm
    block_id = jnp.minimum(jnp.arange(n_blocks, dtype=jnp.int32), n_used - 1)
    block_expert = jnp.minimum(jnp.searchsorted(ends, block_id * bm, side="right"), n_exp - 1)
    return starts.astype(jnp.int32), block_expert.astype(jnp.int32), n_used.astype(jnp.int32).reshape(1)


def _moe(x1, x1p, logits, router_bias, w1, w3, w2, sw1, sw3, sw2, g2, b2, bm=256, tn=128):
    t, d = x1.shape
    n_exp = logits.shape[1]
    idx, gate, rank, counts = _route(logits, router_bias)
    n_blocks = (t * TOP_K) // bm + n_exp
    starts, block_expert, n_used = _block_table(counts[:, 0], bm, n_blocks)
    slots = _slots(idx, rank, starts, tn)
    xg = _dispatch(slots, x1p, n_blocks * bm)
    yg = _experts(block_expert, n_used, xg, w1, w3, w2, bm)
    return _combine(slots, yg, x1, gate, sw1, sw3, sw2, g2, b2)


def kernel(x, w_in, conv_w, conv_b, lru_wa, lru_ba, lru_wx, lru_bx, lru_lambda, pool_w, pool_scale, w_out, ln1_g, ln1_b, router_w, router_bias, exp_w1, exp_w3, exp_w2, sh_w1, sh_w3, sh_w2, ln2_g, ln2_b):
    nb, s, d = x.shape
    for l in range(DEPTH):
        x1, x1p, logits = _mixer(x, w_in[l], conv_w[l], conv_b[l], lru_wa[l], lru_ba[l], lru_wx[l], lru_bx[l],
                                 lru_lambda[l], pool_w[l], pool_scale[l], w_out[l], ln1_g[l], ln1_b[l],
                                 router_w[l])
        t = nb * s
        x = _moe(x1.reshape(t, d), x1p.reshape(t, d // 2), logits.reshape(t, -1), router_bias[l],
                 exp_w1[l], exp_w3[l], exp_w2[l], sh_w1[l], sh_w3[l], sh_w2[l], ln2_g[l], ln2_b[l])
        x = x.reshape(nb, s, d)
    return x
```

```python
import functools
import math

import jax
import jax.numpy as jnp
from jax import lax
from jax.experimental import pallas as pl
from jax.experimental.pallas import tpu as pltpu
from jax.experimental.pallas import tpu_sc as plsc

LRU_HEADS = 8
CONV_WIDTH = 4
LRU_C = 8.0
POOL_WINDOWS = (2, 4, 8, 16)
N_EXPERT_GROUPS = 8
TOPK_GROUPS = 4
TOP_K = 8
ROUTED_SCALE = 2.5
LN_EPS = 1e-5
DEPTH = 1
DEEPNORM_ALPHA = (2.0 * DEPTH) ** 0.25

MXU_DIM = 256
POOL_GROUP_DIM = 128
CONV_HIST = 8
POOL_HIST = 16
GATE_LANES = 128
SC_CHUNK = 128
VMEM_LIMIT = 56 * 1024 * 1024

F32 = jnp.float32
BF16 = jnp.bfloat16


def _dot(a, b):
    return jnp.dot(a, b, preferred_element_type=F32)


def _layer_norm(z, g, b):
    mu = jnp.mean(z, axis=-1, keepdims=True)
    zc = z - mu
    var = jnp.mean(zc * zc, axis=-1, keepdims=True)
    return zc * lax.rsqrt(var + LN_EPS) * g + b


def _block_diag_dot(xb, w_ref):
    n = w_ref.shape[0]
    return jnp.concatenate(
        [_dot(xb[:, i * MXU_DIM:(i + 1) * MXU_DIM], w_ref[i]) for i in range(n)], axis=1)


def _mixer_kernel(x_ref, w_in_ref, conv_w_ref, conv_b_ref, wa_ref, ba_ref, wx_ref, bx_ref, lam_ref,
                  pool_w_ref, pool_scale_ref, w_out_ref, g1_ref, b1_ref, rw_hi_ref, rw_lo_ref,
                  x1_ref, x1p_ref, logit_ref, hist_u, hist_v, carry):
    j = pl.program_id(0)
    nb, ts, d = x_ref.shape
    rows = nb * ts
    c = hist_u.shape[-1]

    @pl.when(j == 0)
    def _():
        hist_u[...] = jnp.zeros_like(hist_u)
        hist_v[...] = jnp.zeros_like(hist_v)
        carry[...] = jnp.zeros_like(carry)

    x = x_ref[...].reshape(rows, d)
    proj = _dot(x.astype(BF16), w_in_ref[...])
    u = proj[:, :c].reshape(nb, ts, c)
    gate = proj[:, c:2 * c]
    v = proj[:, 2 * c:].reshape(nb, ts, c)

    ext_u = jnp.concatenate([hist_u[...], u], axis=1)
    hist_u[...] = u[:, ts - CONV_HIST:, :]
    cw = conv_w_ref[...]
    uc = jnp.broadcast_to(conv_b_ref[...].reshape(1, 1, c), (nb, ts, c))
    for k in range(CONV_WIDTH):
        off = CONV_HIST - (CONV_WIDTH - 1) + k
        uc = uc + cw[k:k + 1, :].reshape(1, 1, c) * ext_u[:, off:off + ts, :]
    ucf = uc.reshape(rows, c)
    ucb = ucf.astype(BF16)

    r = jax.nn.sigmoid(_block_diag_dot(ucb, wa_ref) + ba_ref[...])
    i_gate = jax.nn.sigmoid(_block_diag_dot(ucb, wx_ref) + bx_ref[...])
    lam = lam_ref[...]
    softplus_neg_lam = jnp.maximum(-lam, 0.0) + jnp.log1p(jnp.exp(-jnp.abs(lam)))
    log_a = (-LRU_C) * r * softplus_neg_lam
    a = jnp.exp(log_a)
    mult = jnp.sqrt(jnp.tanh(-log_a) * (1.0 + a * a))
    t_local = lax.broadcasted_iota(jnp.int32, (nb, ts, c), 1)
    first = (t_local + j * ts) == 0
    a3 = a.reshape(nb, ts, c)
    x3 = jnp.where(first, 1.0, mult.reshape(nb, ts, c)) * i_gate.reshape(nb, ts, c) * uc

    shift = 1
    while shift < ts:
        valid = t_local >= shift
        a_prev = pltpu.roll(a3, shift, axis=1)
        x_prev = pltpu.roll(x3, shift, axis=1)
        x3 = jnp.where(valid, a3 * x_prev, 0.0) + x3
        a3 = jnp.where(valid, a3 * a_prev, a3)
        shift *= 2
    h = x3 + a3 * carry[...][:, None, :]
    carry[...] = h[:, ts - 1, :]

    gelu_gate = 0.5 * gate * (1.0 + jnp.tanh(math.sqrt(2.0 / math.pi) * (gate + 0.044715 * gate * gate * gate)))
    y_lru = h.reshape(rows, c) * gelu_gate

    ext_v = jnp.concatenate([hist_v[...], v], axis=1)
    hist_v[...] = v[:, ts - POOL_HIST:, :]
    t_glob = lax.broadcasted_iota(jnp.int32, (1, ts, POOL_GROUP_DIM), 1) + j * ts
    pooled = []
    for g, win in enumerate(POOL_WINDOWS):
        lanes = slice(g * POOL_GROUP_DIM, (g + 1) * POOL_GROUP_DIM)
        s = ext_v[:, :, lanes]
        shift = 1
        while shift < win:
            s = s + pltpu.roll(s, shift, axis=1)
            shift *= 2
        count = jnp.minimum(t_glob + 1, win).astype(F32)
        pooled.append(s[:, POOL_HIST:, :] / count - v[:, :, lanes])
    pooled = jnp.concatenate(pooled, axis=-1).reshape(rows, c).astype(BF16)
    y_pool = _block_diag_dot(pooled, pool_w_ref) * pool_scale_ref[...]

    y = jnp.concatenate([y_lru, y_pool], axis=1).astype(BF16)
    mix = _dot(y, w_out_ref[...])
    x1 = _layer_norm(DEEPNORM_ALPHA * x + mix, g1_ref[...], b1_ref[...])
    x1_ref[...] = x1.reshape(nb, ts, d)

    hi = x1.astype(BF16)
    hi_f = hi.astype(F32)
    lo = (x1 - hi_f).astype(BF16)
    logits = _dot(hi, rw_hi_ref[...]) + _dot(lo, rw_hi_ref[...]) + _dot(hi, rw_lo_ref[...])
    logit_ref[...] = logits.reshape(nb, ts, logits.shape[-1])

    bits = lax.bitcast_convert_type(hi_f, jnp.uint32)
    packed = bits[:, :d // 2] | (bits[:, d // 2:] >> 16)
    x1p_ref[...] = packed.reshape(nb, ts, d // 2)


def _const_spec(shape):
    return pl.BlockSpec(shape, lambda j: (0,) * len(shape), pipeline_mode=pl.Buffered(1))


def _regroup_block_diag(w, width):
    heads, hd, _ = w.shape
    per = width // hd
    w = w.reshape(heads // per, per, hd, hd)
    eye = jnp.eye(per, dtype=w.dtype)
    return jnp.einsum("gpij,pq->gpiqj", w, eye).reshape(heads // per, width, width)


def _mixer(x, w_in, conv_w, conv_b, wa, ba, wx, bx, lam, pool_w, pool_scale, w_out, g1, b1, router_w, ts=64):
    nb, s, d = x.shape
    c = conv_w.shape[-1]
    n_exp = router_w.shape[-1]
    rw_hi = router_w.astype(BF16)
    rw_lo = (router_w - rw_hi.astype(F32)).astype(BF16)
    row = lambda p: p.reshape(1, -1)
    operands = (
        x, w_in.astype(BF16), conv_w, row(conv_b),
        _regroup_block_diag(wa, MXU_DIM).astype(BF16), row(ba),
        _regroup_block_diag(wx, MXU_DIM).astype(BF16), row(bx), row(lam),
        _regroup_block_diag(pool_w, MXU_DIM).astype(BF16), row(pool_scale),
        w_out.astype(BF16), row(g1), row(b1), rw_hi, rw_lo)
    tile = lambda width: pl.BlockSpec((nb, ts, width), lambda j: (0, j, 0))
    in_specs = [tile(d)] + [_const_spec(op.shape) for op in operands[1:]]
    return pl.pallas_call(
        _mixer_kernel,
        grid=(s // ts,),
        in_specs=in_specs,
        out_specs=[tile(d), tile(d // 2), tile(n_exp)],
        out_shape=[jax.ShapeDtypeStruct((nb, s, d), F32),
                   jax.ShapeDtypeStruct((nb, s, d // 2), jnp.uint32),
                   jax.ShapeDtypeStruct((nb, s, n_exp), F32)],
        scratch_shapes=[pltpu.VMEM((nb, CONV_HIST, c), F32),
                        pltpu.VMEM((nb, POOL_HIST, c), F32),
                        pltpu.VMEM((nb, c), F32)],
        compiler_params=pltpu.CompilerParams(dimension_semantics=("arbitrary",),
                                             vmem_limit_bytes=VMEM_LIMIT),
        name="mixer_ln_router",
    )(*operands)


def _first_row_of(cond, rows_rev, n):
    return n - jnp.max(jnp.where(cond, rows_rev, 0.0), axis=0, keepdims=True)


def _route_kernel(logit_ref, bias_ref, idx_ref, gate_ref, rank_ref, count_ref, count_scr):
    i = pl.program_id(0)

    @pl.when(i == 0)
    def _():
        count_scr[...] = jnp.zeros_like(count_scr)

    scores = jax.nn.sigmoid(logit_ref[...].T)
    n_exp, tn = scores.shape
    gsz = n_exp // N_EXPERT_GROUPS
    neg = -jnp.inf
    biased = scores + bias_ref[...]
    row = lax.broadcasted_iota(jnp.int32, (n_exp, tn), 0).astype(F32)
    row_rev = n_exp - row

    g_row = lax.broadcasted_iota(jnp.int32, (N_EXPERT_GROUPS, tn), 0)
    group_score = jnp.zeros((N_EXPERT_GROUPS, tn), F32)
    grp_row = lax.broadcasted_iota(jnp.int32, (gsz, tn), 0).astype(F32)
    grp_rev = gsz - grp_row
    for g in range(N_EXPERT_GROUPS):
        blk = biased[g * gsz:(g + 1) * gsz]
        m1 = jnp.max(blk, axis=0, keepdims=True)
        i1 = _first_row_of(blk == m1, grp_rev, gsz)
        m2 = jnp.max(jnp.where(grp_row == i1, neg, blk), axis=0, keepdims=True)
        group_score = jnp.where(g_row == g, m1 + m2, group_score)

    beaten_by = jnp.zeros((N_EXPERT_GROUPS, tn), F32)
    for g in range(N_EXPERT_GROUPS):
        other = jnp.max(jnp.where(g_row == g, group_score, neg), axis=0, keepdims=True)
        wins = (other > group_score) | ((other == group_score) & (g < g_row))
        beaten_by = beaten_by + jnp.where(wins, 1.0, 0.0)
    masked = []
    for g in range(N_EXPERT_GROUPS):
        beaten_g = jnp.max(jnp.where(g_row == g, beaten_by, 0.0), axis=0, keepdims=True)
        masked.append(jnp.where(beaten_g < TOPK_GROUPS, biased[g * gsz:(g + 1) * gsz], neg))
    masked = jnp.concatenate(masked, axis=0)

    k_row = lax.broadcasted_iota(jnp.int32, (TOP_K, tn), 0)
    selected = jnp.zeros((n_exp, tn), F32)
    idx = jnp.zeros((TOP_K, tn), F32)
    gate = jnp.zeros((TOP_K, tn), F32)
    hits = []
    for k in range(TOP_K):
        m = jnp.max(masked, axis=0, keepdims=True)
        ik = _first_row_of(masked == m, row_rev, n_exp)
        hit = row == ik
        gk = jnp.sum(jnp.where(hit, scores, 0.0), axis=0, keepdims=True)
        masked = jnp.where(hit, neg, masked)
        selected = jnp.where(hit, 1.0, selected)
        idx = jnp.where(k_row == k, ik, idx)
        gate = jnp.where(k_row == k, gk, gate)
        hits.append(hit)
    gate = gate / jnp.sum(gate, axis=0, keepdims=True) * ROUTED_SCALE

    earlier = (lax.broadcasted_iota(jnp.int32, (tn, tn), 0) < lax.broadcasted_iota(jnp.int32, (tn, tn), 1))
    before = _dot(selected.astype(BF16), jnp.where(earlier, 1.0, 0.0).astype(BF16)) + count_scr[...]
    rank = jnp.zeros((TOP_K, tn), F32)
    for k in range(TOP_K):
        rk = jnp.sum(jnp.where(hits[k], before, 0.0), axis=0, keepdims=True)
        rank = jnp.where(k_row == k, rk, rank)
    count_scr[...] = count_scr[...] + jnp.sum(selected, axis=1, keepdims=True)

    idx_ref[...] = idx.astype(jnp.int32)
    pad = jnp.zeros((gate_ref.shape[1] - TOP_K, tn), F32)
    gate_ref[...] = jnp.concatenate([gate, pad], axis=0).T
    rank_ref[...] = rank.astype(jnp.int32)
    count_ref[...] = count_scr[...].astype(jnp.int32)


def _route(logits, bias, tn=512):
    t, n_exp = logits.shape
    return pl.pallas_call(
        _route_kernel,
        grid=(t // tn,),
        in_specs=[pl.BlockSpec((tn, n_exp), lambda i: (i, 0)),
                  pl.BlockSpec((n_exp, 1), lambda i: (0, 0))],
        out_specs=[pl.BlockSpec((TOP_K, tn), lambda i: (0, i)),
                   pl.BlockSpec((tn, GATE_LANES), lambda i: (i, 0)),
                   pl.BlockSpec((TOP_K, tn), lambda i: (0, i)),
                   pl.BlockSpec((n_exp, 1), lambda i: (0, 0))],
        out_shape=[jax.ShapeDtypeStruct((TOP_K, t), jnp.int32),
                   jax.ShapeDtypeStruct((t, GATE_LANES), F32),
                   jax.ShapeDtypeStruct((TOP_K, t), jnp.int32),
                   jax.ShapeDtypeStruct((n_exp, 1), jnp.int32)],
        scratch_shapes=[pltpu.VMEM((n_exp, 1), F32)],
        compiler_params=pltpu.CompilerParams(dimension_semantics=("arbitrary",),
                                             vmem_limit_bytes=VMEM_LIMIT),
        name="route_topk_rank",
    )(logits, bias.reshape(n_exp, 1))


def _slot_kernel(idx_ref, rank_ref, start_ref, slot_ref):
    idx = idx_ref[...]
    n_exp = start_ref.shape[0]
    tn = idx.shape[1]
    row = lax.broadcasted_iota(jnp.int32, (n_exp, tn), 0)
    start = start_ref[...]
    first = [jnp.sum(jnp.where(row == idx[k:k + 1], start, 0), axis=0, keepdims=True) for k in range(TOP_K)]
    slots = jnp.concatenate(first, axis=0) + rank_ref[...]
    chunk = slot_ref.shape[-1]
    for c in range(slot_ref.shape[0]):
        slot_ref[c] = slots[:, c * chunk:(c + 1) * chunk]


def _slots(idx, rank, expert_start, tn=512):
    k, t = idx.shape
    n_exp = expert_start.shape[0]
    per_step = tn // SC_CHUNK
    return pl.pallas_call(
        _slot_kernel,
        grid=(t // tn,),
        in_specs=[pl.BlockSpec((k, tn), lambda i: (0, i)),
                  pl.BlockSpec((k, tn), lambda i: (0, i)),
                  pl.BlockSpec((n_exp, 1), lambda i: (0, 0))],
        out_specs=pl.BlockSpec((per_step, k, SC_CHUNK), lambda i: (i, 0, 0)),
        out_shape=jax.ShapeDtypeStruct((t // SC_CHUNK, k, SC_CHUNK), jnp.int32),
        compiler_params=pltpu.CompilerParams(dimension_semantics=("arbitrary",)),
        name="dispatch_slots",
    )(idx, rank, expert_start.reshape(n_exp, 1))


def _unpack_bf16_pairs(p):
    hi = lax.bitcast_convert_type(p & jnp.uint32(0xFFFF0000), F32)
    lo = lax.bitcast_convert_type(p << 16, F32)
    return jnp.concatenate([hi, lo], axis=1)


def _pack_bf16_pairs(y):
    n = y.shape[1] // 2
    bits = lax.bitcast_convert_type(y.astype(BF16).astype(F32), jnp.uint32)
    return bits[:, :n] | (bits[:, n:] >> 16)


def _sc_worker_layout(n_chunks_total):
    info = plsc.get_sparse_core_info()
    n_workers = info.num_cores * info.num_subcores
    return info.num_cores, n_chunks_total // n_workers


def _sc_dispatch(slots, x1p, n_rows):
    n_chunks_total, n_k, chunk = slots.shape
    t, w = x1p.shape
    n_cores, n_chunks = _sc_worker_layout(n_chunks_total)
    mesh = plsc.VectorSubcoreMesh(core_axis_name="c", subcore_axis_name="s")

    @functools.partial(
        pl.kernel, mesh=mesh, name="sc_dispatch_rows",
        out_type=jax.ShapeDtypeStruct((n_rows, w), x1p.dtype),
        scratch_types=[pltpu.VMEM((n_k, chunk), jnp.int32), pltpu.VMEM((chunk, w), x1p.dtype),
                       pltpu.SemaphoreType.DMA])
    def dispatch(slots_hbm, x_hbm, xg_hbm, idx_v, rows_v, sem):
        wid = lax.axis_index("s") * n_cores + lax.axis_index("c")

        @pl.loop(0, n_chunks)
        def _(ci):
            chunk_id = wid * n_chunks + ci
            pltpu.sync_copy(slots_hbm.at[chunk_id], idx_v)
            pltpu.sync_copy(x_hbm.at[pl.ds(chunk_id * chunk, chunk)], rows_v)
            copies = [pltpu.make_async_copy(rows_v, xg_hbm.at[idx_v.at[k]], sem) for k in range(n_k)]
            for cp in copies:
                cp.start()
            for cp in copies:
                cp.wait()

    return dispatch(slots, x1p)


def _sc_collect(slots, yg):
    n_chunks_total, n_k, chunk = slots.shape
    w = yg.shape[1]
    t = n_chunks_total * chunk
    n_cores, n_chunks = _sc_worker_layout(n_chunks_total)
    mesh = plsc.VectorSubcoreMesh(core_axis_name="c", subcore_axis_name="s")

    @functools.partial(
        pl.kernel, mesh=mesh, name="sc_collect_rows",
        out_type=jax.ShapeDtypeStruct((n_k, t, w), yg.dtype),
        scratch_types=[pltpu.VMEM((n_k, chunk), jnp.int32), pltpu.VMEM((chunk, w), yg.dtype),
                       pltpu.SemaphoreType.DMA])
    def collect(slots_hbm, yg_hbm, yt_hbm, idx_v, rows_v, sem):
        wid = lax.axis_index("s") * n_cores + lax.axis_index("c")

        @pl.loop(0, n_chunks)
        def _(ci):
            chunk_id = wid * n_chunks + ci
            pltpu.sync_copy(slots_hbm.at[chunk_id], idx_v)
            for k in range(n_k):
                pltpu.async_copy(yg_hbm.at[idx_v.at[k]], rows_v, sem).wait()
                pltpu.sync_copy(rows_v, yt_hbm.at[k, pl.ds(chunk_id * chunk, chunk)])

    return collect(slots, yg)


def _expert_kernel(block_expert, n_used, xg_ref, w1_ref, w3_ref, w2_ref, yg_ref):
    del block_expert

    @pl.when(pl.program_id(0) < n_used[0])
    def _():
        xb = _unpack_bf16_pairs(xg_ref[...]).astype(BF16)
        h1 = _dot(xb, w1_ref[0].astype(BF16))
        h3 = _dot(xb, w3_ref[0].astype(BF16))
        h = (h1 * jax.nn.sigmoid(h1) * h3).astype(BF16)
        yg_ref[...] = _pack_bf16_pairs(_dot(h, w2_ref[0].astype(BF16)))


def _experts(block_expert, n_used, xg, w1, w3, w2, bm):
    n_rows, w = xg.shape
    _, d, de = w1.shape
    rows = lambda i, be, nu: (jnp.minimum(i, nu[0] - 1), 0)
    weight = lambda i, be, nu: (be[i], 0, 0)
    return pl.pallas_call(
        _expert_kernel,
        grid_spec=pltpu.PrefetchScalarGridSpec(
            num_scalar_prefetch=2,
            grid=(n_rows // bm,),
            in_specs=[pl.BlockSpec((bm, w), rows),
                      pl.BlockSpec((1, d, de), weight),
                      pl.BlockSpec((1, d, de), weight),
                      pl.BlockSpec((1, de, d), weight)],
            out_specs=pl.BlockSpec((bm, w), rows)),
        out_shape=jax.ShapeDtypeStruct((n_rows, w), jnp.uint32),
        compiler_params=pltpu.CompilerParams(dimension_semantics=("arbitrary",),
                                             vmem_limit_bytes=VMEM_LIMIT),
        name="routed_experts",
    )(block_expert, n_used, xg, w1, w3, w2)


def _combine_kernel(yt_ref, x1_ref, gate_ref, sw1_ref, sw3_ref, sw2_ref, g2_ref, b2_ref, out_ref):
    x1 = x1_ref[...]
    xb = x1.astype(BF16)
    h1 = _dot(xb, sw1_ref[...])
    h3 = _dot(xb, sw3_ref[...])
    ffn = _dot((h1 * jax.nn.sigmoid(h1) * h3).astype(BF16), sw2_ref[...])
    gate = gate_ref[...]
    for k in range(yt_ref.shape[0]):
        ffn = ffn + gate[:, k:k + 1] * _unpack_bf16_pairs(yt_ref[k])
    out_ref[...] = _layer_norm(DEEPNORM_ALPHA * x1 + ffn, g2_ref[...], b2_ref[...])


def _combine(yt, x1, gate, sw1, sw3, sw2, g2, b2, tn=256):
    n_k, t, w = yt.shape
    d = x1.shape[1]
    row = lambda p: p.reshape(1, -1)
    operands = (yt, x1, gate, sw1.astype(BF16), sw3.astype(BF16), sw2.astype(BF16), row(g2), row(b2))
    in_specs = [pl.BlockSpec((n_k, tn, w), lambda i: (0, i, 0)),
                pl.BlockSpec((tn, d), lambda i: (i, 0)),
                pl.BlockSpec((tn, gate.shape[1]), lambda i: (i, 0))]
    in_specs += [_const_spec(op.shape) for op in operands[3:]]
    return pl.pallas_call(
        _combine_kernel,
        grid=(t // tn,),
        in_specs=in_specs,
        out_specs=pl.BlockSpec((tn, d), lambda i: (i, 0)),
        out_shape=jax.ShapeDtypeStruct((t, d), F32),
        compiler_params=pltpu.CompilerParams(dimension_semantics=("arbitrary",),
                                             vmem_limit_bytes=VMEM_LIMIT),
        name="combine_shared_ln",
    )(*operands)


def _block_table(counts, bm, n_blocks):
    n_exp = counts.shape[0]
    padded = (counts + bm - 1) // bm * bm
    ends = jnp.cumsum(padded)
    starts = ends - padded
    n_used = ends[-1] // bm
    block_row = jnp.minimum(jnp.arange(n_blocks, dtype=jnp.int32), n_used - 1) * bm
    block_expert = jnp.sum((ends[None, :] <= block_row[:, None]).astype(jnp.int32), axis=1)
    block_expert = jnp.minimum(block_expert, n_exp - 1)
    return starts.astype(jnp.int32), block_expert.astype(jnp.int32), n_used.astype(jnp.int32).reshape(1)


def _moe(x1, x1p, logits, router_bias, w1, w3, w2, sw1, sw3, sw2, g2, b2, bm=256):
    t, d = x1.shape
    n_exp = logits.shape[1]
    idx, gate, rank, counts = _route(logits, router_bias)
    n_blocks = (t * TOP_K) // bm + n_exp
    starts, block_expert, n_used = _block_table(counts[:, 0], bm, n_blocks)
    slots = _slots(idx, rank, starts)
    xg = _sc_dispatch(slots, x1p, n_blocks * bm)
    yg = _experts(block_expert, n_used, xg, w1, w3, w2, bm)
    yt = _sc_collect(slots, yg)
    return _combine(yt, x1, gate, sw1, sw3, sw2, g2, b2)


def kernel(x, w_in, conv_w, conv_b, lru_wa, lru_ba, lru_wx, lru_bx, lru_lambda, pool_w, pool_scale, w_out, ln1_g, ln1_b, router_w, router_bias, exp_w1, exp_w3, exp_w2, sh_w1, sh_w3, sh_w2, ln2_g, ln2_b):
    nb, s, d = x.shape
    for l in range(DEPTH):
        x1, x1p, logits = _mixer(x, w_in[l], conv_w[l], conv_b[l], lru_wa[l], lru_ba[l], lru_wx[l], lru_bx[l],
                                 lru_lambda[l], pool_w[l], pool_scale[l], w_out[l], ln1_g[l], ln1_b[l],
                                 router_w[l])
        t = nb * s
        x = _moe(x1.reshape(t, d), x1p.reshape(t, d // 2), logits.reshape(t, -1), router_bias[l],
                 exp_w1[l], exp_w3[l], exp_w2[l], sh_w1[l], sh_w3[l], sh_w2[l], ln2_g[l], ln2_b[l])
        x = x.reshape(nb, s, d)
    return x
```

```python
import functools
import math

import jax
import jax.numpy as jnp
from jax import lax
from jax.experimental import pallas as pl
from jax.experimental.pallas import tpu as pltpu
from jax.experimental.pallas import tpu_sc as plsc

LRU_HEADS = 8
CONV_WIDTH = 4
LRU_C = 8.0
POOL_WINDOWS = (2, 4, 8, 16)
N_EXPERT_GROUPS = 8
TOPK_GROUPS = 4
TOP_K = 8
ROUTED_SCALE = 2.5
LN_EPS = 1e-5
DEPTH = 1
DEEPNORM_ALPHA = (2.0 * DEPTH) ** 0.25

MXU_DIM = 256
POOL_GROUP_DIM = 128
CONV_HIST = 8
POOL_HIST = 16
GATE_LANES = 128
SC_CHUNK = 128
VMEM_LIMIT = 56 * 1024 * 1024

F32 = jnp.float32
BF16 = jnp.bfloat16


def _dot(a, b):
    return jnp.dot(a, b, preferred_element_type=F32)


def _layer_norm(z, g, b):
    mu = jnp.mean(z, axis=-1, keepdims=True)
    zc = z - mu
    var = jnp.mean(zc * zc, axis=-1, keepdims=True)
    return zc * lax.rsqrt(var + LN_EPS) * g + b


def _block_diag_dot(xb, w_ref):
    n = w_ref.shape[0]
    return jnp.concatenate(
        [_dot(xb[:, i * MXU_DIM:(i + 1) * MXU_DIM], w_ref[i]) for i in range(n)], axis=1)


def _mixer_kernel(x_ref, w_in_ref, conv_w_ref, conv_b_ref, wa_ref, ba_ref, wx_ref, bx_ref, lam_ref,
                  pool_w_ref, pool_scale_ref, w_out_ref, g1_ref, b1_ref, rw_hi_ref, rw_lo_ref,
                  x1_ref, x1p_ref, logit_ref, hist_u, hist_v, carry):
    j = pl.program_id(0)
    nb, ts, d = x_ref.shape
    rows = nb * ts
    c = hist_u.shape[-1]

    @pl.when(j == 0)
    def _():
        hist_u[...] = jnp.zeros_like(hist_u)
        hist_v[...] = jnp.zeros_like(hist_v)
        carry[...] = jnp.zeros_like(carry)

    x = x_ref[...].reshape(rows, d)
    proj = _dot(x.astype(BF16), w_in_ref[...])
    u = proj[:, :c].reshape(nb, ts, c)
    gate = proj[:, c:2 * c]
    v = proj[:, 2 * c:].reshape(nb, ts, c)

    ext_u = jnp.concatenate([hist_u[...], u], axis=1)
    hist_u[...] = u[:, ts - CONV_HIST:, :]
    cw = conv_w_ref[...]
    uc = jnp.broadcast_to(conv_b_ref[...].reshape(1, 1, c), (nb, ts, c))
    for k in range(CONV_WIDTH):
        off = CONV_HIST - (CONV_WIDTH - 1) + k
        uc = uc + cw[k:k + 1, :].reshape(1, 1, c) * ext_u[:, off:off + ts, :]
    ucf = uc.reshape(rows, c)
    ucb = ucf.astype(BF16)

    r = jax.nn.sigmoid(_block_diag_dot(ucb, wa_ref) + ba_ref[...])
    i_gate = jax.nn.sigmoid(_block_diag_dot(ucb, wx_ref) + bx_ref[...])
    lam = lam_ref[...]
    softplus_neg_lam = jnp.maximum(-lam, 0.0) + jnp.log1p(jnp.exp(-jnp.abs(lam)))
    log_a = (-LRU_C) * r * softplus_neg_lam
    a = jnp.exp(log_a)
    mult = jnp.sqrt(jnp.tanh(-log_a) * (1.0 + a * a))
    t_local = lax.broadcasted_iota(jnp.int32, (nb, ts, c), 1)
    first = (t_local + j * ts) == 0
    a3 = a.reshape(nb, ts, c)
    x3 = jnp.where(first, 1.0, mult.reshape(nb, ts, c)) * i_gate.reshape(nb, ts, c) * uc

    shift = 1
    while shift < ts:
        valid = t_local >= shift
        a_prev = pltpu.roll(a3, shift, axis=1)
        x_prev = pltpu.roll(x3, shift, axis=1)
        x3 = jnp.where(valid, a3 * x_prev, 0.0) + x3
        a3 = jnp.where(valid, a3 * a_prev, a3)
        shift *= 2
    h = x3 + a3 * carry[...][:, None, :]
    carry[...] = h[:, ts - 1, :]

    gelu_gate = 0.5 * gate * (1.0 + jnp.tanh(math.sqrt(2.0 / math.pi) * (gate + 0.044715 * gate * gate * gate)))
    y_lru = h.reshape(rows, c) * gelu_gate

    ext_v = jnp.concatenate([hist_v[...], v], axis=1)
    hist_v[...] = v[:, ts - POOL_HIST:, :]
    t_glob = lax.broadcasted_iota(jnp.int32, (1, ts, POOL_GROUP_DIM), 1) + j * ts
    pooled = []
    for g, win in enumerate(POOL_WINDOWS):
        lanes = slice(g * POOL_GROUP_DIM, (g + 1) * POOL_GROUP_DIM)
        s = ext_v[:, :, lanes]
        shift = 1
        while shift < win:
            s = s + pltpu.roll(s, shift, axis=1)
            shift *= 2
        count = jnp.minimum(t_glob + 1, win).astype(F32)
        pooled.append(s[:, POOL_HIST:, :] / count - v[:, :, lanes])
    pooled = jnp.concatenate(pooled, axis=-1).reshape(rows, c).astype(BF16)
    y_pool = _block_diag_dot(pooled, pool_w_ref) * pool_scale_ref[...]

    y = jnp.concatenate([y_lru, y_pool], axis=1).astype(BF16)
    mix = _dot(y, w_out_ref[...])
    x1 = _layer_norm(DEEPNORM_ALPHA * x + mix, g1_ref[...], b1_ref[...])
    x1_ref[...] = x1.reshape(nb, ts, d)

    hi = x1.astype(BF16)
    hi_f = hi.astype(F32)
    lo = (x1 - hi_f).astype(BF16)
    logits = _dot(hi, rw_hi_ref[...]) + _dot(lo, rw_hi_ref[...]) + _dot(hi, rw_lo_ref[...])
    logit_ref[...] = logits.reshape(nb, ts, logits.shape[-1])

    bits = lax.bitcast_convert_type(hi_f, jnp.uint32)
    packed = bits[:, :d // 2] | (bits[:, d // 2:] >> 16)
    x1p_ref[...] = packed.reshape(nb, ts, d // 2)


def _const_spec(shape):
    return pl.BlockSpec(shape, lambda j: (0,) * len(shape), pipeline_mode=pl.Buffered(1))


def _regroup_block_diag(w, width):
    heads, hd, _ = w.shape
    per = width // hd
    w = w.reshape(heads // per, per, hd, hd)
    eye = jnp.eye(per, dtype=w.dtype)
    return jnp.einsum("gpij,pq->gpiqj", w, eye).reshape(heads // per, width, width)


def _mixer(x, w_in, conv_w, conv_b, wa, ba, wx, bx, lam, pool_w, pool_scale, w_out, g1, b1, router_w, ts=64):
    nb, s, d = x.shape
    c = conv_w.shape[-1]
    n_exp = router_w.shape[-1]
    rw_hi = router_w.astype(BF16)
    rw_lo = (router_w - rw_hi.astype(F32)).astype(BF16)
    row = lambda p: p.reshape(1, -1)
    operands = (
        x, w_in.astype(BF16), conv_w, row(conv_b),
        _regroup_block_diag(wa, MXU_DIM).astype(BF16), row(ba),
        _regroup_block_diag(wx, MXU_DIM).astype(BF16), row(bx), row(lam),
        _regroup_block_diag(pool_w, MXU_DIM).astype(BF16), row(pool_scale),
        w_out.astype(BF16), row(g1), row(b1), rw_hi, rw_lo)
    tile = lambda width: pl.BlockSpec((nb, ts, width), lambda j: (0, j, 0))
    in_specs = [tile(d)] + [_const_spec(op.shape) for op in operands[1:]]
    return pl.pallas_call(
        _mixer_kernel,
        grid=(s // ts,),
        in_specs=in_specs,
        out_specs=[tile(d), tile(d // 2), tile(n_exp)],
        out_shape=[jax.ShapeDtypeStruct((nb, s, d), F32),
                   jax.ShapeDtypeStruct((nb, s, d // 2), jnp.uint32),
                   jax.ShapeDtypeStruct((nb, s, n_exp), F32)],
        scratch_shapes=[pltpu.VMEM((nb, CONV_HIST, c), F32),
                        pltpu.VMEM((nb, POOL_HIST, c), F32),
                        pltpu.VMEM((nb, c), F32)],
        compiler_params=pltpu.CompilerParams(dimension_semantics=("arbitrary",),
                                             vmem_limit_bytes=VMEM_LIMIT),
        name="mixer_ln_router",
    )(*operands)


def _first_row_of(cond, rows_rev, n):
    return n - jnp.max(jnp.where(cond, rows_rev, 0.0), axis=0, keepdims=True)


def _route_kernel(logit_ref, bias_ref, idx_ref, gate_ref, rank_ref, count_ref, count_scr):
    i = pl.program_id(0)

    @pl.when(i == 0)
    def _():
        count_scr[...] = jnp.zeros_like(count_scr)

    scores = jax.nn.sigmoid(logit_ref[...].T)
    n_exp, tn = scores.shape
    gsz = n_exp // N_EXPERT_GROUPS
    neg = -jnp.inf
    biased = scores + bias_ref[...]
    row = lax.broadcasted_iota(jnp.int32, (n_exp, tn), 0).astype(F32)
    row_rev = n_exp - row

    g_row = lax.broadcasted_iota(jnp.int32, (N_EXPERT_GROUPS, tn), 0)
    group_score = jnp.zeros((N_EXPERT_GROUPS, tn), F32)
    grp_row = lax.broadcasted_iota(jnp.int32, (gsz, tn), 0).astype(F32)
    grp_rev = gsz - grp_row
    for g in range(N_EXPERT_GROUPS):
        blk = biased[g * gsz:(g + 1) * gsz]
        m1 = jnp.max(blk, axis=0, keepdims=True)
        i1 = _first_row_of(blk == m1, grp_rev, gsz)
        m2 = jnp.max(jnp.where(grp_row == i1, neg, blk), axis=0, keepdims=True)
        group_score = jnp.where(g_row == g, m1 + m2, group_score)

    beaten_by = jnp.zeros((N_EXPERT_GROUPS, tn), F32)
    for g in range(N_EXPERT_GROUPS):
        other = jnp.max(jnp.where(g_row == g, group_score, neg), axis=0, keepdims=True)
        wins = (other > group_score) | ((other == group_score) & (g < g_row))
        beaten_by = beaten_by + jnp.where(wins, 1.0, 0.0)
    masked = []
    for g in range(N_EXPERT_GROUPS):
        beaten_g = jnp.max(jnp.where(g_row == g, beaten_by, 0.0), axis=0, keepdims=True)
        masked.append(jnp.where(beaten_g < TOPK_GROUPS, biased[g * gsz:(g + 1) * gsz], neg))
    masked = jnp.concatenate(masked, axis=0)

    k_row = lax.broadcasted_iota(jnp.int32, (TOP_K, tn), 0)
    selected = jnp.zeros((n_exp, tn), F32)
    idx = jnp.zeros((TOP_K, tn), F32)
    gate = jnp.zeros((TOP_K, tn), F32)
    hits = []
    for k in range(TOP_K):
        m = jnp.max(masked, axis=0, keepdims=True)
        ik = _first_row_of(masked == m, row_rev, n_exp)
        hit = row == ik
        gk = jnp.sum(jnp.where(hit, scores, 0.0), axis=0, keepdims=True)
        masked = jnp.where(hit, neg, masked)
        selected = jnp.where(hit, 1.0, selected)
        idx = jnp.where(k_row == k, ik, idx)
        gate = jnp.where(k_row == k, gk, gate)
        hits.append(hit)
    gate = gate / jnp.sum(gate, axis=0, keepdims=True) * ROUTED_SCALE

    earlier = (lax.broadcasted_iota(jnp.int32, (tn, tn), 0) < lax.broadcasted_iota(jnp.int32, (tn, tn), 1))
    before = _dot(selected.astype(BF16), jnp.where(earlier, 1.0, 0.0).astype(BF16)) + count_scr[...]
    rank = jnp.zeros((TOP_K, tn), F32)
    for k in range(TOP_K):
        rk = jnp.sum(jnp.where(hits[k], before, 0.0), axis=0, keepdims=True)
        rank = jnp.where(k_row == k, rk, rank)
    count_scr[...] = count_scr[...] + jnp.sum(selected, axis=1, keepdims=True)

    idx_ref[...] = idx.astype(jnp.int32)
    pad = jnp.zeros((gate_ref.shape[1] - TOP_K, tn), F32)
    gate_ref[...] = jnp.concatenate([gate, pad], axis=0).T
    rank_ref[...] = rank.astype(jnp.int32)
    count_ref[...] = count_scr[...].astype(jnp.int32)


def _route(logits, bias, tn=512):
    t, n_exp = logits.shape
    return pl.pallas_call(
        _route_kernel,
        grid=(t // tn,),
        in_specs=[pl.BlockSpec((tn, n_exp), lambda i: (i, 0)),
                  pl.BlockSpec((n_exp, 1), lambda i: (0, 0))],
        out_specs=[pl.BlockSpec((TOP_K, tn), lambda i: (0, i)),
                   pl.BlockSpec((tn, GATE_LANES), lambda i: (i, 0)),
                   pl.BlockSpec((TOP_K, tn), lambda i: (0, i)),
                   pl.BlockSpec((n_exp, 1), lambda i: (0, 0))],
        out_shape=[jax.ShapeDtypeStruct((TOP_K, t), jnp.int32),
                   jax.ShapeDtypeStruct((t, GATE_LANES), F32),
                   jax.ShapeDtypeStruct((TOP_K, t), jnp.int32),
                   jax.ShapeDtypeStruct((n_exp, 1), jnp.int32)],
        scratch_shapes=[pltpu.VMEM((n_exp, 1), F32)],
        compiler_params=pltpu.CompilerParams(dimension_semantics=("arbitrary",),
                                             vmem_limit_bytes=VMEM_LIMIT),
        name="route_topk_rank",
    )(logits, bias.reshape(n_exp, 1))


def _slot_kernel(idx_ref, rank_ref, start_ref, slot_ref):
    idx = idx_ref[...]
    n_exp = start_ref.shape[0]
    tn = idx.shape[1]
    row = lax.broadcasted_iota(jnp.int32, (n_exp, tn), 0)
    start = start_ref[...]
    first = [jnp.sum(jnp.where(row == idx[k:k + 1], start, 0), axis=0, keepdims=True) for k in range(TOP_K)]
    slots = jnp.concatenate(first, axis=0) + rank_ref[...]
    chunk = slot_ref.shape[-1]
    for c in range(slot_ref.shape[0]):
        slot_ref[c] = slots[:, c * chunk:(c + 1) * chunk]


def _slots(idx, rank, expert_start, tn=512):
    k, t = idx.shape
    n_exp = expert_start.shape[0]
    per_step = tn // SC_CHUNK
    return pl.pallas_call(
        _slot_kernel,
        grid=(t // tn,),
        in_specs=[pl.BlockSpec((k, tn), lambda i: (0, i)),
                  pl.BlockSpec((k, tn), lambda i: (0, i)),
                  pl.BlockSpec((n_exp, 1), lambda i: (0, 0))],
        out_specs=pl.BlockSpec((per_step, k, SC_CHUNK), lambda i: (i, 0, 0)),
        out_shape=jax.ShapeDtypeStruct((t // SC_CHUNK, k, SC_CHUNK), jnp.int32),
        compiler_params=pltpu.CompilerParams(dimension_semantics=("arbitrary",)),
        name="dispatch_slots",
    )(idx, rank, expert_start.reshape(n_exp, 1))


def _unpack_bf16_pairs(p):
    hi = lax.bitcast_convert_type(p & jnp.uint32(0xFFFF0000), F32)
    lo = lax.bitcast_convert_type(p << 16, F32)
    return jnp.concatenate([hi, lo], axis=1)


def _pack_bf16_pairs(y):
    n = y.shape[1] // 2
    bits = lax.bitcast_convert_type(y.astype(BF16).astype(F32), jnp.uint32)
    return bits[:, :n] | (bits[:, n:] >> 16)


def _sc_worker_layout(n_chunks_total):
    info = plsc.get_sparse_core_info()
    n_workers = info.num_cores * info.num_subcores
    return info.num_cores, n_chunks_total // n_workers


def _sc_dispatch(slots, x1p, n_rows):
    n_chunks_total, n_k, chunk = slots.shape
    t, w = x1p.shape
    n_cores, n_chunks = _sc_worker_layout(n_chunks_total)
    mesh = plsc.VectorSubcoreMesh(core_axis_name="c", subcore_axis_name="s")

    @functools.partial(
        pl.kernel, mesh=mesh, name="sc_dispatch_rows",
        out_type=jax.ShapeDtypeStruct((n_rows, w), x1p.dtype),
        scratch_types=[pltpu.VMEM((n_k, chunk), jnp.int32), pltpu.VMEM((chunk, w), x1p.dtype),
                       pltpu.SemaphoreType.DMA])
    def dispatch(slots_hbm, x_hbm, xg_hbm, idx_v, rows_v, sem):
        wid = lax.axis_index("s") * n_cores + lax.axis_index("c")

        @pl.loop(0, n_chunks)
        def _(ci):
            chunk_id = wid * n_chunks + ci
            pltpu.sync_copy(slots_hbm.at[chunk_id], idx_v)
            pltpu.sync_copy(x_hbm.at[pl.ds(chunk_id * chunk, chunk)], rows_v)
            copies = [pltpu.make_async_copy(rows_v, xg_hbm.at[idx_v.at[k]], sem) for k in range(n_k)]
            for cp in copies:
                cp.start()
            for cp in copies:
                cp.wait()

    return dispatch(slots, x1p)


def _sc_collect(slots, yg):
    n_chunks_total, n_k, chunk = slots.shape
    w = yg.shape[1]
    t = n_chunks_total * chunk
    n_cores, n_chunks = _sc_worker_layout(n_chunks_total)
    mesh = plsc.VectorSubcoreMesh(core_axis_name="c", subcore_axis_name="s")

    @functools.partial(
        pl.kernel, mesh=mesh, name="sc_collect_rows",
        out_type=jax.ShapeDtypeStruct((n_k, t, w), yg.dtype),
        scratch_types=[pltpu.VMEM((n_k, chunk), jnp.int32), pltpu.VMEM((chunk, w), yg.dtype),
                       pltpu.SemaphoreType.DMA])
    def collect(slots_hbm, yg_hbm, yt_hbm, idx_v, rows_v, sem):
        wid = lax.axis_index("s") * n_cores + lax.axis_index("c")

        @pl.loop(0, n_chunks)
        def _(ci):
            chunk_id = wid * n_chunks + ci
            pltpu.sync_copy(slots_hbm.at[chunk_id], idx_v)
            for k in range(n_k):
                pltpu.async_copy(yg_hbm.at[idx_v.at[k]], rows_v, sem).wait()
                pltpu.sync_copy(rows_v, yt_hbm.at[k, pl.ds(chunk_id * chunk, chunk)])

    return collect(slots, yg)


def _expert_kernel(first_block, n_block, n_used, xg_hbm, w1_ref, w3_ref, w2_ref, yg_hbm,
                   x_buf, y_buf, w1_bf, w3_bf, w2_bf, in_sem, out_sem):
    e = pl.program_id(0)
    bm = x_buf.shape[1]
    total = n_used[0]

    def fetch(g, slot):
        return pltpu.make_async_copy(xg_hbm.at[pl.ds(g * bm, bm)], x_buf.at[slot], in_sem.at[slot])

    def write_back(g, slot):
        return pltpu.make_async_copy(y_buf.at[slot], yg_hbm.at[pl.ds(g * bm, bm)], out_sem.at[slot])

    @pl.when(e == 0)
    def _():
        fetch(0, 0).start()

    @pl.when(n_block[e] > 0)
    def _():
        w1_bf[...] = w1_ref[0].astype(BF16)
        w3_bf[...] = w3_ref[0].astype(BF16)
        w2_bf[...] = w2_ref[0].astype(BF16)

    @pl.loop(0, n_block[e])
    def _(b):
        g = first_block[e] + b
        slot = g % 2
        fetch(g, slot).wait()

        @pl.when(g + 1 < total)
        def _():
            fetch(g + 1, 1 - slot).start()

        xb = _unpack_bf16_pairs(x_buf[slot]).astype(BF16)
        h1 = _dot(xb, w1_bf[...])
        h3 = _dot(xb, w3_bf[...])
        h = (h1 * jax.nn.sigmoid(h1) * h3).astype(BF16)
        y = _pack_bf16_pairs(_dot(h, w2_bf[...]))

        @pl.when(g >= 2)
        def _():
            write_back(g - 2, slot).wait()

        y_buf[slot] = y
        write_back(g, slot).start()

    @pl.when(e == pl.num_programs(0) - 1)
    def _():
        @pl.when(total >= 2)
        def _():
            write_back(total - 2, total % 2).wait()

        write_back(total - 1, (total - 1) % 2).wait()


def _experts(first_block, n_block, n_used, xg, w1, w3, w2, bm):
    n_rows, w = xg.shape
    n_exp, d, de = w1.shape
    weight = lambda e, *_: (e, 0, 0)
    return pl.pallas_call(
        _expert_kernel,
        grid_spec=pltpu.PrefetchScalarGridSpec(
            num_scalar_prefetch=3,
            grid=(n_exp,),
            in_specs=[pl.BlockSpec(memory_space=pl.ANY),
                      pl.BlockSpec((1, d, de), weight),
                      pl.BlockSpec((1, d, de), weight),
                      pl.BlockSpec((1, de, d), weight)],
            out_specs=pl.BlockSpec(memory_space=pl.ANY),
            scratch_shapes=[pltpu.VMEM((2, bm, w), xg.dtype), pltpu.VMEM((2, bm, w), jnp.uint32),
                            pltpu.VMEM((d, de), BF16), pltpu.VMEM((d, de), BF16), pltpu.VMEM((de, d), BF16),
                            pltpu.SemaphoreType.DMA((2,)), pltpu.SemaphoreType.DMA((2,))]),
        out_shape=jax.ShapeDtypeStruct((n_rows, w), jnp.uint32),
        compiler_params=pltpu.CompilerParams(dimension_semantics=("arbitrary",),
                                             vmem_limit_bytes=VMEM_LIMIT),
        name="routed_experts",
    )(first_block, n_block, n_used, xg, w1, w3, w2)


def _combine_kernel(yt_ref, x1_ref, gate_ref, sw1_ref, sw3_ref, sw2_ref, g2_ref, b2_ref, out_ref):
    x1 = x1_ref[...]
    xb = x1.astype(BF16)
    h1 = _dot(xb, sw1_ref[...])
    h3 = _dot(xb, sw3_ref[...])
    ffn = _dot((h1 * jax.nn.sigmoid(h1) * h3).astype(BF16), sw2_ref[...])
    gate = gate_ref[...]
    for k in range(yt_ref.shape[0]):
        ffn = ffn + gate[:, k:k + 1] * _unpack_bf16_pairs(yt_ref[k])
    out_ref[...] = _layer_norm(DEEPNORM_ALPHA * x1 + ffn, g2_ref[...], b2_ref[...])


def _combine(yt, x1, gate, sw1, sw3, sw2, g2, b2, tn=256):
    n_k, t, w = yt.shape
    d = x1.shape[1]
    row = lambda p: p.reshape(1, -1)
    operands = (yt, x1, gate, sw1.astype(BF16), sw3.astype(BF16), sw2.astype(BF16), row(g2), row(b2))
    in_specs = [pl.BlockSpec((n_k, tn, w), lambda i: (0, i, 0)),
                pl.BlockSpec((tn, d), lambda i: (i, 0)),
                pl.BlockSpec((tn, gate.shape[1]), lambda i: (i, 0))]
    in_specs += [_const_spec(op.shape) for op in operands[3:]]
    return pl.pallas_call(
        _combine_kernel,
        grid=(t // tn,),
        in_specs=in_specs,
        out_specs=pl.BlockSpec((tn, d), lambda i: (i, 0)),
        out_shape=jax.ShapeDtypeStruct((t, d), F32),
        compiler_params=pltpu.CompilerParams(dimension_semantics=("arbitrary",),
                                             vmem_limit_bytes=VMEM_LIMIT),
        name="combine_shared_ln",
    )(*operands)


def _block_table(counts, bm):
    n_block = (counts + bm - 1) // bm
    ends = jnp.cumsum(n_block)
    first_block = ends - n_block
    i32 = lambda v: v.astype(jnp.int32)
    return i32(first_block * bm), i32(first_block), i32(n_block), i32(ends[-1:])


def _moe(x1, x1p, logits, router_bias, w1, w3, w2, sw1, sw3, sw2, g2, b2, bm=256):
    t, d = x1.shape
    n_exp = logits.shape[1]
    idx, gate, rank, counts = _route(logits, router_bias)
    n_blocks = (t * TOP_K) // bm + n_exp
    starts, first_block, n_block, n_used = _block_table(counts[:, 0], bm)
    slots = _slots(idx, rank, starts)
    xg = _sc_dispatch(slots, x1p, n_blocks * bm)
    yg = _experts(first_block, n_block, n_used, xg, w1, w3, w2, bm)
    yt = _sc_collect(slots, yg)
    return _combine(yt, x1, gate, sw1, sw3, sw2, g2, b2)


def kernel(x, w_in, conv_w, conv_b, lru_wa, lru_ba, lru_wx, lru_bx, lru_lambda, pool_w, pool_scale, w_out, ln1_g, ln1_b, router_w, router_bias, exp_w1, exp_w3, exp_w2, sh_w1, sh_w3, sh_w2, ln2_g, ln2_b):
    nb, s, d = x.shape
    for l in range(DEPTH):
        x1, x1p, logits = _mixer(x, w_in[l], conv_w[l], conv_b[l], lru_wa[l], lru_ba[l], lru_wx[l], lru_bx[l],
                                 lru_lambda[l], pool_w[l], pool_scale[l], w_out[l], ln1_g[l], ln1_b[l],
                                 router_w[l])
        t = nb * s
        x = _moe(x1.reshape(t, d), x1p.reshape(t, d // 2), logits.reshape(t, -1), router_bias[l],
                 exp_w1[l], exp_w3[l], exp_w2[l], sh_w1[l], sh_w3[l], sh_w2[l], ln2_g[l], ln2_b[l])
        x = x.reshape(nb, s, d)
    return x
```

```python
import functools
import math

import jax
import jax.numpy as jnp
from jax import lax
from jax.experimental import pallas as pl
from jax.experimental.pallas import tpu as pltpu
from jax.experimental.pallas import tpu_sc as plsc

LRU_HEADS = 8
CONV_WIDTH = 4
LRU_C = 8.0
POOL_WINDOWS = (2, 4, 8, 16)
N_EXPERT_GROUPS = 8
TOPK_GROUPS = 4
TOP_K = 8
ROUTED_SCALE = 2.5
LN_EPS = 1e-5
DEPTH = 1
DEEPNORM_ALPHA = (2.0 * DEPTH) ** 0.25

MXU_DIM = 256
POOL_GROUP_DIM = 128
CONV_HIST = 8
POOL_HIST = 16
GATE_LANES = 128
SC_CHUNK = 128
ROW_RING = 4
VMEM_LIMIT = 56 * 1024 * 1024

F32 = jnp.float32
BF16 = jnp.bfloat16


def _dot(a, b):
    return jnp.dot(a, b, preferred_element_type=F32)


def _layer_norm(z, g, b):
    mu = jnp.mean(z, axis=-1, keepdims=True)
    zc = z - mu
    var = jnp.mean(zc * zc, axis=-1, keepdims=True)
    return zc * lax.rsqrt(var + LN_EPS) * g + b


def _block_diag_dot(xb, w_ref):
    n = w_ref.shape[0]
    return jnp.concatenate(
        [_dot(xb[:, i * MXU_DIM:(i + 1) * MXU_DIM], w_ref[i]) for i in range(n)], axis=1)


def _mixer_kernel(x_ref, w_in_ref, conv_w_ref, conv_b_ref, wa_ref, ba_ref, wx_ref, bx_ref, lam_ref,
                  pool_w_ref, pool_scale_ref, w_out_ref, g1_ref, b1_ref, rw_hi_ref, rw_lo_ref,
                  x1_ref, x1p_ref, logit_ref, hist_u, hist_v, carry):
    j = pl.program_id(0)
    nb, ts, d = x_ref.shape
    rows = nb * ts
    c = hist_u.shape[-1]

    @pl.when(j == 0)
    def _():
        hist_u[...] = jnp.zeros_like(hist_u)
        hist_v[...] = jnp.zeros_like(hist_v)
        carry[...] = jnp.zeros_like(carry)

    x = x_ref[...].reshape(rows, d)
    proj = _dot(x.astype(BF16), w_in_ref[...])
    u = proj[:, :c].reshape(nb, ts, c)
    gate = proj[:, c:2 * c]
    v = proj[:, 2 * c:].reshape(nb, ts, c)

    ext_u = jnp.concatenate([hist_u[...], u], axis=1)
    hist_u[...] = u[:, ts - CONV_HIST:, :]
    cw = conv_w_ref[...]
    uc = jnp.broadcast_to(conv_b_ref[...].reshape(1, 1, c), (nb, ts, c))
    for k in range(CONV_WIDTH):
        off = CONV_HIST - (CONV_WIDTH - 1) + k
        uc = uc + cw[k:k + 1, :].reshape(1, 1, c) * ext_u[:, off:off + ts, :]
    ucf = uc.reshape(rows, c)
    ucb = ucf.astype(BF16)

    r = jax.nn.sigmoid(_block_diag_dot(ucb, wa_ref) + ba_ref[...])
    i_gate = jax.nn.sigmoid(_block_diag_dot(ucb, wx_ref) + bx_ref[...])
    lam = lam_ref[...]
    softplus_neg_lam = jnp.maximum(-lam, 0.0) + jnp.log1p(jnp.exp(-jnp.abs(lam)))
    log_a = (-LRU_C) * r * softplus_neg_lam
    a = jnp.exp(log_a)
    mult = jnp.sqrt(jnp.tanh(-log_a) * (1.0 + a * a))
    t_local = lax.broadcasted_iota(jnp.int32, (nb, ts, c), 1)
    first = (t_local + j * ts) == 0
    a3 = a.reshape(nb, ts, c)
    x3 = jnp.where(first, 1.0, mult.reshape(nb, ts, c)) * i_gate.reshape(nb, ts, c) * uc

    shift = 1
    while shift < ts:
        valid = t_local >= shift
        a_prev = pltpu.roll(a3, shift, axis=1)
        x_prev = pltpu.roll(x3, shift, axis=1)
        x3 = jnp.where(valid, a3 * x_prev, 0.0) + x3
        a3 = jnp.where(valid, a3 * a_prev, a3)
        shift *= 2
    h = x3 + a3 * carry[...][:, None, :]
    carry[...] = h[:, ts - 1, :]

    gelu_gate = 0.5 * gate * (1.0 + jnp.tanh(math.sqrt(2.0 / math.pi) * (gate + 0.044715 * gate * gate * gate)))
    y_lru = h.reshape(rows, c) * gelu_gate

    ext_v = jnp.concatenate([hist_v[...], v], axis=1)
    hist_v[...] = v[:, ts - POOL_HIST:, :]
    t_glob = lax.broadcasted_iota(jnp.int32, (1, ts, POOL_GROUP_DIM), 1) + j * ts
    pooled = []
    for g, win in enumerate(POOL_WINDOWS):
        lanes = slice(g * POOL_GROUP_DIM, (g + 1) * POOL_GROUP_DIM)
        s = ext_v[:, :, lanes]
        shift = 1
        while shift < win:
            s = s + pltpu.roll(s, shift, axis=1)
            shift *= 2
        count = jnp.minimum(t_glob + 1, win).astype(F32)
        pooled.append(s[:, POOL_HIST:, :] / count - v[:, :, lanes])
    pooled = jnp.concatenate(pooled, axis=-1).reshape(rows, c).astype(BF16)
    y_pool = _block_diag_dot(pooled, pool_w_ref) * pool_scale_ref[...]

    y = jnp.concatenate([y_lru, y_pool], axis=1).astype(BF16)
    mix = _dot(y, w_out_ref[...])
    x1 = _layer_norm(DEEPNORM_ALPHA * x + mix, g1_ref[...], b1_ref[...])
    x1_ref[...] = x1.reshape(nb, ts, d)

    hi = x1.astype(BF16)
    hi_f = hi.astype(F32)
    lo = (x1 - hi_f).astype(BF16)
    logits = _dot(hi, rw_hi_ref[...]) + _dot(lo, rw_hi_ref[...]) + _dot(hi, rw_lo_ref[...])
    logit_ref[...] = logits.reshape(nb, ts, logits.shape[-1])

    bits = lax.bitcast_convert_type(hi_f, jnp.uint32)
    packed = bits[:, :d // 2] | (bits[:, d // 2:] >> 16)
    x1p_ref[...] = packed.reshape(nb, ts, d // 2)


def _const_spec(shape):
    return pl.BlockSpec(shape, lambda j: (0,) * len(shape), pipeline_mode=pl.Buffered(1))


def _regroup_block_diag(w, width):
    heads, hd, _ = w.shape
    per = width // hd
    w = w.reshape(heads // per, per, hd, hd)
    eye = jnp.eye(per, dtype=w.dtype)
    return jnp.einsum("gpij,pq->gpiqj", w, eye).reshape(heads // per, width, width)


def _mixer(x, w_in, conv_w, conv_b, wa, ba, wx, bx, lam, pool_w, pool_scale, w_out, g1, b1, router_w, ts=64):
    nb, s, d = x.shape
    c = conv_w.shape[-1]
    n_exp = router_w.shape[-1]
    rw_hi = router_w.astype(BF16)
    rw_lo = (router_w - rw_hi.astype(F32)).astype(BF16)
    row = lambda p: p.reshape(1, -1)
    operands = (
        x, w_in.astype(BF16), conv_w, row(conv_b),
        _regroup_block_diag(wa, MXU_DIM).astype(BF16), row(ba),
        _regroup_block_diag(wx, MXU_DIM).astype(BF16), row(bx), row(lam),
        _regroup_block_diag(pool_w, MXU_DIM).astype(BF16), row(pool_scale),
        w_out.astype(BF16), row(g1), row(b1), rw_hi, rw_lo)
    tile = lambda width: pl.BlockSpec((nb, ts, width), lambda j: (0, j, 0))
    in_specs = [tile(d)] + [_const_spec(op.shape) for op in operands[1:]]
    return pl.pallas_call(
        _mixer_kernel,
        grid=(s // ts,),
        in_specs=in_specs,
        out_specs=[tile(d), tile(d // 2), tile(n_exp)],
        out_shape=[jax.ShapeDtypeStruct((nb, s, d), F32),
                   jax.ShapeDtypeStruct((nb, s, d // 2), jnp.uint32),
                   jax.ShapeDtypeStruct((nb, s, n_exp), F32)],
        scratch_shapes=[pltpu.VMEM((nb, CONV_HIST, c), F32),
                        pltpu.VMEM((nb, POOL_HIST, c), F32),
                        pltpu.VMEM((nb, c), F32)],
        compiler_params=pltpu.CompilerParams(dimension_semantics=("arbitrary",),
                                             vmem_limit_bytes=VMEM_LIMIT),
        name="mixer_ln_router",
    )(*operands)


def _first_row_of(cond, rows_rev, n):
    return n - jnp.max(jnp.where(cond, rows_rev, 0.0), axis=0, keepdims=True)


def _route_kernel(logit_ref, bias_ref, idx_ref, gate_ref, rank_ref, count_ref, count_scr):
    i = pl.program_id(0)

    @pl.when(i == 0)
    def _():
        count_scr[...] = jnp.zeros_like(count_scr)

    scores = jax.nn.sigmoid(logit_ref[...].T)
    n_exp, tn = scores.shape
    gsz = n_exp // N_EXPERT_GROUPS
    neg = -jnp.inf
    biased = scores + bias_ref[...]
    row = lax.broadcasted_iota(jnp.int32, (n_exp, tn), 0).astype(F32)
    row_rev = n_exp - row

    g_row = lax.broadcasted_iota(jnp.int32, (N_EXPERT_GROUPS, tn), 0)
    group_score = jnp.zeros((N_EXPERT_GROUPS, tn), F32)
    grp_row = lax.broadcasted_iota(jnp.int32, (gsz, tn), 0).astype(F32)
    grp_rev = gsz - grp_row
    for g in range(N_EXPERT_GROUPS):
        blk = biased[g * gsz:(g + 1) * gsz]
        m1 = jnp.max(blk, axis=0, keepdims=True)
        i1 = _first_row_of(blk == m1, grp_rev, gsz)
        m2 = jnp.max(jnp.where(grp_row == i1, neg, blk), axis=0, keepdims=True)
        group_score = jnp.where(g_row == g, m1 + m2, group_score)

    beaten_by = jnp.zeros((N_EXPERT_GROUPS, tn), F32)
    for g in range(N_EXPERT_GROUPS):
        other = jnp.max(jnp.where(g_row == g, group_score, neg), axis=0, keepdims=True)
        wins = (other > group_score) | ((other == group_score) & (g < g_row))
        beaten_by = beaten_by + jnp.where(wins, 1.0, 0.0)
    masked = []
    for g in range(N_EXPERT_GROUPS):
        beaten_g = jnp.max(jnp.where(g_row == g, beaten_by, 0.0), axis=0, keepdims=True)
        masked.append(jnp.where(beaten_g < TOPK_GROUPS, biased[g * gsz:(g + 1) * gsz], neg))
    masked = jnp.concatenate(masked, axis=0)

    k_row = lax.broadcasted_iota(jnp.int32, (TOP_K, tn), 0)
    selected = jnp.zeros((n_exp, tn), F32)
    idx = jnp.zeros((TOP_K, tn), F32)
    gate = jnp.zeros((TOP_K, tn), F32)
    hits = []
    for k in range(TOP_K):
        m = jnp.max(masked, axis=0, keepdims=True)
        ik = _first_row_of(masked == m, row_rev, n_exp)
        hit = row == ik
        gk = jnp.sum(jnp.where(hit, scores, 0.0), axis=0, keepdims=True)
        masked = jnp.where(hit, neg, masked)
        selected = jnp.where(hit, 1.0, selected)
        idx = jnp.where(k_row == k, ik, idx)
        gate = jnp.where(k_row == k, gk, gate)
        hits.append(hit)
    gate = gate / jnp.sum(gate, axis=0, keepdims=True) * ROUTED_SCALE

    earlier = (lax.broadcasted_iota(jnp.int32, (tn, tn), 0) < lax.broadcasted_iota(jnp.int32, (tn, tn), 1))
    before = _dot(selected.astype(BF16), jnp.where(earlier, 1.0, 0.0).astype(BF16)) + count_scr[...]
    rank = jnp.zeros((TOP_K, tn), F32)
    for k in range(TOP_K):
        rk = jnp.sum(jnp.where(hits[k], before, 0.0), axis=0, keepdims=True)
        rank = jnp.where(k_row == k, rk, rank)
    count_scr[...] = count_scr[...] + jnp.sum(selected, axis=1, keepdims=True)

    idx_ref[...] = idx.astype(jnp.int32)
    pad = jnp.zeros((gate_ref.shape[1] - TOP_K, tn), F32)
    gate_ref[...] = jnp.concatenate([gate, pad], axis=0).T
    rank_ref[...] = rank.astype(jnp.int32)
    count_ref[...] = count_scr[...].astype(jnp.int32)


def _route(logits, bias, tn=512):
    t, n_exp = logits.shape
    return pl.pallas_call(
        _route_kernel,
        grid=(t // tn,),
        in_specs=[pl.BlockSpec((tn, n_exp), lambda i: (i, 0)),
                  pl.BlockSpec((n_exp, 1), lambda i: (0, 0))],
        out_specs=[pl.BlockSpec((TOP_K, tn), lambda i: (0, i)),
                   pl.BlockSpec((tn, GATE_LANES), lambda i: (i, 0)),
                   pl.BlockSpec((TOP_K, tn), lambda i: (0, i)),
                   pl.BlockSpec((n_exp, 1), lambda i: (0, 0))],
        out_shape=[jax.ShapeDtypeStruct((TOP_K, t), jnp.int32),
                   jax.ShapeDtypeStruct((t, GATE_LANES), F32),
                   jax.ShapeDtypeStruct((TOP_K, t), jnp.int32),
                   jax.ShapeDtypeStruct((n_exp, 1), jnp.int32)],
        scratch_shapes=[pltpu.VMEM((n_exp, 1), F32)],
        compiler_params=pltpu.CompilerParams(dimension_semantics=("arbitrary",),
                                             vmem_limit_bytes=VMEM_LIMIT),
        name="route_topk_rank",
    )(logits, bias.reshape(n_exp, 1))


def _slot_kernel(idx_ref, rank_ref, start_ref, slot_ref):
    idx = idx_ref[...]
    n_exp = start_ref.shape[0]
    tn = idx.shape[1]
    row = lax.broadcasted_iota(jnp.int32, (n_exp, tn), 0)
    start = start_ref[...]
    first = [jnp.sum(jnp.where(row == idx[k:k + 1], start, 0), axis=0, keepdims=True) for k in range(TOP_K)]
    slots = jnp.concatenate(first, axis=0) + rank_ref[...]
    chunk = slot_ref.shape[-1]
    for c in range(slot_ref.shape[0]):
        slot_ref[c] = slots[:, c * chunk:(c + 1) * chunk]


def _slots(idx, rank, expert_start, tn=512):
    k, t = idx.shape
    n_exp = expert_start.shape[0]
    per_step = tn // SC_CHUNK
    return pl.pallas_call(
        _slot_kernel,
        grid=(t // tn,),
        in_specs=[pl.BlockSpec((k, tn), lambda i: (0, i)),
                  pl.BlockSpec((k, tn), lambda i: (0, i)),
                  pl.BlockSpec((n_exp, 1), lambda i: (0, 0))],
        out_specs=pl.BlockSpec((per_step, k, SC_CHUNK), lambda i: (i, 0, 0)),
        out_shape=jax.ShapeDtypeStruct((t // SC_CHUNK, k, SC_CHUNK), jnp.int32),
        compiler_params=pltpu.CompilerParams(dimension_semantics=("arbitrary",)),
        name="dispatch_slots",
    )(idx, rank, expert_start.reshape(n_exp, 1))


def _unpack_bf16_pairs(p):
    hi = lax.bitcast_convert_type(p & jnp.uint32(0xFFFF0000), F32)
    lo = lax.bitcast_convert_type(p << 16, F32)
    return jnp.concatenate([hi, lo], axis=1)


def _pack_bf16_pairs(y):
    n = y.shape[1] // 2
    bits = lax.bitcast_convert_type(y.astype(BF16).astype(F32), jnp.uint32)
    return bits[:, :n] | (bits[:, n:] >> 16)


def _sc_worker_layout(n_chunks_total):
    info = plsc.get_sparse_core_info()
    n_workers = info.num_cores * info.num_subcores
    return info.num_cores, n_chunks_total // n_workers


def _sc_dispatch(slots, x1p, n_rows):
    n_chunks_total, n_k, chunk = slots.shape
    t, w = x1p.shape
    n_cores, n_chunks = _sc_worker_layout(n_chunks_total)
    mesh = plsc.VectorSubcoreMesh(core_axis_name="c", subcore_axis_name="s")

    @functools.partial(
        pl.kernel, mesh=mesh, name="sc_dispatch_rows",
        out_type=jax.ShapeDtypeStruct((n_rows, w), x1p.dtype),
        scratch_types=[pltpu.VMEM((n_k, chunk), jnp.int32), pltpu.VMEM((chunk, w), x1p.dtype),
                       pltpu.SemaphoreType.DMA])
    def dispatch(slots_hbm, x_hbm, xg_hbm, idx_v, rows_v, sem):
        wid = lax.axis_index("s") * n_cores + lax.axis_index("c")

        @pl.loop(0, n_chunks)
        def _(ci):
            chunk_id = wid * n_chunks + ci
            pltpu.sync_copy(slots_hbm.at[chunk_id], idx_v)
            pltpu.sync_copy(x_hbm.at[pl.ds(chunk_id * chunk, chunk)], rows_v)
            copies = [pltpu.make_async_copy(rows_v, xg_hbm.at[idx_v.at[k]], sem) for k in range(n_k)]
            for cp in copies:
                cp.start()
            for cp in copies:
                cp.wait()

    return dispatch(slots, x1p)


def _sc_collect(slots, yg):
    n_chunks_total, n_k, chunk = slots.shape
    w = yg.shape[1]
    t = n_chunks_total * chunk
    n_cores, n_chunks = _sc_worker_layout(n_chunks_total)
    mesh = plsc.VectorSubcoreMesh(core_axis_name="c", subcore_axis_name="s")

    @functools.partial(
        pl.kernel, mesh=mesh, name="sc_collect_rows",
        out_type=jax.ShapeDtypeStruct((n_k, t, w), yg.dtype),
        scratch_types=[pltpu.VMEM((n_k, chunk), jnp.int32), pltpu.VMEM((chunk, w), yg.dtype),
                       pltpu.SemaphoreType.DMA])
    def collect(slots_hbm, yg_hbm, yt_hbm, idx_v, rows_v, sem):
        wid = lax.axis_index("s") * n_cores + lax.axis_index("c")

        @pl.loop(0, n_chunks)
        def _(ci):
            chunk_id = wid * n_chunks + ci
            pltpu.sync_copy(slots_hbm.at[chunk_id], idx_v)
            for k in range(n_k):
                pltpu.async_copy(yg_hbm.at[idx_v.at[k]], rows_v, sem).wait()
                pltpu.sync_copy(rows_v, yt_hbm.at[k, pl.ds(chunk_id * chunk, chunk)])

    return collect(slots, yg)


def _expert_kernel(first_block, n_block, n_used, xg_hbm, w1_ref, w3_ref, w2_ref, yg_hbm,
                   x_buf, y_buf, w1_bf, w3_bf, w2_bf, in_sem, out_sem):
    e = pl.program_id(0)
    n_slot, bm, _ = x_buf.shape
    ahead = n_slot - 1
    total = n_used[0]

    def fetch(g):
        slot = g % n_slot
        return pltpu.make_async_copy(xg_hbm.at[pl.ds(g * bm, bm)], x_buf.at[slot], in_sem.at[slot])

    def write_back(g):
        slot = g % n_slot
        return pltpu.make_async_copy(y_buf.at[slot], yg_hbm.at[pl.ds(g * bm, bm)], out_sem.at[slot])

    @pl.when(e == 0)
    def _():
        for g in range(ahead):
            @pl.when(g < total)
            def _():
                fetch(g).start()

    @pl.when(n_block[e] > 0)
    def _():
        w1_bf[...] = w1_ref[0].astype(BF16)
        w3_bf[...] = w3_ref[0].astype(BF16)
        w2_bf[...] = w2_ref[0].astype(BF16)

    @pl.loop(0, n_block[e])
    def _(b):
        g = first_block[e] + b
        slot = g % n_slot
        fetch(g).wait()

        @pl.when(g + ahead < total)
        def _():
            fetch(g + ahead).start()

        xb = _unpack_bf16_pairs(x_buf[slot]).astype(BF16)
        h1 = _dot(xb, w1_bf[...])
        h3 = _dot(xb, w3_bf[...])
        h = (h1 * jax.nn.sigmoid(h1) * h3).astype(BF16)
        y = _pack_bf16_pairs(_dot(h, w2_bf[...]))

        @pl.when(g >= n_slot)
        def _():
            write_back(g - n_slot).wait()

        y_buf[slot] = y
        write_back(g).start()

    @pl.when(e == pl.num_programs(0) - 1)
    def _():
        for back in range(1, n_slot + 1):
            @pl.when(total >= back)
            def _():
                write_back(total - back).wait()


def _experts(first_block, n_block, n_used, xg, w1, w3, w2, bm):
    n_rows, w = xg.shape
    n_exp, d, de = w1.shape
    weight = lambda e, *_: (e, 0, 0)
    return pl.pallas_call(
        _expert_kernel,
        grid_spec=pltpu.PrefetchScalarGridSpec(
            num_scalar_prefetch=3,
            grid=(n_exp,),
            in_specs=[pl.BlockSpec(memory_space=pl.ANY),
                      pl.BlockSpec((1, d, de), weight),
                      pl.BlockSpec((1, d, de), weight),
                      pl.BlockSpec((1, de, d), weight)],
            out_specs=pl.BlockSpec(memory_space=pl.ANY),
            scratch_shapes=[pltpu.VMEM((ROW_RING, bm, w), xg.dtype), pltpu.VMEM((ROW_RING, bm, w), jnp.uint32),
                            pltpu.VMEM((d, de), BF16), pltpu.VMEM((d, de), BF16), pltpu.VMEM((de, d), BF16),
                            pltpu.SemaphoreType.DMA((ROW_RING,)), pltpu.SemaphoreType.DMA((ROW_RING,))]),
        out_shape=jax.ShapeDtypeStruct((n_rows, w), jnp.uint32),
        compiler_params=pltpu.CompilerParams(dimension_semantics=("arbitrary",),
                                             vmem_limit_bytes=VMEM_LIMIT),
        name="routed_experts",
    )(first_block, n_block, n_used, xg, w1, w3, w2)


def _combine_kernel(yt_ref, x1_ref, gate_ref, sw1_ref, sw3_ref, sw2_ref, g2_ref, b2_ref, out_ref):
    x1 = x1_ref[...]
    xb = x1.astype(BF16)
    h1 = _dot(xb, sw1_ref[...])
    h3 = _dot(xb, sw3_ref[...])
    ffn = _dot((h1 * jax.nn.sigmoid(h1) * h3).astype(BF16), sw2_ref[...])
    gate = gate_ref[...]
    for k in range(yt_ref.shape[0]):
        ffn = ffn + gate[:, k:k + 1] * _unpack_bf16_pairs(yt_ref[k])
    out_ref[...] = _layer_norm(DEEPNORM_ALPHA * x1 + ffn, g2_ref[...], b2_ref[...])


def _combine(yt, x1, gate, sw1, sw3, sw2, g2, b2, tn=256):
    n_k, t, w = yt.shape
    d = x1.shape[1]
    row = lambda p: p.reshape(1, -1)
    operands = (yt, x1, gate, sw1.astype(BF16), sw3.astype(BF16), sw2.astype(BF16), row(g2), row(b2))
    in_specs = [pl.BlockSpec((n_k, tn, w), lambda i: (0, i, 0)),
                pl.BlockSpec((tn, d), lambda i: (i, 0)),
                pl.BlockSpec((tn, gate.shape[1]), lambda i: (i, 0))]
    in_specs += [_const_spec(op.shape) for op in operands[3:]]
    return pl.pallas_call(
        _combine_kernel,
        grid=(t // tn,),
        in_specs=in_specs,
        out_specs=pl.BlockSpec((tn, d), lambda i: (i, 0)),
        out_shape=jax.ShapeDtypeStruct((t, d), F32),
        compiler_params=pltpu.CompilerParams(dimension_semantics=("arbitrary",),
                                             vmem_limit_bytes=VMEM_LIMIT),
        name="combine_shared_ln",
    )(*operands)


def _block_table(counts, bm):
    n_block = (counts + bm - 1) // bm
    ends = jnp.cumsum(n_block)
    first_block = ends - n_block
    i32 = lambda v: v.astype(jnp.int32)
    return i32(first_block * bm), i32(first_block), i32(n_block), i32(ends[-1:])


def _moe(x1, x1p, logits, router_bias, w1, w3, w2, sw1, sw3, sw2, g2, b2, bm=256):
    t, d = x1.shape
    n_exp = logits.shape[1]
    idx, gate, rank, counts = _route(logits, router_bias)
    n_blocks = (t * TOP_K) // bm + n_exp
    starts, first_block, n_block, n_used = _block_table(counts[:, 0], bm)
    slots = _slots(idx, rank, starts)
    xg = _sc_dispatch(slots, x1p, n_blocks * bm)
    yg = _experts(first_block, n_block, n_used, xg, w1, w3, w2, bm)
    yt = _sc_collect(slots, yg)
    return _combine(yt, x1, gate, sw1, sw3, sw2, g2, b2)


def kernel(x, w_in, conv_w, conv_b, lru_wa, lru_ba, lru_wx, lru_bx, lru_lambda, pool_w, pool_scale, w_out, ln1_g, ln1_b, router_w, router_bias, exp_w1, exp_w3, exp_w2, sh_w1, sh_w3, sh_w2, ln2_g, ln2_b):
    nb, s, d = x.shape
    for l in range(DEPTH):
        x1, x1p, logits = _mixer(x, w_in[l], conv_w[l], conv_b[l], lru_wa[l], lru_ba[l], lru_wx[l], lru_bx[l],
                                 lru_lambda[l], pool_w[l], pool_scale[l], w_out[l], ln1_g[l], ln1_b[l],
                                 router_w[l])
        t = nb * s
        x = _moe(x1.reshape(t, d), x1p.reshape(t, d // 2), logits.reshape(t, -1), router_bias[l],
                 exp_w1[l], exp_w3[l], exp_w2[l], sh_w1[l], sh_w3[l], sh_w2[l], ln2_g[l], ln2_b[l])
        x = x.reshape(nb, s, d)
    return x
```

```python
import functools
import math

import jax
import jax.numpy as jnp
from jax import lax
from jax.experimental import pallas as pl
from jax.experimental.pallas import tpu as pltpu
from jax.experimental.pallas import tpu_sc as plsc

LRU_HEADS = 8
CONV_WIDTH = 4
LRU_C = 8.0
POOL_WINDOWS = (2, 4, 8, 16)
N_EXPERT_GROUPS = 8
TOPK_GROUPS = 4
TOP_K = 8
ROUTED_SCALE = 2.5
LN_EPS = 1e-5
DEPTH = 1
DEEPNORM_ALPHA = (2.0 * DEPTH) ** 0.25

MXU_DIM = 256
POOL_GROUP_DIM = 128
CONV_HIST = 8
POOL_HIST = 16
GATE_LANES = 128
SC_CHUNK = 128
ROW_RING = 4
TOKEN_PARTS = 2
VMEM_LIMIT = 56 * 1024 * 1024

F32 = jnp.float32
BF16 = jnp.bfloat16


def _dot(a, b):
    return jnp.dot(a, b, preferred_element_type=F32)


def _layer_norm(z, g, b):
    mu = jnp.mean(z, axis=-1, keepdims=True)
    zc = z - mu
    var = jnp.mean(zc * zc, axis=-1, keepdims=True)
    return zc * lax.rsqrt(var + LN_EPS) * g + b


def _block_diag_dot(xb, w_ref):
    n = w_ref.shape[0]
    return jnp.concatenate(
        [_dot(xb[:, i * MXU_DIM:(i + 1) * MXU_DIM], w_ref[i]) for i in range(n)], axis=1)


def _mixer_kernel(x_ref, w_in_ref, conv_w_ref, conv_b_ref, wa_ref, ba_ref, wx_ref, bx_ref, lam_ref,
                  pool_w_ref, pool_scale_ref, w_out_ref, g1_ref, b1_ref, rw_hi_ref, rw_lo_ref,
                  x1_ref, x1p_ref, logit_ref, hist_u, hist_v, carry):
    j = pl.program_id(0)
    nb, ts, d = x_ref.shape
    rows = nb * ts
    c = hist_u.shape[-1]

    @pl.when(j == 0)
    def _():
        hist_u[...] = jnp.zeros_like(hist_u)
        hist_v[...] = jnp.zeros_like(hist_v)
        carry[...] = jnp.zeros_like(carry)

    x = x_ref[...].reshape(rows, d)
    proj = _dot(x.astype(BF16), w_in_ref[...])
    u = proj[:, :c].reshape(nb, ts, c)
    gate = proj[:, c:2 * c]
    v = proj[:, 2 * c:].reshape(nb, ts, c)

    ext_u = jnp.concatenate([hist_u[...], u], axis=1)
    hist_u[...] = u[:, ts - CONV_HIST:, :]
    cw = conv_w_ref[...]
    uc = jnp.broadcast_to(conv_b_ref[...].reshape(1, 1, c), (nb, ts, c))
    for k in range(CONV_WIDTH):
        off = CONV_HIST - (CONV_WIDTH - 1) + k
        uc = uc + cw[k:k + 1, :].reshape(1, 1, c) * ext_u[:, off:off + ts, :]
    ucf = uc.reshape(rows, c)
    ucb = ucf.astype(BF16)

    r = jax.nn.sigmoid(_block_diag_dot(ucb, wa_ref) + ba_ref[...])
    i_gate = jax.nn.sigmoid(_block_diag_dot(ucb, wx_ref) + bx_ref[...])
    lam = lam_ref[...]
    softplus_neg_lam = jnp.maximum(-lam, 0.0) + jnp.log1p(jnp.exp(-jnp.abs(lam)))
    log_a = (-LRU_C) * r * softplus_neg_lam
    a = jnp.exp(log_a)
    mult = jnp.sqrt(jnp.tanh(-log_a) * (1.0 + a * a))
    t_local = lax.broadcasted_iota(jnp.int32, (nb, ts, c), 1)
    first = (t_local + j * ts) == 0
    a3 = a.reshape(nb, ts, c)
    x3 = jnp.where(first, 1.0, mult.reshape(nb, ts, c)) * i_gate.reshape(nb, ts, c) * uc

    shift = 1
    while shift < ts:
        valid = t_local >= shift
        a_prev = pltpu.roll(a3, shift, axis=1)
        x_prev = pltpu.roll(x3, shift, axis=1)
        x3 = jnp.where(valid, a3 * x_prev, 0.0) + x3
        a3 = jnp.where(valid, a3 * a_prev, a3)
        shift *= 2
    h = x3 + a3 * carry[...][:, None, :]
    carry[...] = h[:, ts - 1, :]

    gelu_gate = 0.5 * gate * (1.0 + jnp.tanh(math.sqrt(2.0 / math.pi) * (gate + 0.044715 * gate * gate * gate)))
    y_lru = h.reshape(rows, c) * gelu_gate

    ext_v = jnp.concatenate([hist_v[...], v], axis=1)
    hist_v[...] = v[:, ts - POOL_HIST:, :]
    t_glob = lax.broadcasted_iota(jnp.int32, (1, ts, POOL_GROUP_DIM), 1) + j * ts
    pooled = []
    for g, win in enumerate(POOL_WINDOWS):
        lanes = slice(g * POOL_GROUP_DIM, (g + 1) * POOL_GROUP_DIM)
        s = ext_v[:, :, lanes]
        shift = 1
        while shift < win:
            s = s + pltpu.roll(s, shift, axis=1)
            shift *= 2
        count = jnp.minimum(t_glob + 1, win).astype(F32)
        pooled.append(s[:, POOL_HIST:, :] / count - v[:, :, lanes])
    pooled = jnp.concatenate(pooled, axis=-1).reshape(rows, c).astype(BF16)
    y_pool = _block_diag_dot(pooled, pool_w_ref) * pool_scale_ref[...]

    y = jnp.concatenate([y_lru, y_pool], axis=1).astype(BF16)
    mix = _dot(y, w_out_ref[...])
    x1 = _layer_norm(DEEPNORM_ALPHA * x + mix, g1_ref[...], b1_ref[...])
    x1_ref[...] = x1.reshape(nb, ts, d)

    hi = x1.astype(BF16)
    hi_f = hi.astype(F32)
    lo = (x1 - hi_f).astype(BF16)
    logits = _dot(hi, rw_hi_ref[...]) + _dot(lo, rw_hi_ref[...]) + _dot(hi, rw_lo_ref[...])
    logit_ref[...] = logits.reshape(nb, ts, logits.shape[-1])

    bits = lax.bitcast_convert_type(hi_f, jnp.uint32)
    packed = bits[:, :d // 2] | (bits[:, d // 2:] >> 16)
    x1p_ref[...] = packed.reshape(nb, ts, d // 2)


def _const_spec(shape):
    return pl.BlockSpec(shape, lambda j: (0,) * len(shape), pipeline_mode=pl.Buffered(1))


def _regroup_block_diag(w, width):
    heads, hd, _ = w.shape
    per = width // hd
    w = w.reshape(heads // per, per, hd, hd)
    eye = jnp.eye(per, dtype=w.dtype)
    return jnp.einsum("gpij,pq->gpiqj", w, eye).reshape(heads // per, width, width)


def _mixer(x, w_in, conv_w, conv_b, wa, ba, wx, bx, lam, pool_w, pool_scale, w_out, g1, b1, router_w, ts=64):
    nb, s, d = x.shape
    c = conv_w.shape[-1]
    n_exp = router_w.shape[-1]
    rw_hi = router_w.astype(BF16)
    rw_lo = (router_w - rw_hi.astype(F32)).astype(BF16)
    row = lambda p: p.reshape(1, -1)
    operands = (
        x, w_in.astype(BF16), conv_w, row(conv_b),
        _regroup_block_diag(wa, MXU_DIM).astype(BF16), row(ba),
        _regroup_block_diag(wx, MXU_DIM).astype(BF16), row(bx), row(lam),
        _regroup_block_diag(pool_w, MXU_DIM).astype(BF16), row(pool_scale),
        w_out.astype(BF16), row(g1), row(b1), rw_hi, rw_lo)
    tile = lambda width: pl.BlockSpec((nb, ts, width), lambda j: (0, j, 0))
    in_specs = [tile(d)] + [_const_spec(op.shape) for op in operands[1:]]
    return pl.pallas_call(
        _mixer_kernel,
        grid=(s // ts,),
        in_specs=in_specs,
        out_specs=[tile(d), tile(d // 2), tile(n_exp)],
        out_shape=[jax.ShapeDtypeStruct((nb, s, d), F32),
                   jax.ShapeDtypeStruct((nb, s, d // 2), jnp.uint32),
                   jax.ShapeDtypeStruct((nb, s, n_exp), F32)],
        scratch_shapes=[pltpu.VMEM((nb, CONV_HIST, c), F32),
                        pltpu.VMEM((nb, POOL_HIST, c), F32),
                        pltpu.VMEM((nb, c), F32)],
        compiler_params=pltpu.CompilerParams(dimension_semantics=("arbitrary",),
                                             vmem_limit_bytes=VMEM_LIMIT),
        name="mixer_ln_router",
    )(*operands)


def _first_row_of(cond, rows_rev, n):
    return n - jnp.max(jnp.where(cond, rows_rev, 0.0), axis=0, keepdims=True)


def _route_kernel(logit_ref, bias_ref, idx_ref, gate_ref, rank_ref, count_ref, count_scr):
    i = pl.program_id(0)

    @pl.when(i == 0)
    def _():
        count_scr[...] = jnp.zeros_like(count_scr)

    scores = jax.nn.sigmoid(logit_ref[...].T)
    n_exp, tn = scores.shape
    gsz = n_exp // N_EXPERT_GROUPS
    neg = -jnp.inf
    biased = scores + bias_ref[...]
    row = lax.broadcasted_iota(jnp.int32, (n_exp, tn), 0).astype(F32)
    row_rev = n_exp - row

    g_row = lax.broadcasted_iota(jnp.int32, (N_EXPERT_GROUPS, tn), 0)
    group_score = jnp.zeros((N_EXPERT_GROUPS, tn), F32)
    grp_row = lax.broadcasted_iota(jnp.int32, (gsz, tn), 0).astype(F32)
    grp_rev = gsz - grp_row
    for g in range(N_EXPERT_GROUPS):
        blk = biased[g * gsz:(g + 1) * gsz]
        m1 = jnp.max(blk, axis=0, keepdims=True)
        i1 = _first_row_of(blk == m1, grp_rev, gsz)
        m2 = jnp.max(jnp.where(grp_row == i1, neg, blk), axis=0, keepdims=True)
        group_score = jnp.where(g_row == g, m1 + m2, group_score)

    beaten_by = jnp.zeros((N_EXPERT_GROUPS, tn), F32)
    for g in range(N_EXPERT_GROUPS):
        other = jnp.max(jnp.where(g_row == g, group_score, neg), axis=0, keepdims=True)
        wins = (other > group_score) | ((other == group_score) & (g < g_row))
        beaten_by = beaten_by + jnp.where(wins, 1.0, 0.0)
    masked = []
    for g in range(N_EXPERT_GROUPS):
        beaten_g = jnp.max(jnp.where(g_row == g, beaten_by, 0.0), axis=0, keepdims=True)
        masked.append(jnp.where(beaten_g < TOPK_GROUPS, biased[g * gsz:(g + 1) * gsz], neg))
    masked = jnp.concatenate(masked, axis=0)

    k_row = lax.broadcasted_iota(jnp.int32, (TOP_K, tn), 0)
    selected = jnp.zeros((n_exp, tn), F32)
    idx = jnp.zeros((TOP_K, tn), F32)
    gate = jnp.zeros((TOP_K, tn), F32)
    hits = []
    for k in range(TOP_K):
        m = jnp.max(masked, axis=0, keepdims=True)
        ik = _first_row_of(masked == m, row_rev, n_exp)
        hit = row == ik
        gk = jnp.sum(jnp.where(hit, scores, 0.0), axis=0, keepdims=True)
        masked = jnp.where(hit, neg, masked)
        selected = jnp.where(hit, 1.0, selected)
        idx = jnp.where(k_row == k, ik, idx)
        gate = jnp.where(k_row == k, gk, gate)
        hits.append(hit)
    gate = gate / jnp.sum(gate, axis=0, keepdims=True) * ROUTED_SCALE

    earlier = (lax.broadcasted_iota(jnp.int32, (tn, tn), 0) < lax.broadcasted_iota(jnp.int32, (tn, tn), 1))
    before = _dot(selected.astype(BF16), jnp.where(earlier, 1.0, 0.0).astype(BF16)) + count_scr[...]
    rank = jnp.zeros((TOP_K, tn), F32)
    for k in range(TOP_K):
        rk = jnp.sum(jnp.where(hits[k], before, 0.0), axis=0, keepdims=True)
        rank = jnp.where(k_row == k, rk, rank)
    count_scr[...] = count_scr[...] + jnp.sum(selected, axis=1, keepdims=True)

    idx_ref[...] = idx.astype(jnp.int32)
    pad = jnp.zeros((gate_ref.shape[1] - TOP_K, tn), F32)
    gate_ref[...] = jnp.concatenate([gate, pad], axis=0).T
    rank_ref[...] = rank.astype(jnp.int32)
    count_ref[...] = count_scr[...].astype(jnp.int32)


def _route(logits, bias, tok0, t, tn=512):
    n_exp = logits.shape[1]
    first = tok0 // tn
    return pl.pallas_call(
        _route_kernel,
        grid=(t // tn,),
        in_specs=[pl.BlockSpec((tn, n_exp), lambda i: (first + i, 0)),
                  pl.BlockSpec((n_exp, 1), lambda i: (0, 0))],
        out_specs=[pl.BlockSpec((TOP_K, tn), lambda i: (0, i)),
                   pl.BlockSpec((tn, GATE_LANES), lambda i: (i, 0)),
                   pl.BlockSpec((TOP_K, tn), lambda i: (0, i)),
                   pl.BlockSpec((n_exp, 1), lambda i: (0, 0))],
        out_shape=[jax.ShapeDtypeStruct((TOP_K, t), jnp.int32),
                   jax.ShapeDtypeStruct((t, GATE_LANES), F32),
                   jax.ShapeDtypeStruct((TOP_K, t), jnp.int32),
                   jax.ShapeDtypeStruct((n_exp, 1), jnp.int32)],
        scratch_shapes=[pltpu.VMEM((n_exp, 1), F32)],
        compiler_params=pltpu.CompilerParams(dimension_semantics=("arbitrary",),
                                             vmem_limit_bytes=VMEM_LIMIT),
        name="route_topk_rank",
    )(logits, bias.reshape(n_exp, 1))


def _slot_kernel(idx_ref, rank_ref, start_ref, slot_ref):
    idx = idx_ref[...]
    n_exp = start_ref.shape[0]
    tn = idx.shape[1]
    row = lax.broadcasted_iota(jnp.int32, (n_exp, tn), 0)
    start = start_ref[...]
    first = [jnp.sum(jnp.where(row == idx[k:k + 1], start, 0), axis=0, keepdims=True) for k in range(TOP_K)]
    slots = jnp.concatenate(first, axis=0) + rank_ref[...]
    chunk = slot_ref.shape[-1]
    for c in range(slot_ref.shape[0]):
        slot_ref[c] = slots[:, c * chunk:(c + 1) * chunk]


def _slots(idx, rank, expert_start, tn=512):
    k, t = idx.shape
    n_exp = expert_start.shape[0]
    per_step = tn // SC_CHUNK
    return pl.pallas_call(
        _slot_kernel,
        grid=(t // tn,),
        in_specs=[pl.BlockSpec((k, tn), lambda i: (0, i)),
                  pl.BlockSpec((k, tn), lambda i: (0, i)),
                  pl.BlockSpec((n_exp, 1), lambda i: (0, 0))],
        out_specs=pl.BlockSpec((per_step, k, SC_CHUNK), lambda i: (i, 0, 0)),
        out_shape=jax.ShapeDtypeStruct((t // SC_CHUNK, k, SC_CHUNK), jnp.int32),
        compiler_params=pltpu.CompilerParams(dimension_semantics=("arbitrary",)),
        name="dispatch_slots",
    )(idx, rank, expert_start.reshape(n_exp, 1))


def _unpack_bf16_pairs(p):
    hi = lax.bitcast_convert_type(p & jnp.uint32(0xFFFF0000), F32)
    lo = lax.bitcast_convert_type(p << 16, F32)
    return jnp.concatenate([hi, lo], axis=1)


def _pack_bf16_pairs(y):
    n = y.shape[1] // 2
    bits = lax.bitcast_convert_type(y.astype(BF16).astype(F32), jnp.uint32)
    return bits[:, :n] | (bits[:, n:] >> 16)


def _sc_worker_layout(n_chunks_total):
    info = plsc.get_sparse_core_info()
    n_workers = info.num_cores * info.num_subcores
    return info.num_cores, n_chunks_total // n_workers


def _sc_dispatch(slots, x1p, tok0, n_rows):
    n_chunks_total, n_k, chunk = slots.shape
    w = x1p.shape[1]
    n_cores, n_chunks = _sc_worker_layout(n_chunks_total)
    mesh = plsc.VectorSubcoreMesh(core_axis_name="c", subcore_axis_name="s")

    @functools.partial(
        pl.kernel, mesh=mesh, name="sc_dispatch_rows",
        out_type=jax.ShapeDtypeStruct((n_rows, w), x1p.dtype),
        scratch_types=[pltpu.VMEM((n_k, chunk), jnp.int32), pltpu.VMEM((chunk, w), x1p.dtype),
                       pltpu.SemaphoreType.DMA])
    def dispatch(slots_hbm, x_hbm, xg_hbm, idx_v, rows_v, sem):
        wid = lax.axis_index("s") * n_cores + lax.axis_index("c")

        @pl.loop(0, n_chunks)
        def _(ci):
            chunk_id = wid * n_chunks + ci
            pltpu.sync_copy(slots_hbm.at[chunk_id], idx_v)
            pltpu.sync_copy(x_hbm.at[pl.ds(tok0 + chunk_id * chunk, chunk)], rows_v)
            copies = [pltpu.make_async_copy(rows_v, xg_hbm.at[idx_v.at[k]], sem) for k in range(n_k)]
            for cp in copies:
                cp.start()
            for cp in copies:
                cp.wait()

    return dispatch(slots, x1p)


def _sc_collect(slots, yg):
    n_chunks_total, n_k, chunk = slots.shape
    w = yg.shape[1]
    t = n_chunks_total * chunk
    n_cores, n_chunks = _sc_worker_layout(n_chunks_total)
    mesh = plsc.VectorSubcoreMesh(core_axis_name="c", subcore_axis_name="s")

    @functools.partial(
        pl.kernel, mesh=mesh, name="sc_collect_rows",
        out_type=jax.ShapeDtypeStruct((n_k, t, w), yg.dtype),
        scratch_types=[pltpu.VMEM((n_k, chunk), jnp.int32), pltpu.VMEM((chunk, w), yg.dtype),
                       pltpu.SemaphoreType.DMA])
    def collect(slots_hbm, yg_hbm, yt_hbm, idx_v, rows_v, sem):
        wid = lax.axis_index("s") * n_cores + lax.axis_index("c")

        @pl.loop(0, n_chunks)
        def _(ci):
            chunk_id = wid * n_chunks + ci
            pltpu.sync_copy(slots_hbm.at[chunk_id], idx_v)
            for k in range(n_k):
                pltpu.async_copy(yg_hbm.at[idx_v.at[k]], rows_v, sem).wait()
                pltpu.sync_copy(rows_v, yt_hbm.at[k, pl.ds(chunk_id * chunk, chunk)])

    return collect(slots, yg)


def _expert_kernel(first_block, n_block, n_used, xg_hbm, w1_ref, w3_ref, w2_ref, yg_hbm,
                   x_buf, y_buf, w1_bf, w3_bf, w2_bf, in_sem, out_sem):
    e = pl.program_id(0)
    n_slot, bm, _ = x_buf.shape
    ahead = n_slot - 1
    total = n_used[0]

    def fetch(g):
        slot = g % n_slot
        return pltpu.make_async_copy(xg_hbm.at[pl.ds(g * bm, bm)], x_buf.at[slot], in_sem.at[slot])

    def write_back(g):
        slot = g % n_slot
        return pltpu.make_async_copy(y_buf.at[slot], yg_hbm.at[pl.ds(g * bm, bm)], out_sem.at[slot])

    @pl.when(e == 0)
    def _():
        for g in range(ahead):
            @pl.when(g < total)
            def _():
                fetch(g).start()

    @pl.when(n_block[e] > 0)
    def _():
        w1_bf[...] = w1_ref[0].astype(BF16)
        w3_bf[...] = w3_ref[0].astype(BF16)
        w2_bf[...] = w2_ref[0].astype(BF16)

    @pl.loop(0, n_block[e])
    def _(b):
        g = first_block[e] + b
        slot = g % n_slot
        fetch(g).wait()

        @pl.when(g + ahead < total)
        def _():
            fetch(g + ahead).start()

        xb = _unpack_bf16_pairs(x_buf[slot]).astype(BF16)
        h1 = _dot(xb, w1_bf[...])
        h3 = _dot(xb, w3_bf[...])
        h = (h1 * jax.nn.sigmoid(h1) * h3).astype(BF16)
        y = _pack_bf16_pairs(_dot(h, w2_bf[...]))

        @pl.when(g >= n_slot)
        def _():
            write_back(g - n_slot).wait()

        y_buf[slot] = y
        write_back(g).start()

    @pl.when(e == pl.num_programs(0) - 1)
    def _():
        for back in range(1, n_slot + 1):
            @pl.when(total >= back)
            def _():
                write_back(total - back).wait()


def _experts(first_block, n_block, n_used, xg, w1, w3, w2, bm):
    n_rows, w = xg.shape
    n_exp, d, de = w1.shape
    weight = lambda e, *_: (e, 0, 0)
    return pl.pallas_call(
        _expert_kernel,
        grid_spec=pltpu.PrefetchScalarGridSpec(
            num_scalar_prefetch=3,
            grid=(n_exp,),
            in_specs=[pl.BlockSpec(memory_space=pl.ANY),
                      pl.BlockSpec((1, d, de), weight),
                      pl.BlockSpec((1, d, de), weight),
                      pl.BlockSpec((1, de, d), weight)],
            out_specs=pl.BlockSpec(memory_space=pl.ANY),
            scratch_shapes=[pltpu.VMEM((ROW_RING, bm, w), xg.dtype), pltpu.VMEM((ROW_RING, bm, w), jnp.uint32),
                            pltpu.VMEM((d, de), BF16), pltpu.VMEM((d, de), BF16), pltpu.VMEM((de, d), BF16),
                            pltpu.SemaphoreType.DMA((ROW_RING,)), pltpu.SemaphoreType.DMA((ROW_RING,))]),
        out_shape=jax.ShapeDtypeStruct((n_rows, w), jnp.uint32),
        compiler_params=pltpu.CompilerParams(dimension_semantics=("arbitrary",),
                                             vmem_limit_bytes=VMEM_LIMIT),
        name="routed_experts",
    )(first_block, n_block, n_used, xg, w1, w3, w2)


def _combine_kernel(yt_ref, x1_ref, gate_ref, sw1_ref, sw3_ref, sw2_ref, g2_ref, b2_ref, *rest):
    out_ref = rest[-1]
    x1 = x1_ref[...]
    xb = x1.astype(BF16)
    h1 = _dot(xb, sw1_ref[...])
    h3 = _dot(xb, sw3_ref[...])
    ffn = _dot((h1 * jax.nn.sigmoid(h1) * h3).astype(BF16), sw2_ref[...])
    gate = gate_ref[...]
    for k in range(yt_ref.shape[0]):
        ffn = ffn + gate[:, k:k + 1] * _unpack_bf16_pairs(yt_ref[k])
    out_ref[...] = _layer_norm(DEEPNORM_ALPHA * x1 + ffn, g2_ref[...], b2_ref[...])


def _combine(yt, x1, gate, sw1, sw3, sw2, g2, b2, tok0, out_so_far=None, tn=256):
    n_k, t, w = yt.shape
    t_all, d = x1.shape
    first = tok0 // tn
    row = lambda p: p.reshape(1, -1)
    operands = [yt, x1, gate, sw1.astype(BF16), sw3.astype(BF16), sw2.astype(BF16), row(g2), row(b2)]
    in_specs = [pl.BlockSpec((n_k, tn, w), lambda i: (0, i, 0)),
                pl.BlockSpec((tn, d), lambda i: (first + i, 0)),
                pl.BlockSpec((tn, gate.shape[1]), lambda i: (i, 0))]
    in_specs += [_const_spec(op.shape) for op in operands[3:]]
    aliases = {}
    if out_so_far is not None:
        aliases = {len(operands): 0}
        operands.append(out_so_far)
        in_specs.append(pl.BlockSpec(memory_space=pl.ANY))
    return pl.pallas_call(
        _combine_kernel,
        grid=(t // tn,),
        in_specs=in_specs,
        out_specs=pl.BlockSpec((tn, d), lambda i: (first + i, 0)),
        out_shape=jax.ShapeDtypeStruct((t_all, d), F32),
        input_output_aliases=aliases,
        compiler_params=pltpu.CompilerParams(dimension_semantics=("arbitrary",),
                                             vmem_limit_bytes=VMEM_LIMIT),
        name="combine_shared_ln",
    )(*operands)


def _block_table(counts, bm):
    n_block = (counts + bm - 1) // bm
    ends = jnp.cumsum(n_block)
    first_block = ends - n_block
    i32 = lambda v: v.astype(jnp.int32)
    return i32(first_block * bm), i32(first_block), i32(n_block), i32(ends[-1:])


def _moe(x1, x1p, logits, router_bias, w1, w3, w2, sw1, sw3, sw2, g2, b2, bm=256):
    t, d = x1.shape
    n_exp = logits.shape[1]
    t_part = t // TOKEN_PARTS
    n_blocks = (t_part * TOP_K) // bm + n_exp
    parts = []
    for p in range(TOKEN_PARTS):
        idx, gate, rank, counts = _route(logits, router_bias, p * t_part, t_part)
        starts, first_block, n_block, n_used = _block_table(counts[:, 0], bm)
        slots = _slots(idx, rank, starts)
        xg = _sc_dispatch(slots, x1p, p * t_part, n_blocks * bm)
        parts.append((gate, slots, xg, first_block, n_block, n_used))
    collected = []
    for gate, slots, xg, first_block, n_block, n_used in parts:
        yg = _experts(first_block, n_block, n_used, xg, w1, w3, w2, bm)
        collected.append((gate, _sc_collect(slots, yg)))
    out = None
    for p, (gate, yt) in enumerate(collected):
        out = _combine(yt, x1, gate, sw1, sw3, sw2, g2, b2, p * t_part, out)
    return out


def kernel(x, w_in, conv_w, conv_b, lru_wa, lru_ba, lru_wx, lru_bx, lru_lambda, pool_w, pool_scale, w_out, ln1_g, ln1_b, router_w, router_bias, exp_w1, exp_w3, exp_w2, sh_w1, sh_w3, sh_w2, ln2_g, ln2_b):
    nb, s, d = x.shape
    for l in range(DEPTH):
        x1, x1p, logits = _mixer(x, w_in[l], conv_w[l], conv_b[l], lru_wa[l], lru_ba[l], lru_wx[l], lru_bx[l],
                                 lru_lambda[l], pool_w[l], pool_scale[l], w_out[l], ln1_g[l], ln1_b[l],
                                 router_w[l])
        t = nb * s
        x = _moe(x1.reshape(t, d), x1p.reshape(t, d // 2), logits.reshape(t, -1), router_bias[l],
                 exp_w1[l], exp_w3[l], exp_w2[l], sh_w1[l], sh_w3[l], sh_w2[l], ln2_g[l], ln2_b[l])
        x = x.reshape(nb, s, d)
    return x
```

```python
import functools
import math

import jax
import jax.numpy as jnp
from jax import lax
from jax.experimental import pallas as pl
from jax.experimental.pallas import tpu as pltpu
from jax.experimental.pallas import tpu_sc as plsc

LRU_HEADS = 8
CONV_WIDTH = 4
LRU_C = 8.0
POOL_WINDOWS = (2, 4, 8, 16)
N_EXPERT_GROUPS = 8
TOPK_GROUPS = 4
TOP_K = 8
ROUTED_SCALE = 2.5
LN_EPS = 1e-5
DEPTH = 1
DEEPNORM_ALPHA = (2.0 * DEPTH) ** 0.25

MXU_DIM = 256
POOL_GROUP_DIM = 128
CONV_HIST = 8
POOL_HIST = 16
GATE_LANES = 128
SC_CHUNK = 128
MIXER_PARTS = 2
BLOCK_GROUP = 4
ROW_RING = 8
VMEM_LIMIT = 56 * 1024 * 1024

F32 = jnp.float32
BF16 = jnp.bfloat16


def _dot(a, b):
    return jnp.dot(a, b, preferred_element_type=F32)


def _layer_norm(z, g, b):
    mu = jnp.mean(z, axis=-1, keepdims=True)
    zc = z - mu
    var = jnp.mean(zc * zc, axis=-1, keepdims=True)
    return zc * lax.rsqrt(var + LN_EPS) * g + b


def _block_diag_dot(xb, w_ref):
    n = w_ref.shape[0]
    return jnp.concatenate(
        [_dot(xb[:, i * MXU_DIM:(i + 1) * MXU_DIM], w_ref[i]) for i in range(n)], axis=1)


def _causal_conv(u, hist_ref, part, conv_w, conv_b):
    pb, ts, c = u.shape
    ext = jnp.concatenate([hist_ref[part], u], axis=1)
    hist_ref[part] = u[:, ts - CONV_HIST:, :]
    uc = jnp.broadcast_to(conv_b.reshape(1, 1, c), (pb, ts, c))
    for k in range(CONV_WIDTH):
        off = CONV_HIST - (CONV_WIDTH - 1) + k
        uc = uc + conv_w[k:k + 1, :].reshape(1, 1, c) * ext[:, off:off + ts, :]
    return uc


def _lru_scan(uc, ga, gx, lam, carry_ref, part, t0):
    pb, ts, c = uc.shape
    r = jax.nn.sigmoid(ga)
    i_gate = jax.nn.sigmoid(gx)
    softplus_neg_lam = jnp.maximum(-lam, 0.0) + jnp.log1p(jnp.exp(-jnp.abs(lam)))
    log_a = (-LRU_C) * r * softplus_neg_lam
    a = jnp.exp(log_a)
    mult = jnp.sqrt(jnp.tanh(-log_a) * (1.0 + a * a))
    t_local = lax.broadcasted_iota(jnp.int32, (pb, ts, c), 1)
    first = (t_local + t0) == 0
    a3 = a.reshape(pb, ts, c)
    x3 = jnp.where(first, 1.0, mult.reshape(pb, ts, c)) * i_gate.reshape(pb, ts, c) * uc

    shift = 1
    while shift < ts:
        valid = t_local >= shift
        a_prev = pltpu.roll(a3, shift, axis=1)
        x_prev = pltpu.roll(x3, shift, axis=1)
        x3 = jnp.where(valid, a3 * x_prev, 0.0) + x3
        a3 = jnp.where(valid, a3 * a_prev, a3)
        shift *= 2
    h = x3 + a3 * carry_ref[part][:, None, :]
    carry_ref[part] = h[:, ts - 1, :]
    return h


def _multiscale_pool(v, hist_ref, part, t0):
    pb, ts, c = v.shape
    ext = jnp.concatenate([hist_ref[part], v], axis=1)
    hist_ref[part] = v[:, ts - POOL_HIST:, :]
    t_glob = lax.broadcasted_iota(jnp.int32, (1, ts, POOL_GROUP_DIM), 1) + t0
    pooled = []
    for g, win in enumerate(POOL_WINDOWS):
        lanes = slice(g * POOL_GROUP_DIM, (g + 1) * POOL_GROUP_DIM)
        s = ext[:, :, lanes]
        shift = 1
        while shift < win:
            s = s + pltpu.roll(s, shift, axis=1)
            shift *= 2
        count = jnp.minimum(t_glob + 1, win).astype(F32)
        pooled.append(s[:, POOL_HIST:, :] / count - v[:, :, lanes])
    return jnp.concatenate(pooled, axis=-1)


def _mixer_kernel(x_ref, w_in_ref, conv_w_ref, conv_b_ref, wa_ref, ba_ref, wx_ref, bx_ref, lam_ref,
                  pool_w_ref, pool_scale_ref, w_out_ref, g1_ref, b1_ref, rw_hi_ref, rw_lo_ref,
                  x1_ref, x1p_ref, logit_ref, hist_u, hist_v, carry):
    j = pl.program_id(0)
    nb, ts, d = x_ref.shape
    c = hist_u.shape[-1]
    pb = nb // MIXER_PARTS
    rows = pb * ts
    t0 = j * ts
    parts = [slice(p * pb, (p + 1) * pb) for p in range(MIXER_PARTS)]

    @pl.when(j == 0)
    def _():
        hist_u[...] = jnp.zeros_like(hist_u)
        hist_v[...] = jnp.zeros_like(hist_v)
        carry[...] = jnp.zeros_like(carry)

    xs = [x_ref[p].reshape(rows, d) for p in parts]
    projs = [_dot(x.astype(BF16), w_in_ref[...]) for x in xs]
    ucs = [_causal_conv(proj[:, :c].reshape(pb, ts, c), hist_u, p, conv_w_ref[...], conv_b_ref[...])
           for proj, p in zip(projs, parts)]
    ucbs = [uc.reshape(rows, c).astype(BF16) for uc in ucs]
    gas = [_block_diag_dot(ucb, wa_ref) + ba_ref[...] for ucb in ucbs]
    gxs = [_block_diag_dot(ucb, wx_ref) + bx_ref[...] for ucb in ucbs]
    hs = [_lru_scan(uc, ga, gx, lam_ref[...], carry, p, t0) for uc, ga, gx, p in zip(ucs, gas, gxs, parts)]
    y_lrus = []
    for h, proj in zip(hs, projs):
        gate = proj[:, c:2 * c]
        gelu_gate = 0.5 * gate * (1.0 + jnp.tanh(math.sqrt(2.0 / math.pi) * (gate + 0.044715 * gate * gate * gate)))
        y_lrus.append(h.reshape(rows, c) * gelu_gate)
    pooled = [_multiscale_pool(proj[:, 2 * c:].reshape(pb, ts, c), hist_v, p, t0).reshape(rows, c).astype(BF16)
              for proj, p in zip(projs, parts)]
    y_pools = [_block_diag_dot(pl_, pool_w_ref) * pool_scale_ref[...] for pl_ in pooled]
    mixes = [_dot(jnp.concatenate([y_lru, y_pool], axis=1).astype(BF16), w_out_ref[...])
             for y_lru, y_pool in zip(y_lrus, y_pools)]
    x1s = [_layer_norm(DEEPNORM_ALPHA * x + mix, g1_ref[...], b1_ref[...]) for x, mix in zip(xs, mixes)]

    for p, x1 in zip(parts, x1s):
        x1_ref[p] = x1.reshape(pb, ts, d)
        hi = x1.astype(BF16)
        hi_f = hi.astype(F32)
        lo = (x1 - hi_f).astype(BF16)
        logits = _dot(hi, rw_hi_ref[...]) + _dot(lo, rw_hi_ref[...]) + _dot(hi, rw_lo_ref[...])
        logit_ref[p] = logits.reshape(pb, ts, logits.shape[-1])
        bits = lax.bitcast_convert_type(hi_f, jnp.uint32)
        packed = bits[:, :d // 2] | (bits[:, d // 2:] >> 16)
        x1p_ref[p] = packed.reshape(pb, ts, d // 2)


def _const_spec(shape):
    return pl.BlockSpec(shape, lambda j: (0,) * len(shape), pipeline_mode=pl.Buffered(1))


def _regroup_block_diag(w, width):
    heads, hd, _ = w.shape
    per = width // hd
    w = w.reshape(heads // per, per, hd, hd)
    eye = jnp.eye(per, dtype=w.dtype)
    return jnp.einsum("gpij,pq->gpiqj", w, eye).reshape(heads // per, width, width)


def _mixer(x, w_in, conv_w, conv_b, wa, ba, wx, bx, lam, pool_w, pool_scale, w_out, g1, b1, router_w, ts=64):
    nb, s, d = x.shape
    c = conv_w.shape[-1]
    n_exp = router_w.shape[-1]
    rw_hi = router_w.astype(BF16)
    rw_lo = (router_w - rw_hi.astype(F32)).astype(BF16)
    row = lambda p: p.reshape(1, -1)
    operands = (
        x, w_in.astype(BF16), conv_w, row(conv_b),
        _regroup_block_diag(wa, MXU_DIM).astype(BF16), row(ba),
        _regroup_block_diag(wx, MXU_DIM).astype(BF16), row(bx), row(lam),
        _regroup_block_diag(pool_w, MXU_DIM).astype(BF16), row(pool_scale),
        w_out.astype(BF16), row(g1), row(b1), rw_hi, rw_lo)
    tile = lambda width: pl.BlockSpec((nb, ts, width), lambda j: (0, j, 0))
    in_specs = [tile(d)] + [_const_spec(op.shape) for op in operands[1:]]
    return pl.pallas_call(
        _mixer_kernel,
        grid=(s // ts,),
        in_specs=in_specs,
        out_specs=[tile(d), tile(d // 2), tile(n_exp)],
        out_shape=[jax.ShapeDtypeStruct((nb, s, d), F32),
                   jax.ShapeDtypeStruct((nb, s, d // 2), jnp.uint32),
                   jax.ShapeDtypeStruct((nb, s, n_exp), F32)],
        scratch_shapes=[pltpu.VMEM((nb, CONV_HIST, c), F32),
                        pltpu.VMEM((nb, POOL_HIST, c), F32),
                        pltpu.VMEM((nb, c), F32)],
        compiler_params=pltpu.CompilerParams(dimension_semantics=("arbitrary",),
                                             vmem_limit_bytes=VMEM_LIMIT),
        name="mixer_ln_router",
    )(*operands)


def _first_row_of(cond, rows_rev, n):
    return n - jnp.max(jnp.where(cond, rows_rev, 0.0), axis=0, keepdims=True)


def _route_kernel(logit_ref, bias_ref, idx_ref, gate_ref, rank_ref, count_ref, count_scr):
    i = pl.program_id(0)

    @pl.when(i == 0)
    def _():
        count_scr[...] = jnp.zeros_like(count_scr)

    scores = jax.nn.sigmoid(logit_ref[...].T)
    n_exp, tn = scores.shape
    gsz = n_exp // N_EXPERT_GROUPS
    neg = -jnp.inf
    biased = scores + bias_ref[...]
    row = lax.broadcasted_iota(jnp.int32, (n_exp, tn), 0).astype(F32)
    row_rev = n_exp - row

    g_row = lax.broadcasted_iota(jnp.int32, (N_EXPERT_GROUPS, tn), 0)
    group_score = jnp.zeros((N_EXPERT_GROUPS, tn), F32)
    grp_row = lax.broadcasted_iota(jnp.int32, (gsz, tn), 0).astype(F32)
    grp_rev = gsz - grp_row
    for g in range(N_EXPERT_GROUPS):
        blk = biased[g * gsz:(g + 1) * gsz]
        m1 = jnp.max(blk, axis=0, keepdims=True)
        i1 = _first_row_of(blk == m1, grp_rev, gsz)
        m2 = jnp.max(jnp.where(grp_row == i1, neg, blk), axis=0, keepdims=True)
        group_score = jnp.where(g_row == g, m1 + m2, group_score)

    beaten_by = jnp.zeros((N_EXPERT_GROUPS, tn), F32)
    for g in range(N_EXPERT_GROUPS):
        other = jnp.max(jnp.where(g_row == g, group_score, neg), axis=0, keepdims=True)
        wins = (other > group_score) | ((other == group_score) & (g < g_row))
        beaten_by = beaten_by + jnp.where(wins, 1.0, 0.0)
    masked = []
    for g in range(N_EXPERT_GROUPS):
        beaten_g = jnp.max(jnp.where(g_row == g, beaten_by, 0.0), axis=0, keepdims=True)
        masked.append(jnp.where(beaten_g < TOPK_GROUPS, biased[g * gsz:(g + 1) * gsz], neg))
    masked = jnp.concatenate(masked, axis=0)

    k_row = lax.broadcasted_iota(jnp.int32, (TOP_K, tn), 0)
    selected = jnp.zeros((n_exp, tn), F32)
    idx = jnp.zeros((TOP_K, tn), F32)
    gate = jnp.zeros((TOP_K, tn), F32)
    hits = []
    for k in range(TOP_K):
        m = jnp.max(masked, axis=0, keepdims=True)
        ik = _first_row_of(masked == m, row_rev, n_exp)
        hit = row == ik
        gk = jnp.sum(jnp.where(hit, scores, 0.0), axis=0, keepdims=True)
        masked = jnp.where(hit, neg, masked)
        selected = jnp.where(hit, 1.0, selected)
        idx = jnp.where(k_row == k, ik, idx)
        gate = jnp.where(k_row == k, gk, gate)
        hits.append(hit)
    gate = gate / jnp.sum(gate, axis=0, keepdims=True) * ROUTED_SCALE

    earlier = (lax.broadcasted_iota(jnp.int32, (tn, tn), 0) < lax.broadcasted_iota(jnp.int32, (tn, tn), 1))
    before = _dot(selected.astype(BF16), jnp.where(earlier, 1.0, 0.0).astype(BF16)) + count_scr[...]
    rank = jnp.zeros((TOP_K, tn), F32)
    for k in range(TOP_K):
        rk = jnp.sum(jnp.where(hits[k], before, 0.0), axis=0, keepdims=True)
        rank = jnp.where(k_row == k, rk, rank)
    count_scr[...] = count_scr[...] + jnp.sum(selected, axis=1, keepdims=True)

    idx_ref[...] = idx.astype(jnp.int32)
    pad = jnp.zeros((gate_ref.shape[1] - TOP_K, tn), F32)
    gate_ref[...] = jnp.concatenate([gate, pad], axis=0).T
    rank_ref[...] = rank.astype(jnp.int32)
    count_ref[...] = count_scr[...].astype(jnp.int32)


def _route(logits, bias, tn=512):
    t, n_exp = logits.shape
    return pl.pallas_call(
        _route_kernel,
        grid=(t // tn,),
        in_specs=[pl.BlockSpec((tn, n_exp), lambda i: (i, 0)),
                  pl.BlockSpec((n_exp, 1), lambda i: (0, 0))],
        out_specs=[pl.BlockSpec((TOP_K, tn), lambda i: (0, i)),
                   pl.BlockSpec((tn, GATE_LANES), lambda i: (i, 0)),
                   pl.BlockSpec((TOP_K, tn), lambda i: (0, i)),
                   pl.BlockSpec((n_exp, 1), lambda i: (0, 0))],
        out_shape=[jax.ShapeDtypeStruct((TOP_K, t), jnp.int32),
                   jax.ShapeDtypeStruct((t, GATE_LANES), F32),
                   jax.ShapeDtypeStruct((TOP_K, t), jnp.int32),
                   jax.ShapeDtypeStruct((n_exp, 1), jnp.int32)],
        scratch_shapes=[pltpu.VMEM((n_exp, 1), F32)],
        compiler_params=pltpu.CompilerParams(dimension_semantics=("arbitrary",),
                                             vmem_limit_bytes=VMEM_LIMIT),
        name="route_topk_rank",
    )(logits, bias.reshape(n_exp, 1))


def _slot_kernel(idx_ref, rank_ref, start_ref, slot_ref):
    idx = idx_ref[...]
    n_exp = start_ref.shape[0]
    tn = idx.shape[1]
    row = lax.broadcasted_iota(jnp.int32, (n_exp, tn), 0)
    start = start_ref[...]
    first = [jnp.sum(jnp.where(row == idx[k:k + 1], start, 0), axis=0, keepdims=True) for k in range(TOP_K)]
    slots = jnp.concatenate(first, axis=0) + rank_ref[...]
    chunk = slot_ref.shape[-1]
    for c in range(slot_ref.shape[0]):
        slot_ref[c] = slots[:, c * chunk:(c + 1) * chunk]


def _slots(idx, rank, expert_start, tn=512):
    k, t = idx.shape
    n_exp = expert_start.shape[0]
    per_step = tn // SC_CHUNK
    return pl.pallas_call(
        _slot_kernel,
        grid=(t // tn,),
        in_specs=[pl.BlockSpec((k, tn), lambda i: (0, i)),
                  pl.BlockSpec((k, tn), lambda i: (0, i)),
                  pl.BlockSpec((n_exp, 1), lambda i: (0, 0))],
        out_specs=pl.BlockSpec((per_step, k, SC_CHUNK), lambda i: (i, 0, 0)),
        out_shape=jax.ShapeDtypeStruct((t // SC_CHUNK, k, SC_CHUNK), jnp.int32),
        compiler_params=pltpu.CompilerParams(dimension_semantics=("arbitrary",)),
        name="dispatch_slots",
    )(idx, rank, expert_start.reshape(n_exp, 1))


def _unpack_bf16_pairs(p):
    hi = lax.bitcast_convert_type(p & jnp.uint32(0xFFFF0000), F32)
    lo = lax.bitcast_convert_type(p << 16, F32)
    return jnp.concatenate([hi, lo], axis=1)


def _pack_bf16_pairs(y):
    n = y.shape[1] // 2
    bits = lax.bitcast_convert_type(y.astype(BF16).astype(F32), jnp.uint32)
    return bits[:, :n] | (bits[:, n:] >> 16)


def _sc_worker_layout(n_chunks_total):
    info = plsc.get_sparse_core_info()
    n_workers = info.num_cores * info.num_subcores
    return info.num_cores, n_chunks_total // n_workers


def _sc_dispatch(slots, x1p, n_rows):
    n_chunks_total, n_k, chunk = slots.shape
    t, w = x1p.shape
    n_cores, n_chunks = _sc_worker_layout(n_chunks_total)
    mesh = plsc.VectorSubcoreMesh(core_axis_name="c", subcore_axis_name="s")

    @functools.partial(
        pl.kernel, mesh=mesh, name="sc_dispatch_rows",
        out_type=jax.ShapeDtypeStruct((n_rows, w), x1p.dtype),
        scratch_types=[pltpu.VMEM((n_k, chunk), jnp.int32), pltpu.VMEM((chunk, w), x1p.dtype),
                       pltpu.SemaphoreType.DMA])
    def dispatch(slots_hbm, x_hbm, xg_hbm, idx_v, rows_v, sem):
        wid = lax.axis_index("s") * n_cores + lax.axis_index("c")

        @pl.loop(0, n_chunks)
        def _(ci):
            chunk_id = wid * n_chunks + ci
            pltpu.sync_copy(slots_hbm.at[chunk_id], idx_v)
            pltpu.sync_copy(x_hbm.at[pl.ds(chunk_id * chunk, chunk)], rows_v)
            copies = [pltpu.make_async_copy(rows_v, xg_hbm.at[idx_v.at[k]], sem) for k in range(n_k)]
            for cp in copies:
                cp.start()
            for cp in copies:
                cp.wait()

    return dispatch(slots, x1p)


def _sc_collect(slots, yg):
    n_chunks_total, n_k, chunk = slots.shape
    w = yg.shape[1]
    t = n_chunks_total * chunk
    n_cores, n_chunks = _sc_worker_layout(n_chunks_total)
    mesh = plsc.VectorSubcoreMesh(core_axis_name="c", subcore_axis_name="s")

    @functools.partial(
        pl.kernel, mesh=mesh, name="sc_collect_rows",
        out_type=jax.ShapeDtypeStruct((n_k, t, w), yg.dtype),
        scratch_types=[pltpu.VMEM((n_k, chunk), jnp.int32), pltpu.VMEM((chunk, w), yg.dtype),
                       pltpu.SemaphoreType.DMA])
    def collect(slots_hbm, yg_hbm, yt_hbm, idx_v, rows_v, sem):
        wid = lax.axis_index("s") * n_cores + lax.axis_index("c")

        @pl.loop(0, n_chunks)
        def _(ci):
            chunk_id = wid * n_chunks + ci
            pltpu.sync_copy(slots_hbm.at[chunk_id], idx_v)
            for k in range(n_k):
                pltpu.async_copy(yg_hbm.at[idx_v.at[k]], rows_v, sem).wait()
                pltpu.sync_copy(rows_v, yt_hbm.at[k, pl.ds(chunk_id * chunk, chunk)])

    return collect(slots, yg)


def _expert_kernel(first_block, n_block, n_used, xg_hbm, w1_ref, w3_ref, w2_ref, yg_hbm,
                   x_buf, y_buf, w1_bf, w3_bf, w2_bf, in_sem, out_sem):
    e = pl.program_id(0)
    n_slot, bm, _ = x_buf.shape
    ahead = n_slot - BLOCK_GROUP
    total = n_used[0]

    def fetch(g):
        slot = g % n_slot
        return pltpu.make_async_copy(xg_hbm.at[pl.ds(g * bm, bm)], x_buf.at[slot], in_sem.at[slot])

    def write_back(g):
        slot = g % n_slot
        return pltpu.make_async_copy(y_buf.at[slot], yg_hbm.at[pl.ds(g * bm, bm)], out_sem.at[slot])

    @pl.when(e == 0)
    def _():
        for g in range(ahead):
            @pl.when(g < total)
            def _():
                fetch(g).start()

    @pl.when(n_block[e] > 0)
    def _():
        w1_bf[...] = w1_ref[0].astype(BF16)
        w3_bf[...] = w3_ref[0].astype(BF16)
        w2_bf[...] = w2_ref[0].astype(BF16)

    def swiglu(slots):
        xs = [_unpack_bf16_pairs(x_buf[s]).astype(BF16) for s in slots]
        up = [(_dot(xb, w1_bf[...]), _dot(xb, w3_bf[...])) for xb in xs]
        hs = [(h1 * jax.nn.sigmoid(h1) * h3).astype(BF16) for h1, h3 in up]
        ys = [_dot(h, w2_bf[...]) for h in hs]
        return [_pack_bf16_pairs(y) for y in ys]

    def process(blocks):
        for g in blocks:
            fetch(g).wait()
        for g in blocks:
            @pl.when(g + ahead < total)
            def _():
                fetch(g + ahead).start()
        ys = swiglu([g % n_slot for g in blocks])
        for g in blocks:
            @pl.when(g >= n_slot)
            def _():
                write_back(g - n_slot).wait()
        for g, y in zip(blocks, ys):
            y_buf[g % n_slot] = y
            write_back(g).start()

    done = 0
    size = BLOCK_GROUP
    while size >= 1:
        left = n_block[e] - done
        start = first_block[e] + done
        if size == BLOCK_GROUP:
            @pl.loop(0, left // size)
            def _(i, start=start, size=size):
                process([start + i * size + j for j in range(size)])
        else:
            @pl.when(left >= size)
            def _(start=start, size=size):
                process([start + j for j in range(size)])
        done = done + left // size * size
        size //= 2

    @pl.when(e == pl.num_programs(0) - 1)
    def _():
        for back in range(1, n_slot + 1):
            @pl.when(total >= back)
            def _():
                write_back(total - back).wait()


def _experts(first_block, n_block, n_used, xg, w1, w3, w2, bm):
    n_rows, w = xg.shape
    n_exp, d, de = w1.shape
    weight = lambda e, *_: (e, 0, 0)
    return pl.pallas_call(
        _expert_kernel,
        grid_spec=pltpu.PrefetchScalarGridSpec(
            num_scalar_prefetch=3,
            grid=(n_exp,),
            in_specs=[pl.BlockSpec(memory_space=pl.ANY),
                      pl.BlockSpec((1, d, de), weight),
                      pl.BlockSpec((1, d, de), weight),
                      pl.BlockSpec((1, de, d), weight)],
            out_specs=pl.BlockSpec(memory_space=pl.ANY),
            scratch_shapes=[pltpu.VMEM((ROW_RING, bm, w), xg.dtype), pltpu.VMEM((ROW_RING, bm, w), jnp.uint32),
                            pltpu.VMEM((d, de), BF16), pltpu.VMEM((d, de), BF16), pltpu.VMEM((de, d), BF16),
                            pltpu.SemaphoreType.DMA((ROW_RING,)), pltpu.SemaphoreType.DMA((ROW_RING,))]),
        out_shape=jax.ShapeDtypeStruct((n_rows, w), jnp.uint32),
        compiler_params=pltpu.CompilerParams(dimension_semantics=("arbitrary",),
                                             vmem_limit_bytes=VMEM_LIMIT),
        name="routed_experts",
    )(first_block, n_block, n_used, xg, w1, w3, w2)


def _combine_kernel(yt_ref, x1_ref, gate_ref, sw1_ref, sw3_ref, sw2_ref, g2_ref, b2_ref, out_ref):
    x1 = x1_ref[...]
    xb = x1.astype(BF16)
    h1 = _dot(xb, sw1_ref[...])
    h3 = _dot(xb, sw3_ref[...])
    ffn = _dot((h1 * jax.nn.sigmoid(h1) * h3).astype(BF16), sw2_ref[...])
    gate = gate_ref[...]
    for k in range(yt_ref.shape[0]):
        ffn = ffn + gate[:, k:k + 1] * _unpack_bf16_pairs(yt_ref[k])
    out_ref[...] = _layer_norm(DEEPNORM_ALPHA * x1 + ffn, g2_ref[...], b2_ref[...])


def _combine(yt, x1, gate, sw1, sw3, sw2, g2, b2, tn=256):
    n_k, t, w = yt.shape
    d = x1.shape[1]
    row = lambda p: p.reshape(1, -1)
    operands = (yt, x1, gate, sw1.astype(BF16), sw3.astype(BF16), sw2.astype(BF16), row(g2), row(b2))
    in_specs = [pl.BlockSpec((n_k, tn, w), lambda i: (0, i, 0)),
                pl.BlockSpec((tn, d), lambda i: (i, 0)),
                pl.BlockSpec((tn, gate.shape[1]), lambda i: (i, 0))]
    in_specs += [_const_spec(op.shape) for op in operands[3:]]
    return pl.pallas_call(
        _combine_kernel,
        grid=(t // tn,),
        in_specs=in_specs,
        out_specs=pl.BlockSpec((tn, d), lambda i: (i, 0)),
        out_shape=jax.ShapeDtypeStruct((t, d), F32),
        compiler_params=pltpu.CompilerParams(dimension_semantics=("arbitrary",),
                                             vmem_limit_bytes=VMEM_LIMIT),
        name="combine_shared_ln",
    )(*operands)


def _block_table(counts, bm):
    n_block = (counts + bm - 1) // bm
    ends = jnp.cumsum(n_block)
    first_block = ends - n_block
    i32 = lambda v: v.astype(jnp.int32)
    return i32(first_block * bm), i32(first_block), i32(n_block), i32(ends[-1:])


def _moe(x1, x1p, logits, router_bias, w1, w3, w2, sw1, sw3, sw2, g2, b2, bm=256):
    t, d = x1.shape
    n_exp = logits.shape[1]
    idx, gate, rank, counts = _route(logits, router_bias)
    n_blocks = (t * TOP_K) // bm + n_exp
    starts, first_block, n_block, n_used = _block_table(counts[:, 0], bm)
    slots = _slots(idx, rank, starts)
    xg = _sc_dispatch(slots, x1p, n_blocks * bm)
    yg = _experts(first_block, n_block, n_used, xg, w1, w3, w2, bm)
    yt = _sc_collect(slots, yg)
    return _combine(yt, x1, gate, sw1, sw3, sw2, g2, b2)


def kernel(x, w_in, conv_w, conv_b, lru_wa, lru_ba, lru_wx, lru_bx, lru_lambda, pool_w, pool_scale, w_out, ln1_g, ln1_b, router_w, router_bias, exp_w1, exp_w3, exp_w2, sh_w1, sh_w3, sh_w2, ln2_g, ln2_b):
    nb, s, d = x.shape
    for l in range(DEPTH):
        x1, x1p, logits = _mixer(x, w_in[l], conv_w[l], conv_b[l], lru_wa[l], lru_ba[l], lru_wx[l], lru_bx[l],
                                 lru_lambda[l], pool_w[l], pool_scale[l], w_out[l], ln1_g[l], ln1_b[l],
                                 router_w[l])
        t = nb * s
        x = _moe(x1.reshape(t, d), x1p.reshape(t, d // 2), logits.reshape(t, -1), router_bias[l],
                 exp_w1[l], exp_w3[l], exp_w2[l], sh_w1[l], sh_w3[l], sh_w2[l], ln2_g[l], ln2_b[l])
        x = x.reshape(nb, s, d)
    return x
```

```python
import functools
import math

import jax
import jax.numpy as jnp
from jax import lax
from jax.experimental import pallas as pl
from jax.experimental.pallas import tpu as pltpu
from jax.experimental.pallas import tpu_sc as plsc

LRU_HEADS = 8
CONV_WIDTH = 4
LRU_C = 8.0
POOL_WINDOWS = (2, 4, 8, 16)
N_EXPERT_GROUPS = 8
TOPK_GROUPS = 4
TOP_K = 8
ROUTED_SCALE = 2.5
LN_EPS = 1e-5
DEPTH = 1
DEEPNORM_ALPHA = (2.0 * DEPTH) ** 0.25

MXU_DIM = 256
POOL_GROUP_DIM = 128
CONV_HIST = 8
POOL_HIST = 16
GATE_LANES = 128
SC_CHUNK = 128
MIXER_PARTS = 4
WEIGHT_BUFFERS = 3
BLOCK_GROUP = 4
ROW_RING = 8
VMEM_LIMIT = 56 * 1024 * 1024

F32 = jnp.float32
BF16 = jnp.bfloat16


def _dot(a, b):
    return jnp.dot(a, b, preferred_element_type=F32)


def _layer_norm(z, g, b):
    mu = jnp.mean(z, axis=-1, keepdims=True)
    zc = z - mu
    var = jnp.mean(zc * zc, axis=-1, keepdims=True)
    return zc * lax.rsqrt(var + LN_EPS) * g + b


def _block_diag_dot(xb, w_ref):
    n = w_ref.shape[0]
    return jnp.concatenate(
        [_dot(xb[:, i * MXU_DIM:(i + 1) * MXU_DIM], w_ref[i]) for i in range(n)], axis=1)


def _causal_conv(u, hist_ref, part, conv_w, conv_b):
    pb, ts, c = u.shape
    ext = jnp.concatenate([hist_ref[part], u], axis=1)
    hist_ref[part] = u[:, ts - CONV_HIST:, :]
    uc = jnp.broadcast_to(conv_b.reshape(1, 1, c), (pb, ts, c))
    for k in range(CONV_WIDTH):
        off = CONV_HIST - (CONV_WIDTH - 1) + k
        uc = uc + conv_w[k:k + 1, :].reshape(1, 1, c) * ext[:, off:off + ts, :]
    return uc


def _lru_scan(uc, ga, gx, lam, carry_ref, part, t0):
    pb, ts, c = uc.shape
    r = jax.nn.sigmoid(ga)
    i_gate = jax.nn.sigmoid(gx)
    softplus_neg_lam = jnp.maximum(-lam, 0.0) + jnp.log1p(jnp.exp(-jnp.abs(lam)))
    log_a = (-LRU_C) * r * softplus_neg_lam
    a = jnp.exp(log_a)
    mult = jnp.sqrt(jnp.tanh(-log_a) * (1.0 + a * a))
    t_local = lax.broadcasted_iota(jnp.int32, (pb, ts, c), 1)
    first = (t_local + t0) == 0
    a3 = a.reshape(pb, ts, c)
    x3 = jnp.where(first, 1.0, mult.reshape(pb, ts, c)) * i_gate.reshape(pb, ts, c) * uc

    shift = 1
    while shift < ts:
        valid = t_local >= shift
        a_prev = pltpu.roll(a3, shift, axis=1)
        x_prev = pltpu.roll(x3, shift, axis=1)
        x3 = jnp.where(valid, a3 * x_prev, 0.0) + x3
        a3 = jnp.where(valid, a3 * a_prev, a3)
        shift *= 2
    h = x3 + a3 * carry_ref[part][:, None, :]
    carry_ref[part] = h[:, ts - 1, :]
    return h


def _multiscale_pool(v, hist_ref, part, t0):
    pb, ts, c = v.shape
    ext = jnp.concatenate([hist_ref[part], v], axis=1)
    hist_ref[part] = v[:, ts - POOL_HIST:, :]
    t_glob = lax.broadcasted_iota(jnp.int32, (1, ts, POOL_GROUP_DIM), 1) + t0
    pooled = []
    for g, win in enumerate(POOL_WINDOWS):
        lanes = slice(g * POOL_GROUP_DIM, (g + 1) * POOL_GROUP_DIM)
        s = ext[:, :, lanes]
        shift = 1
        while shift < win:
            s = s + pltpu.roll(s, shift, axis=1)
            shift *= 2
        count = jnp.minimum(t_glob + 1, win).astype(F32)
        pooled.append(s[:, POOL_HIST:, :] / count - v[:, :, lanes])
    return jnp.concatenate(pooled, axis=-1)


def _mixer_kernel(x_ref, w_in_ref, conv_w_ref, conv_b_ref, wa_ref, ba_ref, wx_ref, bx_ref, lam_ref,
                  pool_w_ref, pool_scale_ref, w_out_ref, g1_ref, b1_ref, rw_hi_ref, rw_lo_ref,
                  x1_ref, x1p_ref, logit_ref, hist_u, hist_v, carry):
    j = pl.program_id(0)
    nb, ts, d = x_ref.shape
    c = hist_u.shape[-1]
    pb = nb // MIXER_PARTS
    rows = pb * ts
    t0 = j * ts
    parts = [slice(p * pb, (p + 1) * pb) for p in range(MIXER_PARTS)]

    @pl.when(j == 0)
    def _():
        hist_u[...] = jnp.zeros_like(hist_u)
        hist_v[...] = jnp.zeros_like(hist_v)
        carry[...] = jnp.zeros_like(carry)

    xs = [x_ref[p].reshape(rows, d) for p in parts]
    projs = [_dot(x.astype(BF16), w_in_ref[...]) for x in xs]
    ucs = [_causal_conv(proj[:, :c].reshape(pb, ts, c), hist_u, p, conv_w_ref[...], conv_b_ref[...])
           for proj, p in zip(projs, parts)]
    ucbs = [uc.reshape(rows, c).astype(BF16) for uc in ucs]
    gas = [_block_diag_dot(ucb, wa_ref) + ba_ref[...] for ucb in ucbs]
    gxs = [_block_diag_dot(ucb, wx_ref) + bx_ref[...] for ucb in ucbs]
    hs = [_lru_scan(uc, ga, gx, lam_ref[...], carry, p, t0) for uc, ga, gx, p in zip(ucs, gas, gxs, parts)]
    y_lrus = []
    for h, proj in zip(hs, projs):
        gate = proj[:, c:2 * c]
        gelu_gate = 0.5 * gate * (1.0 + jnp.tanh(math.sqrt(2.0 / math.pi) * (gate + 0.044715 * gate * gate * gate)))
        y_lrus.append(h.reshape(rows, c) * gelu_gate)
    pooled = [_multiscale_pool(proj[:, 2 * c:].reshape(pb, ts, c), hist_v, p, t0).reshape(rows, c).astype(BF16)
              for proj, p in zip(projs, parts)]
    y_pools = [_block_diag_dot(pl_, pool_w_ref) * pool_scale_ref[...] for pl_ in pooled]
    mixes = [_dot(jnp.concatenate([y_lru, y_pool], axis=1).astype(BF16), w_out_ref[...])
             for y_lru, y_pool in zip(y_lrus, y_pools)]
    x1s = [_layer_norm(DEEPNORM_ALPHA * x + mix, g1_ref[...], b1_ref[...]) for x, mix in zip(xs, mixes)]

    for p, x1 in zip(parts, x1s):
        x1_ref[p] = x1.reshape(pb, ts, d)
        hi = x1.astype(BF16)
        hi_f = hi.astype(F32)
        lo = (x1 - hi_f).astype(BF16)
        logits = _dot(hi, rw_hi_ref[...]) + _dot(lo, rw_hi_ref[...]) + _dot(hi, rw_lo_ref[...])
        logit_ref[p] = logits.reshape(pb, ts, logits.shape[-1])
        bits = lax.bitcast_convert_type(hi_f, jnp.uint32)
        packed = bits[:, :d // 2] | (bits[:, d // 2:] >> 16)
        x1p_ref[p] = packed.reshape(pb, ts, d // 2)


def _const_spec(shape):
    return pl.BlockSpec(shape, lambda j: (0,) * len(shape), pipeline_mode=pl.Buffered(1))


def _regroup_block_diag(w, width):
    heads, hd, _ = w.shape
    per = width // hd
    w = w.reshape(heads // per, per, hd, hd)
    eye = jnp.eye(per, dtype=w.dtype)
    return jnp.einsum("gpij,pq->gpiqj", w, eye).reshape(heads // per, width, width)


def _mixer(x, w_in, conv_w, conv_b, wa, ba, wx, bx, lam, pool_w, pool_scale, w_out, g1, b1, router_w, ts=64):
    nb, s, d = x.shape
    c = conv_w.shape[-1]
    n_exp = router_w.shape[-1]
    rw_hi = router_w.astype(BF16)
    rw_lo = (router_w - rw_hi.astype(F32)).astype(BF16)
    row = lambda p: p.reshape(1, -1)
    operands = (
        x, w_in.astype(BF16), conv_w, row(conv_b),
        _regroup_block_diag(wa, MXU_DIM).astype(BF16), row(ba),
        _regroup_block_diag(wx, MXU_DIM).astype(BF16), row(bx), row(lam),
        _regroup_block_diag(pool_w, MXU_DIM).astype(BF16), row(pool_scale),
        w_out.astype(BF16), row(g1), row(b1), rw_hi, rw_lo)
    tile = lambda width: pl.BlockSpec((nb, ts, width), lambda j: (0, j, 0))
    in_specs = [tile(d)] + [_const_spec(op.shape) for op in operands[1:]]
    return pl.pallas_call(
        _mixer_kernel,
        grid=(s // ts,),
        in_specs=in_specs,
        out_specs=[tile(d), tile(d // 2), tile(n_exp)],
        out_shape=[jax.ShapeDtypeStruct((nb, s, d), F32),
                   jax.ShapeDtypeStruct((nb, s, d // 2), jnp.uint32),
                   jax.ShapeDtypeStruct((nb, s, n_exp), F32)],
        scratch_shapes=[pltpu.VMEM((nb, CONV_HIST, c), F32),
                        pltpu.VMEM((nb, POOL_HIST, c), F32),
                        pltpu.VMEM((nb, c), F32)],
        compiler_params=pltpu.CompilerParams(dimension_semantics=("arbitrary",),
                                             vmem_limit_bytes=VMEM_LIMIT),
        name="mixer_ln_router",
    )(*operands)


def _first_row_of(cond, rows_rev, n):
    return n - jnp.max(jnp.where(cond, rows_rev, 0.0), axis=0, keepdims=True)


def _route_kernel(logit_ref, bias_ref, idx_ref, gate_ref, rank_ref, count_ref, count_scr):
    i = pl.program_id(0)

    @pl.when(i == 0)
    def _():
        count_scr[...] = jnp.zeros_like(count_scr)

    scores = jax.nn.sigmoid(logit_ref[...].T)
    n_exp, tn = scores.shape
    gsz = n_exp // N_EXPERT_GROUPS
    neg = -jnp.inf
    biased = scores + bias_ref[...]
    row = lax.broadcasted_iota(jnp.int32, (n_exp, tn), 0).astype(F32)
    row_rev = n_exp - row

    g_row = lax.broadcasted_iota(jnp.int32, (N_EXPERT_GROUPS, tn), 0)
    group_score = jnp.zeros((N_EXPERT_GROUPS, tn), F32)
    grp_row = lax.broadcasted_iota(jnp.int32, (gsz, tn), 0).astype(F32)
    grp_rev = gsz - grp_row
    for g in range(N_EXPERT_GROUPS):
        blk = biased[g * gsz:(g + 1) * gsz]
        m1 = jnp.max(blk, axis=0, keepdims=True)
        i1 = _first_row_of(blk == m1, grp_rev, gsz)
        m2 = jnp.max(jnp.where(grp_row == i1, neg, blk), axis=0, keepdims=True)
        group_score = jnp.where(g_row == g, m1 + m2, group_score)

    beaten_by = jnp.zeros((N_EXPERT_GROUPS, tn), F32)
    for g in range(N_EXPERT_GROUPS):
        other = jnp.max(jnp.where(g_row == g, group_score, neg), axis=0, keepdims=True)
        wins = (other > group_score) | ((other == group_score) & (g < g_row))
        beaten_by = beaten_by + jnp.where(wins, 1.0, 0.0)
    masked = []
    for g in range(N_EXPERT_GROUPS):
        beaten_g = jnp.max(jnp.where(g_row == g, beaten_by, 0.0), axis=0, keepdims=True)
        masked.append(jnp.where(beaten_g < TOPK_GROUPS, biased[g * gsz:(g + 1) * gsz], neg))
    masked = jnp.concatenate(masked, axis=0)

    k_row = lax.broadcasted_iota(jnp.int32, (TOP_K, tn), 0)
    selected = jnp.zeros((n_exp, tn), F32)
    idx = jnp.zeros((TOP_K, tn), F32)
    gate = jnp.zeros((TOP_K, tn), F32)
    hits = []
    for k in range(TOP_K):
        m = jnp.max(masked, axis=0, keepdims=True)
        ik = _first_row_of(masked == m, row_rev, n_exp)
        hit = row == ik
        gk = jnp.sum(jnp.where(hit, scores, 0.0), axis=0, keepdims=True)
        masked = jnp.where(hit, neg, masked)
        selected = jnp.where(hit, 1.0, selected)
        idx = jnp.where(k_row == k, ik, idx)
        gate = jnp.where(k_row == k, gk, gate)
        hits.append(hit)
    gate = gate / jnp.sum(gate, axis=0, keepdims=True) * ROUTED_SCALE

    earlier = (lax.broadcasted_iota(jnp.int32, (tn, tn), 0) < lax.broadcasted_iota(jnp.int32, (tn, tn), 1))
    before = _dot(selected.astype(BF16), jnp.where(earlier, 1.0, 0.0).astype(BF16)) + count_scr[...]
    rank = jnp.zeros((TOP_K, tn), F32)
    for k in range(TOP_K):
        rk = jnp.sum(jnp.where(hits[k], before, 0.0), axis=0, keepdims=True)
        rank = jnp.where(k_row == k, rk, rank)
    count_scr[...] = count_scr[...] + jnp.sum(selected, axis=1, keepdims=True)

    idx_ref[...] = idx.astype(jnp.int32)
    pad = jnp.zeros((gate_ref.shape[1] - TOP_K, tn), F32)
    gate_ref[...] = jnp.concatenate([gate, pad], axis=0).T
    rank_ref[...] = rank.astype(jnp.int32)
    count_ref[...] = count_scr[...].astype(jnp.int32)


def _route(logits, bias, tn=512):
    t, n_exp = logits.shape
    return pl.pallas_call(
        _route_kernel,
        grid=(t // tn,),
        in_specs=[pl.BlockSpec((tn, n_exp), lambda i: (i, 0)),
                  pl.BlockSpec((n_exp, 1), lambda i: (0, 0))],
        out_specs=[pl.BlockSpec((TOP_K, tn), lambda i: (0, i)),
                   pl.BlockSpec((tn, GATE_LANES), lambda i: (i, 0)),
                   pl.BlockSpec((TOP_K, tn), lambda i: (0, i)),
                   pl.BlockSpec((n_exp, 1), lambda i: (0, 0))],
        out_shape=[jax.ShapeDtypeStruct((TOP_K, t), jnp.int32),
                   jax.ShapeDtypeStruct((t, GATE_LANES), F32),
                   jax.ShapeDtypeStruct((TOP_K, t), jnp.int32),
                   jax.ShapeDtypeStruct((n_exp, 1), jnp.int32)],
        scratch_shapes=[pltpu.VMEM((n_exp, 1), F32)],
        compiler_params=pltpu.CompilerParams(dimension_semantics=("arbitrary",),
                                             vmem_limit_bytes=VMEM_LIMIT),
        name="route_topk_rank",
    )(logits, bias.reshape(n_exp, 1))


def _slot_kernel(idx_ref, rank_ref, start_ref, slot_ref):
    idx = idx_ref[...]
    n_exp = start_ref.shape[0]
    tn = idx.shape[1]
    row = lax.broadcasted_iota(jnp.int32, (n_exp, tn), 0)
    start = start_ref[...]
    first = [jnp.sum(jnp.where(row == idx[k:k + 1], start, 0), axis=0, keepdims=True) for k in range(TOP_K)]
    slots = jnp.concatenate(first, axis=0) + rank_ref[...]
    chunk = slot_ref.shape[-1]
    for c in range(slot_ref.shape[0]):
        slot_ref[c] = slots[:, c * chunk:(c + 1) * chunk]


def _slots(idx, rank, expert_start, tn=512):
    k, t = idx.shape
    n_exp = expert_start.shape[0]
    per_step = tn // SC_CHUNK
    return pl.pallas_call(
        _slot_kernel,
        grid=(t // tn,),
        in_specs=[pl.BlockSpec((k, tn), lambda i: (0, i)),
                  pl.BlockSpec((k, tn), lambda i: (0, i)),
                  pl.BlockSpec((n_exp, 1), lambda i: (0, 0))],
        out_specs=pl.BlockSpec((per_step, k, SC_CHUNK), lambda i: (i, 0, 0)),
        out_shape=jax.ShapeDtypeStruct((t // SC_CHUNK, k, SC_CHUNK), jnp.int32),
        compiler_params=pltpu.CompilerParams(dimension_semantics=("arbitrary",)),
        name="dispatch_slots",
    )(idx, rank, expert_start.reshape(n_exp, 1))


def _unpack_bf16_pairs(p):
    hi = lax.bitcast_convert_type(p & jnp.uint32(0xFFFF0000), F32)
    lo = lax.bitcast_convert_type(p << 16, F32)
    return jnp.concatenate([hi, lo], axis=1)


def _pack_bf16_pairs(y):
    n = y.shape[1] // 2
    bits = lax.bitcast_convert_type(y.astype(BF16).astype(F32), jnp.uint32)
    return bits[:, :n] | (bits[:, n:] >> 16)


def _sc_worker_layout(n_chunks_total):
    info = plsc.get_sparse_core_info()
    n_workers = info.num_cores * info.num_subcores
    return info.num_cores, n_chunks_total // n_workers


def _sc_dispatch(slots, x1p, n_rows):
    n_chunks_total, n_k, chunk = slots.shape
    t, w = x1p.shape
    n_cores, n_chunks = _sc_worker_layout(n_chunks_total)
    mesh = plsc.VectorSubcoreMesh(core_axis_name="c", subcore_axis_name="s")

    @functools.partial(
        pl.kernel, mesh=mesh, name="sc_dispatch_rows",
        out_type=jax.ShapeDtypeStruct((n_rows, w), x1p.dtype),
        scratch_types=[pltpu.VMEM((n_k, chunk), jnp.int32), pltpu.VMEM((chunk, w), x1p.dtype),
                       pltpu.SemaphoreType.DMA])
    def dispatch(slots_hbm, x_hbm, xg_hbm, idx_v, rows_v, sem):
        wid = lax.axis_index("s") * n_cores + lax.axis_index("c")

        @pl.loop(0, n_chunks)
        def _(ci):
            chunk_id = wid * n_chunks + ci
            pltpu.sync_copy(slots_hbm.at[chunk_id], idx_v)
            pltpu.sync_copy(x_hbm.at[pl.ds(chunk_id * chunk, chunk)], rows_v)
            copies = [pltpu.make_async_copy(rows_v, xg_hbm.at[idx_v.at[k]], sem) for k in range(n_k)]
            for cp in copies:
                cp.start()
            for cp in copies:
                cp.wait()

    return dispatch(slots, x1p)


def _sc_collect(slots, yg):
    n_chunks_total, n_k, chunk = slots.shape
    w = yg.shape[1]
    t = n_chunks_total * chunk
    n_cores, n_chunks = _sc_worker_layout(n_chunks_total)
    mesh = plsc.VectorSubcoreMesh(core_axis_name="c", subcore_axis_name="s")

    @functools.partial(
        pl.kernel, mesh=mesh, name="sc_collect_rows",
        out_type=jax.ShapeDtypeStruct((n_k, t, w), yg.dtype),
        scratch_types=[pltpu.VMEM((n_k, chunk), jnp.int32), pltpu.VMEM((chunk, w), yg.dtype),
                       pltpu.SemaphoreType.DMA])
    def collect(slots_hbm, yg_hbm, yt_hbm, idx_v, rows_v, sem):
        wid = lax.axis_index("s") * n_cores + lax.axis_index("c")

        @pl.loop(0, n_chunks)
        def _(ci):
            chunk_id = wid * n_chunks + ci
            pltpu.sync_copy(slots_hbm.at[chunk_id], idx_v)
            for k in range(n_k):
                pltpu.async_copy(yg_hbm.at[idx_v.at[k]], rows_v, sem).wait()
                pltpu.sync_copy(rows_v, yt_hbm.at[k, pl.ds(chunk_id * chunk, chunk)])

    return collect(slots, yg)


def _expert_kernel(first_block, n_block, n_used, xg_hbm, w1_hbm, w3_hbm, w2_hbm, yg_hbm,
                   x_buf, y_buf, w1_f32, w3_f32, w2_f32, w1_bf, w3_bf, w2_bf, in_sem, out_sem, w_sem):
    e = pl.program_id(0)
    n_exp = pl.num_programs(0)
    n_slot, bm, _ = x_buf.shape
    ahead = n_slot - BLOCK_GROUP
    total = n_used[0]

    def fetch(g):
        slot = g % n_slot
        return pltpu.make_async_copy(xg_hbm.at[pl.ds(g * bm, bm)], x_buf.at[slot], in_sem.at[slot])

    def write_back(g):
        slot = g % n_slot
        return pltpu.make_async_copy(y_buf.at[slot], yg_hbm.at[pl.ds(g * bm, bm)], out_sem.at[slot])

    def weight_fetches(ex):
        slot = ex % WEIGHT_BUFFERS
        return [pltpu.make_async_copy(hbm.at[ex], buf.at[slot], w_sem.at[i, slot])
                for i, (hbm, buf) in enumerate(((w1_hbm, w1_f32), (w3_hbm, w3_f32), (w2_hbm, w2_f32)))]

    @pl.when(e == 0)
    def _():
        for g in range(ahead):
            @pl.when(g < total)
            def _():
                fetch(g).start()
        for ex in range(WEIGHT_BUFFERS - 1):
            for cp in weight_fetches(ex):
                cp.start()

    @pl.when(e + WEIGHT_BUFFERS - 1 < n_exp)
    def _():
        for cp in weight_fetches(e + WEIGHT_BUFFERS - 1):
            cp.start()

    for cp in weight_fetches(e):
        cp.wait()
    w_slot = e % WEIGHT_BUFFERS

    @pl.when(n_block[e] > 0)
    def _():
        w1_bf[...] = w1_f32[w_slot].astype(BF16)
        w3_bf[...] = w3_f32[w_slot].astype(BF16)
        w2_bf[...] = w2_f32[w_slot].astype(BF16)

    def swiglu(slots):
        xs = [_unpack_bf16_pairs(x_buf[s]).astype(BF16) for s in slots]
        up = [(_dot(xb, w1_bf[...]), _dot(xb, w3_bf[...])) for xb in xs]
        hs = [(h1 * jax.nn.sigmoid(h1) * h3).astype(BF16) for h1, h3 in up]
        ys = [_dot(h, w2_bf[...]) for h in hs]
        return [_pack_bf16_pairs(y) for y in ys]

    def process(blocks):
        for g in blocks:
            fetch(g).wait()
        for g in blocks:
            @pl.when(g + ahead < total)
            def _():
                fetch(g + ahead).start()
        ys = swiglu([g % n_slot for g in blocks])
        for g in blocks:
            @pl.when(g >= n_slot)
            def _():
                write_back(g - n_slot).wait()
        for g, y in zip(blocks, ys):
            y_buf[g % n_slot] = y
            write_back(g).start()

    done = 0
    size = BLOCK_GROUP
    while size >= 1:
        left = n_block[e] - done
        start = first_block[e] + done
        if size == BLOCK_GROUP:
            @pl.loop(0, left // size)
            def _(i, start=start, size=size):
                process([start + i * size + j for j in range(size)])
        else:
            @pl.when(left >= size)
            def _(start=start, size=size):
                process([start + j for j in range(size)])
        done = done + left // size * size
        size //= 2

    @pl.when(e == n_exp - 1)
    def _():
        for back in range(1, n_slot + 1):
            @pl.when(total >= back)
            def _():
                write_back(total - back).wait()


def _experts(first_block, n_block, n_used, xg, w1, w3, w2, bm):
    n_rows, w = xg.shape
    n_exp, d, de = w1.shape
    assert n_exp >= WEIGHT_BUFFERS
    hbm = pl.BlockSpec(memory_space=pl.ANY)
    return pl.pallas_call(
        _expert_kernel,
        grid_spec=pltpu.PrefetchScalarGridSpec(
            num_scalar_prefetch=3,
            grid=(n_exp,),
            in_specs=[hbm, hbm, hbm, hbm],
            out_specs=hbm,
            scratch_shapes=[pltpu.VMEM((ROW_RING, bm, w), xg.dtype), pltpu.VMEM((ROW_RING, bm, w), jnp.uint32),
                            pltpu.VMEM((WEIGHT_BUFFERS, d, de), F32), pltpu.VMEM((WEIGHT_BUFFERS, d, de), F32),
                            pltpu.VMEM((WEIGHT_BUFFERS, de, d), F32),
                            pltpu.VMEM((d, de), BF16), pltpu.VMEM((d, de), BF16), pltpu.VMEM((de, d), BF16),
                            pltpu.SemaphoreType.DMA((ROW_RING,)), pltpu.SemaphoreType.DMA((ROW_RING,)),
                            pltpu.SemaphoreType.DMA((3, WEIGHT_BUFFERS))]),
        out_shape=jax.ShapeDtypeStruct((n_rows, w), jnp.uint32),
        compiler_params=pltpu.CompilerParams(dimension_semantics=("arbitrary",),
                                             vmem_limit_bytes=VMEM_LIMIT),
        name="routed_experts",
    )(first_block, n_block, n_used, xg, w1, w3, w2)


def _combine_kernel(yt_ref, x1_ref, gate_ref, sw1_ref, sw3_ref, sw2_ref, g2_ref, b2_ref, out_ref):
    x1 = x1_ref[...]
    xb = x1.astype(BF16)
    h1 = _dot(xb, sw1_ref[...])
    h3 = _dot(xb, sw3_ref[...])
    ffn = _dot((h1 * jax.nn.sigmoid(h1) * h3).astype(BF16), sw2_ref[...])
    gate = gate_ref[...]
    for k in range(yt_ref.shape[0]):
        ffn = ffn + gate[:, k:k + 1] * _unpack_bf16_pairs(yt_ref[k])
    out_ref[...] = _layer_norm(DEEPNORM_ALPHA * x1 + ffn, g2_ref[...], b2_ref[...])


def _combine(yt, x1, gate, sw1, sw3, sw2, g2, b2, tn=256):
    n_k, t, w = yt.shape
    d = x1.shape[1]
    row = lambda p: p.reshape(1, -1)
    operands = (yt, x1, gate, sw1.astype(BF16), sw3.astype(BF16), sw2.astype(BF16), row(g2), row(b2))
    in_specs = [pl.BlockSpec((n_k, tn, w), lambda i: (0, i, 0)),
                pl.BlockSpec((tn, d), lambda i: (i, 0)),
                pl.BlockSpec((tn, gate.shape[1]), lambda i: (i, 0))]
    in_specs += [_const_spec(op.shape) for op in operands[3:]]
    return pl.pallas_call(
        _combine_kernel,
        grid=(t // tn,),
        in_specs=in_specs,
        out_specs=pl.BlockSpec((tn, d), lambda i: (i, 0)),
        out_shape=jax.ShapeDtypeStruct((t, d), F32),
        compiler_params=pltpu.CompilerParams(dimension_semantics=("arbitrary",),
                                             vmem_limit_bytes=VMEM_LIMIT),
        name="combine_shared_ln",
    )(*operands)


def _block_table(counts, bm):
    n_block = (counts + bm - 1) // bm
    ends = jnp.cumsum(n_block)
    first_block = ends - n_block
    i32 = lambda v: v.astype(jnp.int32)
    return i32(first_block * bm), i32(first_block), i32(n_block), i32(ends[-1:])


def _moe(x1, x1p, logits, router_bias, w1, w3, w2, sw1, sw3, sw2, g2, b2, bm=256):
    t, d = x1.shape
    n_exp = logits.shape[1]
    idx, gate, rank, counts = _route(logits, router_bias)
    n_blocks = (t * TOP_K) // bm + n_exp
    starts, first_block, n_block, n_used = _block_table(counts[:, 0], bm)
    slots = _slots(idx, rank, starts)
    xg = _sc_dispatch(slots, x1p, n_blocks * bm)
    yg = _experts(first_block, n_block, n_used, xg, w1, w3, w2, bm)
    yt = _sc_collect(slots, yg)
    return _combine(yt, x1, gate, sw1, sw3, sw2, g2, b2)


def kernel(x, w_in, conv_w, conv_b, lru_wa, lru_ba, lru_wx, lru_bx, lru_lambda, pool_w, pool_scale, w_out, ln1_g, ln1_b, router_w, router_bias, exp_w1, exp_w3, exp_w2, sh_w1, sh_w3, sh_w2, ln2_g, ln2_b):
    nb, s, d = x.shape
    for l in range(DEPTH):
        x1, x1p, logits = _mixer(x, w_in[l], conv_w[l], conv_b[l], lru_wa[l], lru_ba[l], lru_wx[l], lru_bx[l],
                                 lru_lambda[l], pool_w[l], pool_scale[l], w_out[l], ln1_g[l], ln1_b[l],
                                 router_w[l])
        t = nb * s
        x = _moe(x1.reshape(t, d), x1p.reshape(t, d // 2), logits.reshape(t, -1), router_bias[l],
                 exp_w1[l], exp_w3[l], exp_w2[l], sh_w1[l], sh_w3[l], sh_w2[l], ln2_g[l], ln2_b[l])
        x = x.reshape(nb, s, d)
    return x
```

```python
import functools
import math

import jax
import jax.numpy as jnp
from jax import lax
from jax.experimental import pallas as pl
from jax.experimental.pallas import tpu as pltpu
from jax.experimental.pallas import tpu_sc as plsc

LRU_HEADS = 8
CONV_WIDTH = 4
LRU_C = 8.0
POOL_WINDOWS = (2, 4, 8, 16)
N_EXPERT_GROUPS = 8
TOPK_GROUPS = 4
TOP_K = 8
ROUTED_SCALE = 2.5
LN_EPS = 1e-5
DEPTH = 1
DEEPNORM_ALPHA = (2.0 * DEPTH) ** 0.25

MXU_DIM = 256
POOL_GROUP_DIM = 128
CONV_HIST = 8
POOL_HIST = 16
SC_CHUNK = 128
SC_SUB = 8
MIXER_PARTS = 2
BLOCK_GROUP = 4
ROW_RING = 8
VMEM_LIMIT = 56 * 1024 * 1024

F32 = jnp.float32
BF16 = jnp.bfloat16


def _dot(a, b):
    return jnp.dot(a, b, preferred_element_type=F32)


def _layer_norm(z, g, b):
    mu = jnp.mean(z, axis=-1, keepdims=True)
    zc = z - mu
    var = jnp.mean(zc * zc, axis=-1, keepdims=True)
    return zc * lax.rsqrt(var + LN_EPS) * g + b


def _block_diag_dot(xb, w_ref):
    n = w_ref.shape[0]
    return jnp.concatenate(
        [_dot(xb[:, i * MXU_DIM:(i + 1) * MXU_DIM], w_ref[i]) for i in range(n)], axis=1)


def _causal_conv(u, hist_ref, part, conv_w, conv_b):
    pb, ts, c = u.shape
    ext = jnp.concatenate([hist_ref[part], u], axis=1)
    hist_ref[part] = u[:, ts - CONV_HIST:, :]
    uc = jnp.broadcast_to(conv_b.reshape(1, 1, c), (pb, ts, c))
    for k in range(CONV_WIDTH):
        off = CONV_HIST - (CONV_WIDTH - 1) + k
        uc = uc + conv_w[k:k + 1, :].reshape(1, 1, c) * ext[:, off:off + ts, :]
    return uc


def _lru_scan(uc, ga, gx, lam, carry_ref, part, t0):
    pb, ts, c = uc.shape
    r = jax.nn.sigmoid(ga)
    i_gate = jax.nn.sigmoid(gx)
    softplus_neg_lam = jnp.maximum(-lam, 0.0) + jnp.log1p(jnp.exp(-jnp.abs(lam)))
    log_a = (-LRU_C) * r * softplus_neg_lam
    a = jnp.exp(log_a)
    mult = jnp.sqrt(jnp.tanh(-log_a) * (1.0 + a * a))
    t_local = lax.broadcasted_iota(jnp.int32, (pb, ts, c), 1)
    first = (t_local + t0) == 0
    a3 = a.reshape(pb, ts, c)
    x3 = jnp.where(first, 1.0, mult.reshape(pb, ts, c)) * i_gate.reshape(pb, ts, c) * uc

    shift = 1
    while shift < ts:
        valid = t_local >= shift
        a_prev = pltpu.roll(a3, shift, axis=1)
        x_prev = pltpu.roll(x3, shift, axis=1)
        x3 = jnp.where(valid, a3 * x_prev, 0.0) + x3
        a3 = jnp.where(valid, a3 * a_prev, a3)
        shift *= 2
    h = x3 + a3 * carry_ref[part][:, None, :]
    carry_ref[part] = h[:, ts - 1, :]
    return h


def _multiscale_pool(v, hist_ref, part, t0):
    pb, ts, c = v.shape
    ext = jnp.concatenate([hist_ref[part], v], axis=1)
    hist_ref[part] = v[:, ts - POOL_HIST:, :]
    t_glob = lax.broadcasted_iota(jnp.int32, (1, ts, POOL_GROUP_DIM), 1) + t0
    pooled = []
    for g, win in enumerate(POOL_WINDOWS):
        lanes = slice(g * POOL_GROUP_DIM, (g + 1) * POOL_GROUP_DIM)
        s = ext[:, :, lanes]
        shift = 1
        while shift < win:
            s = s + pltpu.roll(s, shift, axis=1)
            shift *= 2
        count = jnp.minimum(t_glob + 1, win).astype(F32)
        pooled.append(s[:, POOL_HIST:, :] / count - v[:, :, lanes])
    return jnp.concatenate(pooled, axis=-1)


def _mixer_kernel(x_ref, w_in_ref, conv_w_ref, conv_b_ref, wa_ref, ba_ref, wx_ref, bx_ref, lam_ref,
                  pool_w_ref, pool_scale_ref, w_out_ref, g1_ref, b1_ref, rw_hi_ref, rw_lo_ref,
                  x1_ref, x1p_ref, logit_ref, hist_u, hist_v, carry):
    j = pl.program_id(0)
    nb, ts, d = x_ref.shape
    c = hist_u.shape[-1]
    pb = nb // MIXER_PARTS
    rows = pb * ts
    t0 = j * ts
    parts = [slice(p * pb, (p + 1) * pb) for p in range(MIXER_PARTS)]

    @pl.when(j == 0)
    def _():
        hist_u[...] = jnp.zeros_like(hist_u)
        hist_v[...] = jnp.zeros_like(hist_v)
        carry[...] = jnp.zeros_like(carry)

    xs = [x_ref[p].reshape(rows, d) for p in parts]
    projs = [_dot(x.astype(BF16), w_in_ref[...]) for x in xs]
    ucs = [_causal_conv(proj[:, :c].reshape(pb, ts, c), hist_u, p, conv_w_ref[...], conv_b_ref[...])
           for proj, p in zip(projs, parts)]
    ucbs = [uc.reshape(rows, c).astype(BF16) for uc in ucs]
    gas = [_block_diag_dot(ucb, wa_ref) + ba_ref[...] for ucb in ucbs]
    gxs = [_block_diag_dot(ucb, wx_ref) + bx_ref[...] for ucb in ucbs]
    hs = [_lru_scan(uc, ga, gx, lam_ref[...], carry, p, t0) for uc, ga, gx, p in zip(ucs, gas, gxs, parts)]
    y_lrus = []
    for h, proj in zip(hs, projs):
        gate = proj[:, c:2 * c]
        gelu_gate = 0.5 * gate * (1.0 + jnp.tanh(math.sqrt(2.0 / math.pi) * (gate + 0.044715 * gate * gate * gate)))
        y_lrus.append(h.reshape(rows, c) * gelu_gate)
    pooled = [_multiscale_pool(proj[:, 2 * c:].reshape(pb, ts, c), hist_v, p, t0).reshape(rows, c).astype(BF16)
              for proj, p in zip(projs, parts)]
    y_pools = [_block_diag_dot(pl_, pool_w_ref) * pool_scale_ref[...] for pl_ in pooled]
    mixes = [_dot(jnp.concatenate([y_lru, y_pool], axis=1).astype(BF16), w_out_ref[...])
             for y_lru, y_pool in zip(y_lrus, y_pools)]
    x1s = [_layer_norm(DEEPNORM_ALPHA * x + mix, g1_ref[...], b1_ref[...]) for x, mix in zip(xs, mixes)]

    for p, x1 in zip(parts, x1s):
        x1_ref[p] = x1.reshape(pb, ts, d)
        hi = x1.astype(BF16)
        hi_f = hi.astype(F32)
        lo = (x1 - hi_f).astype(BF16)
        logits = _dot(hi, rw_hi_ref[...]) + _dot(lo, rw_hi_ref[...]) + _dot(hi, rw_lo_ref[...])
        logit_ref[p] = logits.reshape(pb, ts, logits.shape[-1])
        bits = lax.bitcast_convert_type(hi_f, jnp.uint32)
        packed = bits[:, :d // 2] | (bits[:, d // 2:] >> 16)
        x1p_ref[p] = packed.reshape(pb, ts, d // 2)


def _const_spec(shape):
    return pl.BlockSpec(shape, lambda j: (0,) * len(shape), pipeline_mode=pl.Buffered(1))


def _regroup_block_diag(w, width):
    heads, hd, _ = w.shape
    per = width // hd
    w = w.reshape(heads // per, per, hd, hd)
    eye = jnp.eye(per, dtype=w.dtype)
    return jnp.einsum("gpij,pq->gpiqj", w, eye).reshape(heads // per, width, width)


def _mixer(x, w_in, conv_w, conv_b, wa, ba, wx, bx, lam, pool_w, pool_scale, w_out, g1, b1, router_w, ts=64):
    nb, s, d = x.shape
    c = conv_w.shape[-1]
    n_exp = router_w.shape[-1]
    rw_hi = router_w.astype(BF16)
    rw_lo = (router_w - rw_hi.astype(F32)).astype(BF16)
    row = lambda p: p.reshape(1, -1)
    operands = (
        x, w_in.astype(BF16), conv_w, row(conv_b),
        _regroup_block_diag(wa, MXU_DIM).astype(BF16), row(ba),
        _regroup_block_diag(wx, MXU_DIM).astype(BF16), row(bx), row(lam),
        _regroup_block_diag(pool_w, MXU_DIM).astype(BF16), row(pool_scale),
        w_out.astype(BF16), row(g1), row(b1), rw_hi, rw_lo)
    tile = lambda width: pl.BlockSpec((nb, ts, width), lambda j: (0, j, 0))
    in_specs = [tile(d)] + [_const_spec(op.shape) for op in operands[1:]]
    return pl.pallas_call(
        _mixer_kernel,
        grid=(s // ts,),
        in_specs=in_specs,
        out_specs=[tile(d), tile(d // 2), tile(n_exp)],
        out_shape=[jax.ShapeDtypeStruct((nb, s, d), F32),
                   jax.ShapeDtypeStruct((nb, s, d // 2), jnp.uint32),
                   jax.ShapeDtypeStruct((nb, s, n_exp), F32)],
        scratch_shapes=[pltpu.VMEM((nb, CONV_HIST, c), F32),
                        pltpu.VMEM((nb, POOL_HIST, c), F32),
                        pltpu.VMEM((nb, c), F32)],
        compiler_params=pltpu.CompilerParams(dimension_semantics=("arbitrary",),
                                             vmem_limit_bytes=VMEM_LIMIT),
        name="mixer_ln_router",
    )(*operands)


def _first_row_of(cond, rows_rev, n):
    return n - jnp.max(jnp.where(cond, rows_rev, 0.0), axis=0, keepdims=True)


def _route_kernel(logit_ref, bias_ref, idx_ref, gate_ref, rank_ref, count_ref, count_scr):
    i = pl.program_id(0)

    @pl.when(i == 0)
    def _():
        count_scr[...] = jnp.zeros_like(count_scr)

    scores = jax.nn.sigmoid(logit_ref[...].T)
    n_exp, tn = scores.shape
    gsz = n_exp // N_EXPERT_GROUPS
    neg = -jnp.inf
    biased = scores + bias_ref[...]
    row = lax.broadcasted_iota(jnp.int32, (n_exp, tn), 0).astype(F32)
    row_rev = n_exp - row

    g_row = lax.broadcasted_iota(jnp.int32, (N_EXPERT_GROUPS, tn), 0)
    group_score = jnp.zeros((N_EXPERT_GROUPS, tn), F32)
    grp_row = lax.broadcasted_iota(jnp.int32, (gsz, tn), 0).astype(F32)
    grp_rev = gsz - grp_row
    for g in range(N_EXPERT_GROUPS):
        blk = biased[g * gsz:(g + 1) * gsz]
        m1 = jnp.max(blk, axis=0, keepdims=True)
        i1 = _first_row_of(blk == m1, grp_rev, gsz)
        m2 = jnp.max(jnp.where(grp_row == i1, neg, blk), axis=0, keepdims=True)
        group_score = jnp.where(g_row == g, m1 + m2, group_score)

    beaten_by = jnp.zeros((N_EXPERT_GROUPS, tn), F32)
    for g in range(N_EXPERT_GROUPS):
        other = jnp.max(jnp.where(g_row == g, group_score, neg), axis=0, keepdims=True)
        wins = (other > group_score) | ((other == group_score) & (g < g_row))
        beaten_by = beaten_by + jnp.where(wins, 1.0, 0.0)
    masked = []
    for g in range(N_EXPERT_GROUPS):
        beaten_g = jnp.max(jnp.where(g_row == g, beaten_by, 0.0), axis=0, keepdims=True)
        masked.append(jnp.where(beaten_g < TOPK_GROUPS, biased[g * gsz:(g + 1) * gsz], neg))
    masked = jnp.concatenate(masked, axis=0)

    k_row = lax.broadcasted_iota(jnp.int32, (TOP_K, tn), 0)
    selected = jnp.zeros((n_exp, tn), F32)
    idx = jnp.zeros((TOP_K, tn), F32)
    gate = jnp.zeros((TOP_K, tn), F32)
    hits = []
    for k in range(TOP_K):
        m = jnp.max(masked, axis=0, keepdims=True)
        ik = _first_row_of(masked == m, row_rev, n_exp)
        hit = row == ik
        gk = jnp.sum(jnp.where(hit, scores, 0.0), axis=0, keepdims=True)
        masked = jnp.where(hit, neg, masked)
        selected = jnp.where(hit, 1.0, selected)
        idx = jnp.where(k_row == k, ik, idx)
        gate = jnp.where(k_row == k, gk, gate)
        hits.append(hit)
    gate = gate / jnp.sum(gate, axis=0, keepdims=True) * ROUTED_SCALE

    earlier = (lax.broadcasted_iota(jnp.int32, (tn, tn), 0) < lax.broadcasted_iota(jnp.int32, (tn, tn), 1))
    before = _dot(selected.astype(BF16), jnp.where(earlier, 1.0, 0.0).astype(BF16)) + count_scr[...]
    rank = jnp.zeros((TOP_K, tn), F32)
    for k in range(TOP_K):
        rk = jnp.sum(jnp.where(hits[k], before, 0.0), axis=0, keepdims=True)
        rank = jnp.where(k_row == k, rk, rank)
    count_scr[...] = count_scr[...] + jnp.sum(selected, axis=1, keepdims=True)

    idx_ref[...] = idx.astype(jnp.int32)
    gate_ref[...] = gate
    rank_ref[...] = rank.astype(jnp.int32)
    count_ref[...] = count_scr[...].astype(jnp.int32)


def _route(logits, bias, tn=512):
    t, n_exp = logits.shape
    return pl.pallas_call(
        _route_kernel,
        grid=(t // tn,),
        in_specs=[pl.BlockSpec((tn, n_exp), lambda i: (i, 0)),
                  pl.BlockSpec((n_exp, 1), lambda i: (0, 0))],
        out_specs=[pl.BlockSpec((TOP_K, tn), lambda i: (0, i)),
                   pl.BlockSpec((TOP_K, tn), lambda i: (0, i)),
                   pl.BlockSpec((TOP_K, tn), lambda i: (0, i)),
                   pl.BlockSpec((n_exp, 1), lambda i: (0, 0))],
        out_shape=[jax.ShapeDtypeStruct((TOP_K, t), jnp.int32),
                   jax.ShapeDtypeStruct((TOP_K, t), F32),
                   jax.ShapeDtypeStruct((TOP_K, t), jnp.int32),
                   jax.ShapeDtypeStruct((n_exp, 1), jnp.int32)],
        scratch_shapes=[pltpu.VMEM((n_exp, 1), F32)],
        compiler_params=pltpu.CompilerParams(dimension_semantics=("arbitrary",),
                                             vmem_limit_bytes=VMEM_LIMIT),
        name="route_topk_rank",
    )(logits, bias.reshape(n_exp, 1))


def _slot_kernel(idx_ref, rank_ref, gate_ref, start_ref, slot_ref, gate_word_ref):
    idx = idx_ref[...]
    n_exp = start_ref.shape[0]
    tn = idx.shape[1]
    row = lax.broadcasted_iota(jnp.int32, (n_exp, tn), 0)
    start = start_ref[...]
    first = [jnp.sum(jnp.where(row == idx[k:k + 1], start, 0), axis=0, keepdims=True) for k in range(TOP_K)]
    slots = jnp.concatenate(first, axis=0) + rank_ref[...]
    bits = lax.bitcast_convert_type(gate_ref[...].astype(BF16).astype(F32), jnp.uint32)
    words = lax.bitcast_convert_type(bits | (bits >> 16), jnp.int32)
    chunk = slot_ref.shape[-1]
    for c in range(slot_ref.shape[0]):
        slot_ref[c] = slots[:, c * chunk:(c + 1) * chunk]
        gate_word_ref[c] = words[:, c * chunk:(c + 1) * chunk]


def _slots(idx, rank, gate, expert_start, tn=512):
    k, t = idx.shape
    n_exp = expert_start.shape[0]
    per_step = tn // SC_CHUNK
    tile = pl.BlockSpec((k, tn), lambda i: (0, i))
    chunked = pl.BlockSpec((per_step, k, SC_CHUNK), lambda i: (i, 0, 0))
    chunked_shape = jax.ShapeDtypeStruct((t // SC_CHUNK, k, SC_CHUNK), jnp.int32)
    return pl.pallas_call(
        _slot_kernel,
        grid=(t // tn,),
        in_specs=[tile, tile, tile, pl.BlockSpec((n_exp, 1), lambda i: (0, 0))],
        out_specs=[chunked, chunked],
        out_shape=[chunked_shape, chunked_shape],
        compiler_params=pltpu.CompilerParams(dimension_semantics=("arbitrary",)),
        name="dispatch_slots",
    )(idx, rank, gate, expert_start.reshape(n_exp, 1))


def _unpack_bf16_pairs(p):
    hi = lax.bitcast_convert_type(p & jnp.uint32(0xFFFF0000), F32)
    lo = lax.bitcast_convert_type(p << 16, F32)
    return jnp.concatenate([hi, lo], axis=1)


def _pack_bf16_pairs(y):
    n = y.shape[1] // 2
    bits = lax.bitcast_convert_type(y.astype(BF16).astype(F32), jnp.uint32)
    return bits[:, :n] | (bits[:, n:] >> 16)


def _sc_worker_layout(n_chunks_total):
    info = plsc.get_sparse_core_info()
    n_workers = info.num_cores * info.num_subcores
    return info.num_cores, n_chunks_total // n_workers


def _sc_dispatch(slots, x1p, n_rows):
    n_chunks_total, n_k, chunk = slots.shape
    t, w = x1p.shape
    n_cores, n_chunks = _sc_worker_layout(n_chunks_total)
    mesh = plsc.VectorSubcoreMesh(core_axis_name="c", subcore_axis_name="s")

    @functools.partial(
        pl.kernel, mesh=mesh, name="sc_dispatch_rows",
        out_type=jax.ShapeDtypeStruct((n_rows, w), x1p.dtype),
        scratch_types=[pltpu.VMEM((n_k, chunk), jnp.int32), pltpu.VMEM((chunk, w), x1p.dtype),
                       pltpu.SemaphoreType.DMA])
    def dispatch(slots_hbm, x_hbm, xg_hbm, idx_v, rows_v, sem):
        wid = lax.axis_index("s") * n_cores + lax.axis_index("c")

        @pl.loop(0, n_chunks)
        def _(ci):
            chunk_id = wid * n_chunks + ci
            pltpu.sync_copy(slots_hbm.at[chunk_id], idx_v)
            pltpu.sync_copy(x_hbm.at[pl.ds(chunk_id * chunk, chunk)], rows_v)
            copies = [pltpu.make_async_copy(rows_v, xg_hbm.at[idx_v.at[k]], sem) for k in range(n_k)]
            for cp in copies:
                cp.start()
            for cp in copies:
                cp.wait()

    return dispatch(slots, x1p)


def _sc_combine(slots, gate_words, yg):
    n_chunks_total, n_k, chunk = slots.shape
    w = yg.shape[1]
    t = n_chunks_total * chunk
    n_cores, n_chunks = _sc_worker_layout(n_chunks_total)
    lanes = plsc.get_sparse_core_info().num_lanes
    n_sub = chunk // SC_SUB
    mesh = plsc.VectorSubcoreMesh(core_axis_name="c", subcore_axis_name="s")
    rows_t = pltpu.VMEM((n_k, SC_SUB, w), jnp.int32)
    out_t = pltpu.VMEM((SC_SUB, w), jnp.int32)
    dma = pltpu.SemaphoreType.DMA

    @functools.partial(
        pl.kernel, mesh=mesh, name="sc_combine_rows",
        compiler_params=pltpu.CompilerParams(needs_layout_passes=False),
        out_type=jax.ShapeDtypeStruct((t, w), jnp.int32),
        scratch_types=[pltpu.VMEM((n_k, chunk), jnp.int32), pltpu.VMEM((n_k, chunk), jnp.int32),
                       rows_t, rows_t, out_t, out_t, dma, dma, dma, dma])
    def combine(slots_hbm, gate_hbm, yg_hbm, out_hbm, idx_v, gate_v, rows0, rows1, out0, out1,
                row_sem0, row_sem1, out_sem0, out_sem1):
        wid = lax.axis_index("s") * n_cores + lax.axis_index("c")
        rows, outs = (rows0, rows1), (out0, out1)
        row_sems, out_sems = (row_sem0, row_sem1), (out_sem0, out_sem1)

        def gathers(sub, b):
            return [pltpu.make_async_copy(yg_hbm.at[idx_v.at[k, pl.ds(sub * SC_SUB, SC_SUB)]], rows[b].at[k],
                                          row_sems[b]) for k in range(n_k)]

        @pl.loop(0, n_chunks)
        def _(ci):
            chunk_id = wid * n_chunks + ci
            tok0 = chunk_id * chunk
            pltpu.sync_copy(slots_hbm.at[chunk_id], idx_v)
            pltpu.sync_copy(gate_hbm.at[chunk_id], gate_v)

            def store(sub, b):
                return pltpu.make_async_copy(outs[b], out_hbm.at[pl.ds(tok0 + sub * SC_SUB, SC_SUB)], out_sems[b])

            for cp in gathers(0, 0):
                cp.start()

            @pl.loop(0, n_sub // 2)
            def _(pair):
                for b in (0, 1):
                    sub = 2 * pair + b

                    @pl.when(sub + 1 < n_sub)
                    def _():
                        for cp in gathers(sub + 1, 1 - b):
                            cp.start()

                    for cp in gathers(sub, b):
                        cp.wait()

                    @pl.when(sub >= 2)
                    def _():
                        store(sub - 2, b).wait()

                    buf, out = rows[b], outs[b]

                    @pl.loop(0, SC_SUB)
                    def _(tt):
                        col = jnp.full((lanes,), sub * SC_SUB + tt, jnp.int32)
                        gates = [plsc.bitcast(plsc.load_gather(gate_v, [jnp.full((lanes,), k, jnp.int32), col]), BF16)
                                 for k in range(n_k)]
                        for v in range(w // lanes):
                            terms = [gates[k] * plsc.bitcast(buf[k, tt, pl.ds(v * lanes, lanes)], BF16)
                                     for k in range(n_k)]
                            while len(terms) > 1:
                                terms = [terms[i] + terms[i + 1] for i in range(0, len(terms), 2)]
                            out[tt, pl.ds(v * lanes, lanes)] = plsc.bitcast(terms[0], jnp.int32)

                    store(sub, b).start()

            store(n_sub - 2, 0).wait()
            store(n_sub - 1, 1).wait()

    return combine(slots, gate_words, lax.bitcast_convert_type(yg, jnp.int32))


def _expert_kernel(first_block, n_block, n_used, xg_hbm, w1_ref, w3_ref, w2_ref, yg_hbm,
                   x_buf, y_buf, w1_bf, w3_bf, w2_bf, in_sem, out_sem):
    e = pl.program_id(0)
    n_slot, bm, _ = x_buf.shape
    ahead = n_slot - BLOCK_GROUP
    total = n_used[0]

    def fetch(g):
        slot = g % n_slot
        return pltpu.make_async_copy(xg_hbm.at[pl.ds(g * bm, bm)], x_buf.at[slot], in_sem.at[slot])

    def write_back(g):
        slot = g % n_slot
        return pltpu.make_async_copy(y_buf.at[slot], yg_hbm.at[pl.ds(g * bm, bm)], out_sem.at[slot])

    @pl.when(e == 0)
    def _():
        for g in range(ahead):
            @pl.when(g < total)
            def _():
                fetch(g).start()

    @pl.when(n_block[e] > 0)
    def _():
        w1_bf[...] = w1_ref[0].astype(BF16)
        w3_bf[...] = w3_ref[0].astype(BF16)
        w2_bf[...] = w2_ref[0].astype(BF16)

    def swiglu(slots):
        xs = [_unpack_bf16_pairs(x_buf[s]).astype(BF16) for s in slots]
        up = [(_dot(xb, w1_bf[...]), _dot(xb, w3_bf[...])) for xb in xs]
        hs = [(h1 * jax.nn.sigmoid(h1) * h3).astype(BF16) for h1, h3 in up]
        ys = [_dot(h, w2_bf[...]) for h in hs]
        return [_pack_bf16_pairs(y) for y in ys]

    def process(blocks):
        for g in blocks:
            fetch(g).wait()
        for g in blocks:
            @pl.when(g + ahead < total)
            def _():
                fetch(g + ahead).start()
        ys = swiglu([g % n_slot for g in blocks])
        for g in blocks:
            @pl.when(g >= n_slot)
            def _():
                write_back(g - n_slot).wait()
        for g, y in zip(blocks, ys):
            y_buf[g % n_slot] = y
            write_back(g).start()

    done = 0
    size = BLOCK_GROUP
    while size >= 1:
        left = n_block[e] - done
        start = first_block[e] + done
        if size == BLOCK_GROUP:
            @pl.loop(0, left // size)
            def _(i, start=start, size=size):
                process([start + i * size + j for j in range(size)])
        else:
            @pl.when(left >= size)
            def _(start=start, size=size):
                process([start + j for j in range(size)])
        done = done + left // size * size
        size //= 2

    @pl.when(e == pl.num_programs(0) - 1)
    def _():
        for back in range(1, n_slot + 1):
            @pl.when(total >= back)
            def _():
                write_back(total - back).wait()


def _experts(first_block, n_block, n_used, xg, w1, w3, w2, bm):
    n_rows, w = xg.shape
    n_exp, d, de = w1.shape
    weight = lambda e, *_: (e, 0, 0)
    return pl.pallas_call(
        _expert_kernel,
        grid_spec=pltpu.PrefetchScalarGridSpec(
            num_scalar_prefetch=3,
            grid=(n_exp,),
            in_specs=[pl.BlockSpec(memory_space=pl.ANY),
                      pl.BlockSpec((1, d, de), weight),
                      pl.BlockSpec((1, d, de), weight),
                      pl.BlockSpec((1, de, d), weight)],
            out_specs=pl.BlockSpec(memory_space=pl.ANY),
            scratch_shapes=[pltpu.VMEM((ROW_RING, bm, w), xg.dtype), pltpu.VMEM((ROW_RING, bm, w), jnp.uint32),
                            pltpu.VMEM((d, de), BF16), pltpu.VMEM((d, de), BF16), pltpu.VMEM((de, d), BF16),
                            pltpu.SemaphoreType.DMA((ROW_RING,)), pltpu.SemaphoreType.DMA((ROW_RING,))]),
        out_shape=jax.ShapeDtypeStruct((n_rows, w), jnp.uint32),
        compiler_params=pltpu.CompilerParams(dimension_semantics=("arbitrary",),
                                             vmem_limit_bytes=VMEM_LIMIT),
        name="routed_experts",
    )(first_block, n_block, n_used, xg, w1, w3, w2)


def _combine_kernel(y_ref, x1_ref, sw1_ref, sw3_ref, sw2_ref, g2_ref, b2_ref, out_ref):
    x1 = x1_ref[...]
    xb = x1.astype(BF16)
    h1 = _dot(xb, sw1_ref[...])
    h3 = _dot(xb, sw3_ref[...])
    ffn = _dot((h1 * jax.nn.sigmoid(h1) * h3).astype(BF16), sw2_ref[...])
    ffn = ffn + _unpack_bf16_pairs(lax.bitcast_convert_type(y_ref[...], jnp.uint32))
    out_ref[...] = _layer_norm(DEEPNORM_ALPHA * x1 + ffn, g2_ref[...], b2_ref[...])


def _combine(y_routed, x1, sw1, sw3, sw2, g2, b2, tn=512):
    t, d = x1.shape
    w = y_routed.shape[1]
    row = lambda p: p.reshape(1, -1)
    operands = (y_routed, x1, sw1.astype(BF16), sw3.astype(BF16), sw2.astype(BF16), row(g2), row(b2))
    in_specs = [pl.BlockSpec((tn, w), lambda i: (i, 0)),
                pl.BlockSpec((tn, d), lambda i: (i, 0))]
    in_specs += [_const_spec(op.shape) for op in operands[2:]]
    return pl.pallas_call(
        _combine_kernel,
        grid=(t // tn,),
        in_specs=in_specs,
        out_specs=pl.BlockSpec((tn, d), lambda i: (i, 0)),
        out_shape=jax.ShapeDtypeStruct((t, d), F32),
        compiler_params=pltpu.CompilerParams(dimension_semantics=("arbitrary",),
                                             vmem_limit_bytes=VMEM_LIMIT),
        name="combine_shared_ln",
    )(*operands)


def _block_table(counts, bm):
    n_block = (counts + bm - 1) // bm
    ends = jnp.cumsum(n_block)
    first_block = ends - n_block
    i32 = lambda v: v.astype(jnp.int32)
    return i32(first_block * bm), i32(first_block), i32(n_block), i32(ends[-1:])


def _moe(x1, x1p, logits, router_bias, w1, w3, w2, sw1, sw3, sw2, g2, b2, bm=256):
    t, d = x1.shape
    n_exp = logits.shape[1]
    idx, gate, rank, counts = _route(logits, router_bias)
    n_blocks = (t * TOP_K) // bm + n_exp
    starts, first_block, n_block, n_used = _block_table(counts[:, 0], bm)
    slots, gate_words = _slots(idx, rank, gate, starts)
    xg = _sc_dispatch(slots, x1p, n_blocks * bm)
    yg = _experts(first_block, n_block, n_used, xg, w1, w3, w2, bm)
    y_routed = _sc_combine(slots, gate_words, yg)
    return _combine(y_routed, x1, sw1, sw3, sw2, g2, b2)


def kernel(x, w_in, conv_w, conv_b, lru_wa, lru_ba, lru_wx, lru_bx, lru_lambda, pool_w, pool_scale, w_out, ln1_g, ln1_b, router_w, router_bias, exp_w1, exp_w3, exp_w2, sh_w1, sh_w3, sh_w2, ln2_g, ln2_b):
    nb, s, d = x.shape
    for l in range(DEPTH):
        x1, x1p, logits = _mixer(x, w_in[l], conv_w[l], conv_b[l], lru_wa[l], lru_ba[l], lru_wx[l], lru_bx[l],
                                 lru_lambda[l], pool_w[l], pool_scale[l], w_out[l], ln1_g[l], ln1_b[l],
                                 router_w[l])
        t = nb * s
        x = _moe(x1.reshape(t, d), x1p.reshape(t, d // 2), logits.reshape(t, -1), router_bias[l],
                 exp_w1[l], exp_w3[l], exp_w2[l], sh_w1[l], sh_w3[l], sh_w2[l], ln2_g[l], ln2_b[l])
        x = x.reshape(nb, s, d)
    return x
```

```python
import functools
import math

import jax
import jax.numpy as jnp
from jax import lax
from jax.experimental import pallas as pl
from jax.experimental.pallas import tpu as pltpu
from jax.experimental.pallas import tpu_sc as plsc

LRU_HEADS = 8
CONV_WIDTH = 4
LRU_C = 8.0
POOL_WINDOWS = (2, 4, 8, 16)
N_EXPERT_GROUPS = 8
TOPK_GROUPS = 4
TOP_K = 8
ROUTED_SCALE = 2.5
LN_EPS = 1e-5
DEPTH = 1
DEEPNORM_ALPHA = (2.0 * DEPTH) ** 0.25

MXU_DIM = 256
POOL_GROUP_DIM = 128
CONV_HIST = 8
POOL_HIST = 16
SC_CHUNK = 128
SC_SUB = 8
MIXER_PARTS = 2
BLOCK_GROUP = 4
ROW_RING = 8
VMEM_LIMIT = 56 * 1024 * 1024

F32 = jnp.float32
BF16 = jnp.bfloat16


def _dot(a, b):
    return jnp.dot(a, b, preferred_element_type=F32)


def _layer_norm(z, g, b):
    mu = jnp.mean(z, axis=-1, keepdims=True)
    zc = z - mu
    var = jnp.mean(zc * zc, axis=-1, keepdims=True)
    return zc * lax.rsqrt(var + LN_EPS) * g + b


def _block_diag_dot(xb, w_ref):
    n = w_ref.shape[0]
    return jnp.concatenate(
        [_dot(xb[:, i * MXU_DIM:(i + 1) * MXU_DIM], w_ref[i]) for i in range(n)], axis=1)


def _causal_conv(u, hist_ref, part, conv_w, conv_b):
    pb, ts, c = u.shape
    ext = jnp.concatenate([hist_ref[part], u], axis=1)
    hist_ref[part] = u[:, ts - CONV_HIST:, :]
    uc = jnp.broadcast_to(conv_b.reshape(1, 1, c), (pb, ts, c))
    for k in range(CONV_WIDTH):
        off = CONV_HIST - (CONV_WIDTH - 1) + k
        uc = uc + conv_w[k:k + 1, :].reshape(1, 1, c) * ext[:, off:off + ts, :]
    return uc


def _lru_scan(uc, ga, gx, lam, carry_ref, part, t0):
    pb, ts, c = uc.shape
    r = jax.nn.sigmoid(ga)
    i_gate = jax.nn.sigmoid(gx)
    softplus_neg_lam = jnp.maximum(-lam, 0.0) + jnp.log1p(jnp.exp(-jnp.abs(lam)))
    log_a = (-LRU_C) * r * softplus_neg_lam
    a = jnp.exp(log_a)
    mult = jnp.sqrt(jnp.tanh(-log_a) * (1.0 + a * a))
    t_local = lax.broadcasted_iota(jnp.int32, (pb, ts, c), 1)
    first = (t_local + t0) == 0
    a3 = a.reshape(pb, ts, c)
    x3 = jnp.where(first, 1.0, mult.reshape(pb, ts, c)) * i_gate.reshape(pb, ts, c) * uc

    shift = 1
    while shift < ts:
        valid = t_local >= shift
        a_prev = pltpu.roll(a3, shift, axis=1)
        x_prev = pltpu.roll(x3, shift, axis=1)
        x3 = jnp.where(valid, a3 * x_prev, 0.0) + x3
        a3 = jnp.where(valid, a3 * a_prev, a3)
        shift *= 2
    h = x3 + a3 * carry_ref[part][:, None, :]
    carry_ref[part] = h[:, ts - 1, :]
    return h


def _multiscale_pool(v, hist_ref, part, t0):
    pb, ts, c = v.shape
    ext = jnp.concatenate([hist_ref[part], v], axis=1)
    hist_ref[part] = v[:, ts - POOL_HIST:, :]
    t_glob = lax.broadcasted_iota(jnp.int32, (1, ts, POOL_GROUP_DIM), 1) + t0
    pooled = []
    for g, win in enumerate(POOL_WINDOWS):
        lanes = slice(g * POOL_GROUP_DIM, (g + 1) * POOL_GROUP_DIM)
        s = ext[:, :, lanes]
        shift = 1
        while shift < win:
            s = s + pltpu.roll(s, shift, axis=1)
            shift *= 2
        count = jnp.minimum(t_glob + 1, win).astype(F32)
        pooled.append(s[:, POOL_HIST:, :] / count - v[:, :, lanes])
    return jnp.concatenate(pooled, axis=-1)


def _mixer_kernel(x_ref, w_in_ref, conv_w_ref, conv_b_ref, wa_ref, ba_ref, wx_ref, bx_ref, lam_ref,
                  pool_w_ref, pool_scale_ref, w_out_ref, g1_ref, b1_ref, rw_hi_ref, rw_lo_ref,
                  x1_ref, x1p_ref, logit_ref, hist_u, hist_v, carry):
    j = pl.program_id(0)
    nb, ts, d = x_ref.shape
    c = hist_u.shape[-1]
    pb = nb // MIXER_PARTS
    rows = pb * ts
    t0 = j * ts
    parts = [slice(p * pb, (p + 1) * pb) for p in range(MIXER_PARTS)]

    @pl.when(j == 0)
    def _():
        hist_u[...] = jnp.zeros_like(hist_u)
        hist_v[...] = jnp.zeros_like(hist_v)
        carry[...] = jnp.zeros_like(carry)

    xs = [x_ref[p].reshape(rows, d) for p in parts]
    projs = [_dot(x.astype(BF16), w_in_ref[...]) for x in xs]
    ucs = [_causal_conv(proj[:, :c].reshape(pb, ts, c), hist_u, p, conv_w_ref[...], conv_b_ref[...])
           for proj, p in zip(projs, parts)]
    ucbs = [uc.reshape(rows, c).astype(BF16) for uc in ucs]
    gas = [_block_diag_dot(ucb, wa_ref) + ba_ref[...] for ucb in ucbs]
    gxs = [_block_diag_dot(ucb, wx_ref) + bx_ref[...] for ucb in ucbs]
    hs = [_lru_scan(uc, ga, gx, lam_ref[...], carry, p, t0) for uc, ga, gx, p in zip(ucs, gas, gxs, parts)]
    y_lrus = []
    for h, proj in zip(hs, projs):
        gate = proj[:, c:2 * c]
        gelu_gate = 0.5 * gate * (1.0 + jnp.tanh(math.sqrt(2.0 / math.pi) * (gate + 0.044715 * gate * gate * gate)))
        y_lrus.append(h.reshape(rows, c) * gelu_gate)
    pooled = [_multiscale_pool(proj[:, 2 * c:].reshape(pb, ts, c), hist_v, p, t0).reshape(rows, c).astype(BF16)
              for proj, p in zip(projs, parts)]
    y_pools = [_block_diag_dot(pl_, pool_w_ref) * pool_scale_ref[...] for pl_ in pooled]
    mixes = [_dot(jnp.concatenate([y_lru, y_pool], axis=1).astype(BF16), w_out_ref[...])
             for y_lru, y_pool in zip(y_lrus, y_pools)]
    x1s = [_layer_norm(DEEPNORM_ALPHA * x + mix, g1_ref[...], b1_ref[...]) for x, mix in zip(xs, mixes)]

    for p, x1 in zip(parts, x1s):
        x1_ref[p] = x1.reshape(pb, ts, d)
        hi = x1.astype(BF16)
        hi_f = hi.astype(F32)
        lo = (x1 - hi_f).astype(BF16)
        logits = _dot(hi, rw_hi_ref[...]) + _dot(lo, rw_hi_ref[...]) + _dot(hi, rw_lo_ref[...])
        logit_ref[p] = logits.reshape(pb, ts, logits.shape[-1])
        bits = lax.bitcast_convert_type(hi_f, jnp.uint32)
        packed = bits[:, :d // 2] | (bits[:, d // 2:] >> 16)
        x1p_ref[p] = packed.reshape(pb, ts, d // 2)


def _const_spec(shape):
    return pl.BlockSpec(shape, lambda j: (0,) * len(shape), pipeline_mode=pl.Buffered(1))


def _regroup_block_diag(w, width):
    heads, hd, _ = w.shape
    per = width // hd
    w = w.reshape(heads // per, per, hd, hd)
    eye = jnp.eye(per, dtype=w.dtype)
    return jnp.einsum("gpij,pq->gpiqj", w, eye).reshape(heads // per, width, width)


def _mixer(x, w_in, conv_w, conv_b, wa, ba, wx, bx, lam, pool_w, pool_scale, w_out, g1, b1, router_w, ts=64):
    nb, s, d = x.shape
    c = conv_w.shape[-1]
    n_exp = router_w.shape[-1]
    rw_hi = router_w.astype(BF16)
    rw_lo = (router_w - rw_hi.astype(F32)).astype(BF16)
    row = lambda p: p.reshape(1, -1)
    operands = (
        x, w_in.astype(BF16), conv_w, row(conv_b),
        _regroup_block_diag(wa, MXU_DIM).astype(BF16), row(ba),
        _regroup_block_diag(wx, MXU_DIM).astype(BF16), row(bx), row(lam),
        _regroup_block_diag(pool_w, MXU_DIM).astype(BF16), row(pool_scale),
        w_out.astype(BF16), row(g1), row(b1), rw_hi, rw_lo)
    tile = lambda width: pl.BlockSpec((nb, ts, width), lambda j: (0, j, 0))
    in_specs = [tile(d)] + [_const_spec(op.shape) for op in operands[1:]]
    return pl.pallas_call(
        _mixer_kernel,
        grid=(s // ts,),
        in_specs=in_specs,
        out_specs=[tile(d), tile(d // 2), tile(n_exp)],
        out_shape=[jax.ShapeDtypeStruct((nb, s, d), F32),
                   jax.ShapeDtypeStruct((nb, s, d // 2), jnp.uint32),
                   jax.ShapeDtypeStruct((nb, s, n_exp), F32)],
        scratch_shapes=[pltpu.VMEM((nb, CONV_HIST, c), F32),
                        pltpu.VMEM((nb, POOL_HIST, c), F32),
                        pltpu.VMEM((nb, c), F32)],
        compiler_params=pltpu.CompilerParams(dimension_semantics=("arbitrary",),
                                             vmem_limit_bytes=VMEM_LIMIT),
        name="mixer_ln_router",
    )(*operands)


def _first_row_of(cond, rows_rev, n):
    return n - jnp.max(jnp.where(cond, rows_rev, 0.0), axis=0, keepdims=True)


def _route_kernel(logit_ref, bias_ref, idx_ref, gate_ref, rank_ref, count_ref, count_scr):
    i = pl.program_id(0)

    @pl.when(i == 0)
    def _():
        count_scr[...] = jnp.zeros_like(count_scr)

    scores = jax.nn.sigmoid(logit_ref[...].T)
    n_exp, tn = scores.shape
    gsz = n_exp // N_EXPERT_GROUPS
    neg = -jnp.inf
    biased = scores + bias_ref[...]
    row = lax.broadcasted_iota(jnp.int32, (n_exp, tn), 0).astype(F32)
    row_rev = n_exp - row

    g_row = lax.broadcasted_iota(jnp.int32, (N_EXPERT_GROUPS, tn), 0)
    group_score = jnp.zeros((N_EXPERT_GROUPS, tn), F32)
    grp_row = lax.broadcasted_iota(jnp.int32, (gsz, tn), 0).astype(F32)
    grp_rev = gsz - grp_row
    for g in range(N_EXPERT_GROUPS):
        blk = biased[g * gsz:(g + 1) * gsz]
        m1 = jnp.max(blk, axis=0, keepdims=True)
        i1 = _first_row_of(blk == m1, grp_rev, gsz)
        m2 = jnp.max(jnp.where(grp_row == i1, neg, blk), axis=0, keepdims=True)
        group_score = jnp.where(g_row == g, m1 + m2, group_score)

    beaten_by = jnp.zeros((N_EXPERT_GROUPS, tn), F32)
    for g in range(N_EXPERT_GROUPS):
        other = jnp.max(jnp.where(g_row == g, group_score, neg), axis=0, keepdims=True)
        wins = (other > group_score) | ((other == group_score) & (g < g_row))
        beaten_by = beaten_by + jnp.where(wins, 1.0, 0.0)
    masked = []
    for g in range(N_EXPERT_GROUPS):
        beaten_g = jnp.max(jnp.where(g_row == g, beaten_by, 0.0), axis=0, keepdims=True)
        masked.append(jnp.where(beaten_g < TOPK_GROUPS, biased[g * gsz:(g + 1) * gsz], neg))
    masked = jnp.concatenate(masked, axis=0)

    k_row = lax.broadcasted_iota(jnp.int32, (TOP_K, tn), 0)
    selected = jnp.zeros((n_exp, tn), F32)
    idx = jnp.zeros((TOP_K, tn), F32)
    gate = jnp.zeros((TOP_K, tn), F32)
    hits = []
    for k in range(TOP_K):
        m = jnp.max(masked, axis=0, keepdims=True)
        ik = _first_row_of(masked == m, row_rev, n_exp)
        hit = row == ik
        gk = jnp.sum(jnp.where(hit, scores, 0.0), axis=0, keepdims=True)
        masked = jnp.where(hit, neg, masked)
        selected = jnp.where(hit, 1.0, selected)
        idx = jnp.where(k_row == k, ik, idx)
        gate = jnp.where(k_row == k, gk, gate)
        hits.append(hit)
    gate = gate / jnp.sum(gate, axis=0, keepdims=True) * ROUTED_SCALE

    earlier = (lax.broadcasted_iota(jnp.int32, (tn, tn), 0) < lax.broadcasted_iota(jnp.int32, (tn, tn), 1))
    before = _dot(selected.astype(BF16), jnp.where(earlier, 1.0, 0.0).astype(BF16)) + count_scr[...]
    rank = jnp.zeros((TOP_K, tn), F32)
    for k in range(TOP_K):
        rk = jnp.sum(jnp.where(hits[k], before, 0.0), axis=0, keepdims=True)
        rank = jnp.where(k_row == k, rk, rank)
    count_scr[...] = count_scr[...] + jnp.sum(selected, axis=1, keepdims=True)

    idx_ref[...] = idx.astype(jnp.int32)
    gate_ref[...] = gate
    rank_ref[...] = rank.astype(jnp.int32)
    count_ref[...] = count_scr[...].astype(jnp.int32)


def _route(logits, bias, tn=512):
    t, n_exp = logits.shape
    return pl.pallas_call(
        _route_kernel,
        grid=(t // tn,),
        in_specs=[pl.BlockSpec((tn, n_exp), lambda i: (i, 0)),
                  pl.BlockSpec((n_exp, 1), lambda i: (0, 0))],
        out_specs=[pl.BlockSpec((TOP_K, tn), lambda i: (0, i)),
                   pl.BlockSpec((TOP_K, tn), lambda i: (0, i)),
                   pl.BlockSpec((TOP_K, tn), lambda i: (0, i)),
                   pl.BlockSpec((n_exp, 1), lambda i: (0, 0))],
        out_shape=[jax.ShapeDtypeStruct((TOP_K, t), jnp.int32),
                   jax.ShapeDtypeStruct((TOP_K, t), F32),
                   jax.ShapeDtypeStruct((TOP_K, t), jnp.int32),
                   jax.ShapeDtypeStruct((n_exp, 1), jnp.int32)],
        scratch_shapes=[pltpu.VMEM((n_exp, 1), F32)],
        compiler_params=pltpu.CompilerParams(dimension_semantics=("arbitrary",),
                                             vmem_limit_bytes=VMEM_LIMIT),
        name="route_topk_rank",
    )(logits, bias.reshape(n_exp, 1))


def _slot_kernel(idx_ref, rank_ref, gate_ref, start_ref, slot_ref, gate_word_ref):
    idx = idx_ref[...]
    n_exp = start_ref.shape[0]
    tn = idx.shape[1]
    row = lax.broadcasted_iota(jnp.int32, (n_exp, tn), 0)
    start = start_ref[...]
    first = [jnp.sum(jnp.where(row == idx[k:k + 1], start, 0), axis=0, keepdims=True) for k in range(TOP_K)]
    slots = jnp.concatenate(first, axis=0) + rank_ref[...]
    bits = lax.bitcast_convert_type(gate_ref[...].astype(BF16).astype(F32), jnp.uint32)
    words = lax.bitcast_convert_type(bits | (bits >> 16), jnp.int32)
    chunk = slot_ref.shape[-1]
    for c in range(slot_ref.shape[0]):
        slot_ref[c] = slots[:, c * chunk:(c + 1) * chunk]
        gate_word_ref[c] = words[:, c * chunk:(c + 1) * chunk]


def _slots(idx, rank, gate, expert_start, tn=512):
    k, t = idx.shape
    n_exp = expert_start.shape[0]
    per_step = tn // SC_CHUNK
    tile = pl.BlockSpec((k, tn), lambda i: (0, i))
    chunked = pl.BlockSpec((per_step, k, SC_CHUNK), lambda i: (i, 0, 0))
    chunked_shape = jax.ShapeDtypeStruct((t // SC_CHUNK, k, SC_CHUNK), jnp.int32)
    return pl.pallas_call(
        _slot_kernel,
        grid=(t // tn,),
        in_specs=[tile, tile, tile, pl.BlockSpec((n_exp, 1), lambda i: (0, 0))],
        out_specs=[chunked, chunked],
        out_shape=[chunked_shape, chunked_shape],
        compiler_params=pltpu.CompilerParams(dimension_semantics=("arbitrary",)),
        name="dispatch_slots",
    )(idx, rank, gate, expert_start.reshape(n_exp, 1))


def _unpack_bf16_pairs(p):
    hi = lax.bitcast_convert_type(p & jnp.uint32(0xFFFF0000), F32)
    lo = lax.bitcast_convert_type(p << 16, F32)
    return jnp.concatenate([hi, lo], axis=1)


def _pack_bf16_pairs(y):
    n = y.shape[1] // 2
    bits = lax.bitcast_convert_type(y.astype(BF16).astype(F32), jnp.uint32)
    return bits[:, :n] | (bits[:, n:] >> 16)


def _sc_worker_layout(n_chunks_total):
    info = plsc.get_sparse_core_info()
    n_workers = info.num_cores * info.num_subcores
    return info.num_cores, n_chunks_total // n_workers


def _sc_dispatch(slots, x1p, n_rows):
    n_chunks_total, n_k, chunk = slots.shape
    t, w = x1p.shape
    n_cores, n_chunks = _sc_worker_layout(n_chunks_total)
    mesh = plsc.VectorSubcoreMesh(core_axis_name="c", subcore_axis_name="s")

    @functools.partial(
        pl.kernel, mesh=mesh, name="sc_dispatch_rows",
        out_type=jax.ShapeDtypeStruct((n_rows, w), x1p.dtype),
        scratch_types=[pltpu.VMEM((n_k, chunk), jnp.int32), pltpu.VMEM((chunk, w), x1p.dtype),
                       pltpu.SemaphoreType.DMA])
    def dispatch(slots_hbm, x_hbm, xg_hbm, idx_v, rows_v, sem):
        wid = lax.axis_index("s") * n_cores + lax.axis_index("c")

        @pl.loop(0, n_chunks)
        def _(ci):
            chunk_id = wid * n_chunks + ci
            pltpu.sync_copy(slots_hbm.at[chunk_id], idx_v)
            pltpu.sync_copy(x_hbm.at[pl.ds(chunk_id * chunk, chunk)], rows_v)
            copies = [pltpu.make_async_copy(rows_v, xg_hbm.at[idx_v.at[k]], sem) for k in range(n_k)]
            for cp in copies:
                cp.start()
            for cp in copies:
                cp.wait()

    return dispatch(slots, x1p)


def _sc_combine(slots, gate_words, yg):
    n_chunks_total, n_k, chunk = slots.shape
    w = yg.shape[1]
    t = n_chunks_total * chunk
    n_cores, n_chunks = _sc_worker_layout(n_chunks_total)
    lanes = plsc.get_sparse_core_info().num_lanes
    n_sub = chunk // SC_SUB
    mesh = plsc.VectorSubcoreMesh(core_axis_name="c", subcore_axis_name="s")
    rows_t = pltpu.VMEM((n_k, SC_SUB, w), jnp.int32)
    out_t = pltpu.VMEM((SC_SUB, w), jnp.int32)
    dma = pltpu.SemaphoreType.DMA

    @functools.partial(
        pl.kernel, mesh=mesh, name="sc_combine_rows",
        compiler_params=pltpu.CompilerParams(needs_layout_passes=False),
        out_type=jax.ShapeDtypeStruct((t, w), jnp.int32),
        scratch_types=[pltpu.VMEM((n_k, chunk), jnp.int32), pltpu.VMEM((n_k, chunk), jnp.int32),
                       rows_t, rows_t, out_t, out_t, dma, dma, dma, dma])
    def combine(slots_hbm, gate_hbm, yg_hbm, out_hbm, idx_v, gate_v, rows0, rows1, out0, out1,
                row_sem0, row_sem1, out_sem0, out_sem1):
        wid = lax.axis_index("s") * n_cores + lax.axis_index("c")
        rows, outs = (rows0, rows1), (out0, out1)
        row_sems, out_sems = (row_sem0, row_sem1), (out_sem0, out_sem1)

        def gathers(sub, b):
            return [pltpu.make_async_copy(yg_hbm.at[idx_v.at[k, pl.ds(sub * SC_SUB, SC_SUB)]], rows[b].at[k],
                                          row_sems[b]) for k in range(n_k)]

        @pl.loop(0, n_chunks)
        def _(ci):
            chunk_id = wid * n_chunks + ci
            tok0 = chunk_id * chunk
            pltpu.sync_copy(slots_hbm.at[chunk_id], idx_v)
            pltpu.sync_copy(gate_hbm.at[chunk_id], gate_v)

            def store(sub, b):
                return pltpu.make_async_copy(outs[b], out_hbm.at[pl.ds(tok0 + sub * SC_SUB, SC_SUB)], out_sems[b])

            for cp in gathers(0, 0):
                cp.start()

            @pl.loop(0, n_sub // 2)
            def _(pair):
                for b in (0, 1):
                    sub = 2 * pair + b

                    @pl.when(sub + 1 < n_sub)
                    def _():
                        for cp in gathers(sub + 1, 1 - b):
                            cp.start()

                    for cp in gathers(sub, b):
                        cp.wait()

                    @pl.when(sub >= 2)
                    def _():
                        store(sub - 2, b).wait()

                    buf, out = rows[b], outs[b]

                    @pl.loop(0, SC_SUB)
                    def _(tt):
                        col = jnp.full((lanes,), sub * SC_SUB + tt, jnp.int32)
                        gates = [plsc.bitcast(plsc.load_gather(gate_v, [jnp.full((lanes,), k, jnp.int32), col]), BF16)
                                 for k in range(n_k)]
                        for v in range(w // lanes):
                            terms = [gates[k] * plsc.bitcast(buf[k, tt, pl.ds(v * lanes, lanes)], BF16)
                                     for k in range(n_k)]
                            while len(terms) > 1:
                                terms = [terms[i] + terms[i + 1] for i in range(0, len(terms), 2)]
                            out[tt, pl.ds(v * lanes, lanes)] = plsc.bitcast(terms[0], jnp.int32)

                    store(sub, b).start()

            store(n_sub - 2, 0).wait()
            store(n_sub - 1, 1).wait()

    return combine(slots, gate_words, yg)


def _expert_kernel(first_block, n_block, n_used, xg_hbm, w1_ref, w3_ref, w2_ref, yg_hbm,
                   x_buf, y_buf, w1_bf, w3_bf, w2_bf, in_sem, out_sem):
    e = pl.program_id(0)
    n_slot, bm, _ = x_buf.shape
    ahead = n_slot - BLOCK_GROUP
    total = n_used[0]

    def fetch(g):
        slot = g % n_slot
        return pltpu.make_async_copy(xg_hbm.at[pl.ds(g * bm, bm)], x_buf.at[slot], in_sem.at[slot])

    def write_back(g):
        slot = g % n_slot
        return pltpu.make_async_copy(y_buf.at[slot], yg_hbm.at[pl.ds(g * bm, bm)], out_sem.at[slot])

    @pl.when(e == 0)
    def _():
        for g in range(ahead):
            @pl.when(g < total)
            def _():
                fetch(g).start()

    @pl.when(n_block[e] > 0)
    def _():
        w1_bf[...] = w1_ref[0].astype(BF16)
        w3_bf[...] = w3_ref[0].astype(BF16)
        w2_bf[...] = w2_ref[0].astype(BF16)

    def swiglu(slots):
        xs = [_unpack_bf16_pairs(x_buf[s]).astype(BF16) for s in slots]
        up = [(_dot(xb, w1_bf[...]), _dot(xb, w3_bf[...])) for xb in xs]
        hs = [(h1 * jax.nn.sigmoid(h1) * h3).astype(BF16) for h1, h3 in up]
        ys = [_dot(h, w2_bf[...]) for h in hs]
        return [lax.bitcast_convert_type(_pack_bf16_pairs(y), y_buf.dtype) for y in ys]

    def process(blocks):
        for g in blocks:
            fetch(g).wait()
        for g in blocks:
            @pl.when(g + ahead < total)
            def _():
                fetch(g + ahead).start()
        ys = swiglu([g % n_slot for g in blocks])
        for g in blocks:
            @pl.when(g >= n_slot)
            def _():
                write_back(g - n_slot).wait()
        for g, y in zip(blocks, ys):
            y_buf[g % n_slot] = y
            write_back(g).start()

    done = 0
    size = BLOCK_GROUP
    while size >= 1:
        left = n_block[e] - done
        start = first_block[e] + done
        if size == BLOCK_GROUP:
            @pl.loop(0, left // size)
            def _(i, start=start, size=size):
                process([start + i * size + j for j in range(size)])
        else:
            @pl.when(left >= size)
            def _(start=start, size=size):
                process([start + j for j in range(size)])
        done = done + left // size * size
        size //= 2

    @pl.when(e == pl.num_programs(0) - 1)
    def _():
        for back in range(1, n_slot + 1):
            @pl.when(total >= back)
            def _():
                write_back(total - back).wait()


def _experts(first_block, n_block, n_used, xg, w1, w3, w2, bm):
    n_rows, w = xg.shape
    n_exp, d, de = w1.shape
    weight = lambda e, *_: (e, 0, 0)
    return pl.pallas_call(
        _expert_kernel,
        grid_spec=pltpu.PrefetchScalarGridSpec(
            num_scalar_prefetch=3,
            grid=(n_exp,),
            in_specs=[pl.BlockSpec(memory_space=pl.ANY),
                      pl.BlockSpec((1, d, de), weight),
                      pl.BlockSpec((1, d, de), weight),
                      pl.BlockSpec((1, de, d), weight)],
            out_specs=pl.BlockSpec(memory_space=pl.ANY),
            scratch_shapes=[pltpu.VMEM((ROW_RING, bm, w), xg.dtype), pltpu.VMEM((ROW_RING, bm, w), jnp.int32),
                            pltpu.VMEM((d, de), BF16), pltpu.VMEM((d, de), BF16), pltpu.VMEM((de, d), BF16),
                            pltpu.SemaphoreType.DMA((ROW_RING,)), pltpu.SemaphoreType.DMA((ROW_RING,))]),
        out_shape=jax.ShapeDtypeStruct((n_rows, w), jnp.int32),
        compiler_params=pltpu.CompilerParams(dimension_semantics=("arbitrary",),
                                             vmem_limit_bytes=VMEM_LIMIT),
        name="routed_experts",
    )(first_block, n_block, n_used, xg, w1, w3, w2)


def _combine_kernel(y_ref, x1_ref, sw1_ref, sw3_ref, sw2_ref, g2_ref, b2_ref, out_ref):
    x1 = x1_ref[...]
    xb = x1.astype(BF16)
    h1 = _dot(xb, sw1_ref[...])
    h3 = _dot(xb, sw3_ref[...])
    ffn = _dot((h1 * jax.nn.sigmoid(h1) * h3).astype(BF16), sw2_ref[...])
    ffn = ffn + _unpack_bf16_pairs(lax.bitcast_convert_type(y_ref[...], jnp.uint32))
    out_ref[...] = _layer_norm(DEEPNORM_ALPHA * x1 + ffn, g2_ref[...], b2_ref[...])


def _combine(y_routed, x1, sw1, sw3, sw2, g2, b2, tn=512):
    t, d = x1.shape
    w = y_routed.shape[1]
    row = lambda p: p.reshape(1, -1)
    operands = (y_routed, x1, sw1.astype(BF16), sw3.astype(BF16), sw2.astype(BF16), row(g2), row(b2))
    in_specs = [pl.BlockSpec((tn, w), lambda i: (i, 0)),
                pl.BlockSpec((tn, d), lambda i: (i, 0))]
    in_specs += [_const_spec(op.shape) for op in operands[2:]]
    return pl.pallas_call(
        _combine_kernel,
        grid=(t // tn,),
        in_specs=in_specs,
        out_specs=pl.BlockSpec((tn, d), lambda i: (i, 0)),
        out_shape=jax.ShapeDtypeStruct((t, d), F32),
        compiler_params=pltpu.CompilerParams(dimension_semantics=("arbitrary",),
                                             vmem_limit_bytes=VMEM_LIMIT),
        name="combine_shared_ln",
    )(*operands)


def _block_table(counts, bm):
    n_block = (counts + bm - 1) // bm
    ends = jnp.cumsum(n_block)
    first_block = ends - n_block
    i32 = lambda v: v.astype(jnp.int32)
    return i32(first_block * bm), i32(first_block), i32(n_block), i32(ends[-1:])


def _moe(x1, x1p, logits, router_bias, w1, w3, w2, sw1, sw3, sw2, g2, b2, bm=256):
    t, d = x1.shape
    n_exp = logits.shape[1]
    idx, gate, rank, counts = _route(logits, router_bias)
    n_blocks = (t * TOP_K) // bm + n_exp
    starts, first_block, n_block, n_used = _block_table(counts[:, 0], bm)
    slots, gate_words = _slots(idx, rank, gate, starts)
    xg = _sc_dispatch(slots, x1p, n_blocks * bm)
    yg = _experts(first_block, n_block, n_used, xg, w1, w3, w2, bm)
    y_routed = _sc_combine(slots, gate_words, yg)
    return _combine(y_routed, x1, sw1, sw3, sw2, g2, b2)


def kernel(x, w_in, conv_w, conv_b, lru_wa, lru_ba, lru_wx, lru_bx, lru_lambda, pool_w, pool_scale, w_out, ln1_g, ln1_b, router_w, router_bias, exp_w1, exp_w3, exp_w2, sh_w1, sh_w3, sh_w2, ln2_g, ln2_b):
    nb, s, d = x.shape
    for l in range(DEPTH):
        x1, x1p, logits = _mixer(x, w_in[l], conv_w[l], conv_b[l], lru_wa[l], lru_ba[l], lru_wx[l], lru_bx[l],
                                 lru_lambda[l], pool_w[l], pool_scale[l], w_out[l], ln1_g[l], ln1_b[l],
                                 router_w[l])
        t = nb * s
        x = _moe(x1.reshape(t, d), x1p.reshape(t, d // 2), logits.reshape(t, -1), router_bias[l],
                 exp_w1[l], exp_w3[l], exp_w2[l], sh_w1[l], sh_w3[l], sh_w2[l], ln2_g[l], ln2_b[l])
        x = x.reshape(nb, s, d)
    return x
```

```python
import functools
import math

import jax
import jax.numpy as jnp
from jax import lax
from jax.experimental import pallas as pl
from jax.experimental.pallas import tpu as pltpu
from jax.experimental.pallas import tpu_sc as plsc

LRU_HEADS = 8
CONV_WIDTH = 4
LRU_C = 8.0
POOL_WINDOWS = (2, 4, 8, 16)
N_EXPERT_GROUPS = 8
TOPK_GROUPS = 4
TOP_K = 8
ROUTED_SCALE = 2.5
LN_EPS = 1e-5
DEPTH = 1
DEEPNORM_ALPHA = (2.0 * DEPTH) ** 0.25

MXU_DIM = 256
POOL_GROUP_DIM = 128
CONV_HIST = 8
POOL_HIST = 16
SC_CHUNK = 128
SC_SUB = 8
MIXER_PARTS = 4
BLOCK_GROUP = 4
ROW_RING = 8
VMEM_LIMIT = 56 * 1024 * 1024

F32 = jnp.float32
BF16 = jnp.bfloat16


def _dot(a, b):
    return jnp.dot(a, b, preferred_element_type=F32)


def _layer_norm(z, g, b):
    mu = jnp.mean(z, axis=-1, keepdims=True)
    zc = z - mu
    var = jnp.mean(zc * zc, axis=-1, keepdims=True)
    return zc * lax.rsqrt(var + LN_EPS) * g + b


def _block_diag_dot(xb, w_ref):
    n = w_ref.shape[0]
    return jnp.concatenate(
        [_dot(xb[:, i * MXU_DIM:(i + 1) * MXU_DIM], w_ref[i]) for i in range(n)], axis=1)


def _causal_conv(u, hist_ref, part, conv_w, conv_b):
    pb, ts, c = u.shape
    ext = jnp.concatenate([hist_ref[part], u], axis=1)
    hist_ref[part] = u[:, ts - CONV_HIST:, :]
    uc = jnp.broadcast_to(conv_b.reshape(1, 1, c), (pb, ts, c))
    for k in range(CONV_WIDTH):
        off = CONV_HIST - (CONV_WIDTH - 1) + k
        uc = uc + conv_w[k:k + 1, :].reshape(1, 1, c) * ext[:, off:off + ts, :]
    return uc


def _lru_scan(uc, ga, gx, lam, carry_ref, part, t0):
    pb, ts, c = uc.shape
    r = jax.nn.sigmoid(ga)
    i_gate = jax.nn.sigmoid(gx)
    softplus_neg_lam = jnp.maximum(-lam, 0.0) + jnp.log1p(jnp.exp(-jnp.abs(lam)))
    log_a = (-LRU_C) * r * softplus_neg_lam
    a = jnp.exp(log_a)
    mult = jnp.sqrt(jnp.tanh(-log_a) * (1.0 + a * a))
    t_local = lax.broadcasted_iota(jnp.int32, (pb, ts, c), 1)
    first = (t_local + t0) == 0
    a3 = a.reshape(pb, ts, c)
    x3 = jnp.where(first, 1.0, mult.reshape(pb, ts, c)) * i_gate.reshape(pb, ts, c) * uc

    shift = 1
    while shift < ts:
        valid = t_local >= shift
        a_prev = pltpu.roll(a3, shift, axis=1)
        x_prev = pltpu.roll(x3, shift, axis=1)
        x3 = jnp.where(valid, a3 * x_prev, 0.0) + x3
        a3 = jnp.where(valid, a3 * a_prev, a3)
        shift *= 2
    h = x3 + a3 * carry_ref[part][:, None, :]
    carry_ref[part] = h[:, ts - 1, :]
    return h


def _multiscale_pool(v, hist_ref, part, t0):
    pb, ts, c = v.shape
    ext = jnp.concatenate([hist_ref[part], v], axis=1)
    hist_ref[part] = v[:, ts - POOL_HIST:, :]
    t_glob = lax.broadcasted_iota(jnp.int32, (1, ts, POOL_GROUP_DIM), 1) + t0
    pooled = []
    for g, win in enumerate(POOL_WINDOWS):
        lanes = slice(g * POOL_GROUP_DIM, (g + 1) * POOL_GROUP_DIM)
        s = ext[:, :, lanes]
        shift = 1
        while shift < win:
            s = s + pltpu.roll(s, shift, axis=1)
            shift *= 2
        count = jnp.minimum(t_glob + 1, win).astype(F32)
        pooled.append(s[:, POOL_HIST:, :] / count - v[:, :, lanes])
    return jnp.concatenate(pooled, axis=-1)


def _mixer_kernel(x_ref, w_in_ref, conv_w_ref, conv_b_ref, wa_ref, ba_ref, wx_ref, bx_ref, lam_ref,
                  pool_w_ref, pool_scale_ref, w_out_ref, g1_ref, b1_ref, rw_hi_ref, rw_lo_ref,
                  x1_ref, x1p_ref, logit_ref, hist_u, hist_v, carry):
    j = pl.program_id(0)
    nb, ts, d = x_ref.shape
    c = hist_u.shape[-1]
    pb = nb // MIXER_PARTS
    rows = pb * ts
    t0 = j * ts
    parts = [slice(p * pb, (p + 1) * pb) for p in range(MIXER_PARTS)]

    @pl.when(j == 0)
    def _():
        hist_u[...] = jnp.zeros_like(hist_u)
        hist_v[...] = jnp.zeros_like(hist_v)
        carry[...] = jnp.zeros_like(carry)

    xs = [x_ref[p].reshape(rows, d) for p in parts]
    projs = [_dot(x.astype(BF16), w_in_ref[...]) for x in xs]
    ucs = [_causal_conv(proj[:, :c].reshape(pb, ts, c), hist_u, p, conv_w_ref[...], conv_b_ref[...])
           for proj, p in zip(projs, parts)]
    ucbs = [uc.reshape(rows, c).astype(BF16) for uc in ucs]
    gas = [_block_diag_dot(ucb, wa_ref) + ba_ref[...] for ucb in ucbs]
    gxs = [_block_diag_dot(ucb, wx_ref) + bx_ref[...] for ucb in ucbs]
    hs = [_lru_scan(uc, ga, gx, lam_ref[...], carry, p, t0) for uc, ga, gx, p in zip(ucs, gas, gxs, parts)]
    y_lrus = []
    for h, proj in zip(hs, projs):
        gate = proj[:, c:2 * c]
        gelu_gate = 0.5 * gate * (1.0 + jnp.tanh(math.sqrt(2.0 / math.pi) * (gate + 0.044715 * gate * gate * gate)))
        y_lrus.append(h.reshape(rows, c) * gelu_gate)
    pooled = [_multiscale_pool(proj[:, 2 * c:].reshape(pb, ts, c), hist_v, p, t0).reshape(rows, c).astype(BF16)
              for proj, p in zip(projs, parts)]
    y_pools = [_block_diag_dot(pl_, pool_w_ref) * pool_scale_ref[...] for pl_ in pooled]
    mixes = [_dot(jnp.concatenate([y_lru, y_pool], axis=1).astype(BF16), w_out_ref[...])
             for y_lru, y_pool in zip(y_lrus, y_pools)]
    x1s = [_layer_norm(DEEPNORM_ALPHA * x + mix, g1_ref[...], b1_ref[...]) for x, mix in zip(xs, mixes)]

    for p, x1 in zip(parts, x1s):
        x1_ref[p] = x1.reshape(pb, ts, d)
        hi = x1.astype(BF16)
        hi_f = hi.astype(F32)
        lo = (x1 - hi_f).astype(BF16)
        logits = _dot(hi, rw_hi_ref[...]) + _dot(lo, rw_hi_ref[...]) + _dot(hi, rw_lo_ref[...])
        logit_ref[p] = logits.reshape(pb, ts, logits.shape[-1])
        bits = lax.bitcast_convert_type(hi_f, jnp.uint32)
        packed = bits[:, :d // 2] | (bits[:, d // 2:] >> 16)
        x1p_ref[p] = packed.reshape(pb, ts, d // 2)


def _const_spec(shape):
    return pl.BlockSpec(shape, lambda j: (0,) * len(shape), pipeline_mode=pl.Buffered(1))


def _regroup_block_diag(w, width):
    heads, hd, _ = w.shape
    per = width // hd
    w = w.reshape(heads // per, per, hd, hd)
    eye = jnp.eye(per, dtype=w.dtype)
    return jnp.einsum("gpij,pq->gpiqj", w, eye).reshape(heads // per, width, width)


def _mixer(x, w_in, conv_w, conv_b, wa, ba, wx, bx, lam, pool_w, pool_scale, w_out, g1, b1, router_w, ts=64):
    nb, s, d = x.shape
    c = conv_w.shape[-1]
    n_exp = router_w.shape[-1]
    rw_hi = router_w.astype(BF16)
    rw_lo = (router_w - rw_hi.astype(F32)).astype(BF16)
    row = lambda p: p.reshape(1, -1)
    operands = (
        x, w_in.astype(BF16), conv_w, row(conv_b),
        _regroup_block_diag(wa, MXU_DIM).astype(BF16), row(ba),
        _regroup_block_diag(wx, MXU_DIM).astype(BF16), row(bx), row(lam),
        _regroup_block_diag(pool_w, MXU_DIM).astype(BF16), row(pool_scale),
        w_out.astype(BF16), row(g1), row(b1), rw_hi, rw_lo)
    tile = lambda width: pl.BlockSpec((nb, ts, width), lambda j: (0, j, 0))
    in_specs = [tile(d)] + [_const_spec(op.shape) for op in operands[1:]]
    return pl.pallas_call(
        _mixer_kernel,
        grid=(s // ts,),
        in_specs=in_specs,
        out_specs=[tile(d), tile(d // 2), tile(n_exp)],
        out_shape=[jax.ShapeDtypeStruct((nb, s, d), F32),
                   jax.ShapeDtypeStruct((nb, s, d // 2), jnp.uint32),
                   jax.ShapeDtypeStruct((nb, s, n_exp), F32)],
        scratch_shapes=[pltpu.VMEM((nb, CONV_HIST, c), F32),
                        pltpu.VMEM((nb, POOL_HIST, c), F32),
                        pltpu.VMEM((nb, c), F32)],
        compiler_params=pltpu.CompilerParams(dimension_semantics=("arbitrary",),
                                             vmem_limit_bytes=VMEM_LIMIT),
        name="mixer_ln_router",
    )(*operands)


def _first_row_of(cond, rows_rev, n):
    return n - jnp.max(jnp.where(cond, rows_rev, 0.0), axis=0, keepdims=True)


def _route_kernel(logit_ref, bias_ref, idx_ref, gate_ref, rank_ref, count_ref, count_scr):
    i = pl.program_id(0)

    @pl.when(i == 0)
    def _():
        count_scr[...] = jnp.zeros_like(count_scr)

    scores = jax.nn.sigmoid(logit_ref[...].T)
    n_exp, tn = scores.shape
    gsz = n_exp // N_EXPERT_GROUPS
    neg = -jnp.inf
    biased = scores + bias_ref[...]
    row = lax.broadcasted_iota(jnp.int32, (n_exp, tn), 0).astype(F32)
    row_rev = n_exp - row

    g_row = lax.broadcasted_iota(jnp.int32, (N_EXPERT_GROUPS, tn), 0)
    group_score = jnp.zeros((N_EXPERT_GROUPS, tn), F32)
    grp_row = lax.broadcasted_iota(jnp.int32, (gsz, tn), 0).astype(F32)
    grp_rev = gsz - grp_row
    for g in range(N_EXPERT_GROUPS):
        blk = biased[g * gsz:(g + 1) * gsz]
        m1 = jnp.max(blk, axis=0, keepdims=True)
        i1 = _first_row_of(blk == m1, grp_rev, gsz)
        m2 = jnp.max(jnp.where(grp_row == i1, neg, blk), axis=0, keepdims=True)
        group_score = jnp.where(g_row == g, m1 + m2, group_score)

    beaten_by = jnp.zeros((N_EXPERT_GROUPS, tn), F32)
    for g in range(N_EXPERT_GROUPS):
        other = jnp.max(jnp.where(g_row == g, group_score, neg), axis=0, keepdims=True)
        wins = (other > group_score) | ((other == group_score) & (g < g_row))
        beaten_by = beaten_by + jnp.where(wins, 1.0, 0.0)
    masked = []
    for g in range(N_EXPERT_GROUPS):
        beaten_g = jnp.max(jnp.where(g_row == g, beaten_by, 0.0), axis=0, keepdims=True)
        masked.append(jnp.where(beaten_g < TOPK_GROUPS, biased[g * gsz:(g + 1) * gsz], neg))
    masked = jnp.concatenate(masked, axis=0)

    k_row = lax.broadcasted_iota(jnp.int32, (TOP_K, tn), 0)
    selected = jnp.zeros((n_exp, tn), F32)
    idx = jnp.zeros((TOP_K, tn), F32)
    gate = jnp.zeros((TOP_K, tn), F32)
    hits = []
    for k in range(TOP_K):
        m = jnp.max(masked, axis=0, keepdims=True)
        ik = _first_row_of(masked == m, row_rev, n_exp)
        hit = row == ik
        gk = jnp.sum(jnp.where(hit, scores, 0.0), axis=0, keepdims=True)
        masked = jnp.where(hit, neg, masked)
        selected = jnp.where(hit, 1.0, selected)
        idx = jnp.where(k_row == k, ik, idx)
        gate = jnp.where(k_row == k, gk, gate)
        hits.append(hit)
    gate = gate / jnp.sum(gate, axis=0, keepdims=True) * ROUTED_SCALE

    earlier = (lax.broadcasted_iota(jnp.int32, (tn, tn), 0) < lax.broadcasted_iota(jnp.int32, (tn, tn), 1))
    before = _dot(selected.astype(BF16), jnp.where(earlier, 1.0, 0.0).astype(BF16)) + count_scr[...]
    rank = jnp.zeros((TOP_K, tn), F32)
    for k in range(TOP_K):
        rk = jnp.sum(jnp.where(hits[k], before, 0.0), axis=0, keepdims=True)
        rank = jnp.where(k_row == k, rk, rank)
    count_scr[...] = count_scr[...] + jnp.sum(selected, axis=1, keepdims=True)

    idx_ref[...] = idx.astype(jnp.int32)
    gate_ref[...] = gate
    rank_ref[...] = rank.astype(jnp.int32)
    count_ref[...] = count_scr[...].astype(jnp.int32)


def _route(logits, bias, tn=512):
    t, n_exp = logits.shape
    return pl.pallas_call(
        _route_kernel,
        grid=(t // tn,),
        in_specs=[pl.BlockSpec((tn, n_exp), lambda i: (i, 0)),
                  pl.BlockSpec((n_exp, 1), lambda i: (0, 0))],
        out_specs=[pl.BlockSpec((TOP_K, tn), lambda i: (0, i)),
                   pl.BlockSpec((TOP_K, tn), lambda i: (0, i)),
                   pl.BlockSpec((TOP_K, tn), lambda i: (0, i)),
                   pl.BlockSpec((n_exp, 1), lambda i: (0, 0))],
        out_shape=[jax.ShapeDtypeStruct((TOP_K, t), jnp.int32),
                   jax.ShapeDtypeStruct((TOP_K, t), F32),
                   jax.ShapeDtypeStruct((TOP_K, t), jnp.int32),
                   jax.ShapeDtypeStruct((n_exp, 1), jnp.int32)],
        scratch_shapes=[pltpu.VMEM((n_exp, 1), F32)],
        compiler_params=pltpu.CompilerParams(dimension_semantics=("arbitrary",),
                                             vmem_limit_bytes=VMEM_LIMIT),
        name="route_topk_rank",
    )(logits, bias.reshape(n_exp, 1))


def _slot_kernel(idx_ref, rank_ref, gate_ref, start_ref, slot_ref, gate_word_ref):
    idx = idx_ref[...]
    n_exp = start_ref.shape[0]
    tn = idx.shape[1]
    row = lax.broadcasted_iota(jnp.int32, (n_exp, tn), 0)
    start = start_ref[...]
    first = [jnp.sum(jnp.where(row == idx[k:k + 1], start, 0), axis=0, keepdims=True) for k in range(TOP_K)]
    slots = jnp.concatenate(first, axis=0) + rank_ref[...]
    bits = lax.bitcast_convert_type(gate_ref[...].astype(BF16).astype(F32), jnp.uint32)
    words = lax.bitcast_convert_type(bits | (bits >> 16), jnp.int32)
    chunk = slot_ref.shape[-1]
    for c in range(slot_ref.shape[0]):
        slot_ref[c] = slots[:, c * chunk:(c + 1) * chunk]
        gate_word_ref[c] = words[:, c * chunk:(c + 1) * chunk]


def _slots(idx, rank, gate, expert_start, tn=512):
    k, t = idx.shape
    n_exp = expert_start.shape[0]
    per_step = tn // SC_CHUNK
    tile = pl.BlockSpec((k, tn), lambda i: (0, i))
    chunked = pl.BlockSpec((per_step, k, SC_CHUNK), lambda i: (i, 0, 0))
    chunked_shape = jax.ShapeDtypeStruct((t // SC_CHUNK, k, SC_CHUNK), jnp.int32)
    return pl.pallas_call(
        _slot_kernel,
        grid=(t // tn,),
        in_specs=[tile, tile, tile, pl.BlockSpec((n_exp, 1), lambda i: (0, 0))],
        out_specs=[chunked, chunked],
        out_shape=[chunked_shape, chunked_shape],
        compiler_params=pltpu.CompilerParams(dimension_semantics=("arbitrary",)),
        name="dispatch_slots",
    )(idx, rank, gate, expert_start.reshape(n_exp, 1))


def _unpack_bf16_pairs(p):
    hi = lax.bitcast_convert_type(p & jnp.uint32(0xFFFF0000), F32)
    lo = lax.bitcast_convert_type(p << 16, F32)
    return jnp.concatenate([hi, lo], axis=1)


def _pack_bf16_pairs(y):
    n = y.shape[1] // 2
    bits = lax.bitcast_convert_type(y.astype(BF16).astype(F32), jnp.uint32)
    return bits[:, :n] | (bits[:, n:] >> 16)


def _sc_worker_layout(n_chunks_total):
    info = plsc.get_sparse_core_info()
    n_workers = info.num_cores * info.num_subcores
    return info.num_cores, n_chunks_total // n_workers


def _sc_dispatch(slots, x1p, n_rows):
    n_chunks_total, n_k, chunk = slots.shape
    t, w = x1p.shape
    n_cores, n_chunks = _sc_worker_layout(n_chunks_total)
    mesh = plsc.VectorSubcoreMesh(core_axis_name="c", subcore_axis_name="s")

    @functools.partial(
        pl.kernel, mesh=mesh, name="sc_dispatch_rows",
        out_type=jax.ShapeDtypeStruct((n_rows, w), x1p.dtype),
        scratch_types=[pltpu.VMEM((n_k, chunk), jnp.int32), pltpu.VMEM((chunk, w), x1p.dtype),
                       pltpu.SemaphoreType.DMA])
    def dispatch(slots_hbm, x_hbm, xg_hbm, idx_v, rows_v, sem):
        wid = lax.axis_index("s") * n_cores + lax.axis_index("c")

        @pl.loop(0, n_chunks)
        def _(ci):
            chunk_id = wid * n_chunks + ci
            pltpu.sync_copy(slots_hbm.at[chunk_id], idx_v)
            pltpu.sync_copy(x_hbm.at[pl.ds(chunk_id * chunk, chunk)], rows_v)
            copies = [pltpu.make_async_copy(rows_v, xg_hbm.at[idx_v.at[k]], sem) for k in range(n_k)]
            for cp in copies:
                cp.start()
            for cp in copies:
                cp.wait()

    return dispatch(slots, x1p)


def _sc_combine(slots, gate_words, yg):
    n_chunks_total, n_k, chunk = slots.shape
    w = yg.shape[1]
    t = n_chunks_total * chunk
    n_cores, n_chunks = _sc_worker_layout(n_chunks_total)
    lanes = plsc.get_sparse_core_info().num_lanes
    n_sub = chunk // SC_SUB
    mesh = plsc.VectorSubcoreMesh(core_axis_name="c", subcore_axis_name="s")
    rows_t = pltpu.VMEM((n_k, SC_SUB, w), jnp.int32)
    out_t = pltpu.VMEM((SC_SUB, w), jnp.int32)
    dma = pltpu.SemaphoreType.DMA

    @functools.partial(
        pl.kernel, mesh=mesh, name="sc_combine_rows",
        compiler_params=pltpu.CompilerParams(needs_layout_passes=False),
        out_type=jax.ShapeDtypeStruct((t, w), jnp.int32),
        scratch_types=[pltpu.VMEM((n_k, chunk), jnp.int32), pltpu.VMEM((n_k, chunk), jnp.int32),
                       rows_t, rows_t, out_t, out_t, dma, dma, dma, dma])
    def combine(slots_hbm, gate_hbm, yg_hbm, out_hbm, idx_v, gate_v, rows0, rows1, out0, out1,
                row_sem0, row_sem1, out_sem0, out_sem1):
        wid = lax.axis_index("s") * n_cores + lax.axis_index("c")
        rows, outs = (rows0, rows1), (out0, out1)
        row_sems, out_sems = (row_sem0, row_sem1), (out_sem0, out_sem1)

        def gathers(sub, b):
            return [pltpu.make_async_copy(yg_hbm.at[idx_v.at[k, pl.ds(sub * SC_SUB, SC_SUB)]], rows[b].at[k],
                                          row_sems[b]) for k in range(n_k)]

        @pl.loop(0, n_chunks)
        def _(ci):
            chunk_id = wid * n_chunks + ci
            tok0 = chunk_id * chunk
            pltpu.sync_copy(slots_hbm.at[chunk_id], idx_v)
            pltpu.sync_copy(gate_hbm.at[chunk_id], gate_v)

            def store(sub, b):
                return pltpu.make_async_copy(outs[b], out_hbm.at[pl.ds(tok0 + sub * SC_SUB, SC_SUB)], out_sems[b])

            for cp in gathers(0, 0):
                cp.start()

            @pl.loop(0, n_sub // 2)
            def _(pair):
                for b in (0, 1):
                    sub = 2 * pair + b

                    @pl.when(sub + 1 < n_sub)
                    def _():
                        for cp in gathers(sub + 1, 1 - b):
                            cp.start()

                    for cp in gathers(sub, b):
                        cp.wait()

                    @pl.when(sub >= 2)
                    def _():
                        store(sub - 2, b).wait()

                    buf, out = rows[b], outs[b]

                    @pl.loop(0, SC_SUB)
                    def _(tt):
                        col = jnp.full((lanes,), sub * SC_SUB + tt, jnp.int32)
                        gates = [plsc.bitcast(plsc.load_gather(gate_v, [jnp.full((lanes,), k, jnp.int32), col]), BF16)
                                 for k in range(n_k)]
                        for v in range(w // lanes):
                            terms = [gates[k] * plsc.bitcast(buf[k, tt, pl.ds(v * lanes, lanes)], BF16)
                                     for k in range(n_k)]
                            while len(terms) > 1:
                                terms = [terms[i] + terms[i + 1] for i in range(0, len(terms), 2)]
                            out[tt, pl.ds(v * lanes, lanes)] = plsc.bitcast(terms[0], jnp.int32)

                    store(sub, b).start()

            store(n_sub - 2, 0).wait()
            store(n_sub - 1, 1).wait()

    return combine(slots, gate_words, yg)


def _expert_kernel(first_block, n_block, n_used, xg_hbm, w1_ref, w3_ref, w2_ref, yg_hbm,
                   x_buf, y_buf, w1_bf, w3_bf, w2_bf, in_sem, out_sem):
    e = pl.program_id(0)
    n_slot, bm, _ = x_buf.shape
    ahead = n_slot - BLOCK_GROUP
    total = n_used[0]

    def fetch(g):
        slot = g % n_slot
        return pltpu.make_async_copy(xg_hbm.at[pl.ds(g * bm, bm)], x_buf.at[slot], in_sem.at[slot])

    def write_back(g):
        slot = g % n_slot
        return pltpu.make_async_copy(y_buf.at[slot], yg_hbm.at[pl.ds(g * bm, bm)], out_sem.at[slot])

    @pl.when(e == 0)
    def _():
        for g in range(ahead):
            @pl.when(g < total)
            def _():
                fetch(g).start()

    def cast_weights():
        ws = [w1_ref[0].astype(BF16), w3_ref[0].astype(BF16), w2_ref[0].astype(BF16)]
        for ref, wb in zip((w1_bf, w3_bf, w2_bf), ws):
            ref[...] = wb
        return ws

    def swiglu(slots, weights):
        w1b, w3b, w2b = weights
        xs = [_unpack_bf16_pairs(x_buf[s]).astype(BF16) for s in slots]
        up = [(_dot(xb, w1b), _dot(xb, w3b)) for xb in xs]
        hs = [(h1 * jax.nn.sigmoid(h1) * h3).astype(BF16) for h1, h3 in up]
        ys = [_dot(h, w2b) for h in hs]
        return [lax.bitcast_convert_type(_pack_bf16_pairs(y), y_buf.dtype) for y in ys]

    def process(blocks, cast_here=False):
        for g in blocks:
            fetch(g).wait()
        for g in blocks:
            @pl.when(g + ahead < total)
            def _():
                fetch(g + ahead).start()
        weights = cast_weights() if cast_here else (w1_bf[...], w3_bf[...], w2_bf[...])
        ys = swiglu([g % n_slot for g in blocks], weights)
        for g in blocks:
            @pl.when(g >= n_slot)
            def _():
                write_back(g - n_slot).wait()
        for g, y in zip(blocks, ys):
            y_buf[g % n_slot] = y
            write_back(g).start()

    n_full = n_block[e] // BLOCK_GROUP

    @pl.when(n_full > 0)
    def _():
        process([first_block[e] + j for j in range(BLOCK_GROUP)], cast_here=True)

    @pl.when((n_full == 0) & (n_block[e] > 0))
    def _():
        cast_weights()

    @pl.loop(1, n_full)
    def _(i):
        process([first_block[e] + i * BLOCK_GROUP + j for j in range(BLOCK_GROUP)])

    done = n_full * BLOCK_GROUP
    size = BLOCK_GROUP // 2
    while size >= 1:
        left = n_block[e] - done
        start = first_block[e] + done

        @pl.when(left >= size)
        def _(start=start, size=size):
            process([start + j for j in range(size)])

        done = done + left // size * size
        size //= 2

    @pl.when(e == pl.num_programs(0) - 1)
    def _():
        for back in range(1, n_slot + 1):
            @pl.when(total >= back)
            def _():
                write_back(total - back).wait()


def _experts(first_block, n_block, n_used, xg, w1, w3, w2, bm):
    n_rows, w = xg.shape
    n_exp, d, de = w1.shape
    weight = lambda e, *_: (e, 0, 0)
    return pl.pallas_call(
        _expert_kernel,
        grid_spec=pltpu.PrefetchScalarGridSpec(
            num_scalar_prefetch=3,
            grid=(n_exp,),
            in_specs=[pl.BlockSpec(memory_space=pl.ANY),
                      pl.BlockSpec((1, d, de), weight),
                      pl.BlockSpec((1, d, de), weight),
                      pl.BlockSpec((1, de, d), weight)],
            out_specs=pl.BlockSpec(memory_space=pl.ANY),
            scratch_shapes=[pltpu.VMEM((ROW_RING, bm, w), xg.dtype), pltpu.VMEM((ROW_RING, bm, w), jnp.int32),
                            pltpu.VMEM((d, de), BF16), pltpu.VMEM((d, de), BF16), pltpu.VMEM((de, d), BF16),
                            pltpu.SemaphoreType.DMA((ROW_RING,)), pltpu.SemaphoreType.DMA((ROW_RING,))]),
        out_shape=jax.ShapeDtypeStruct((n_rows, w), jnp.int32),
        compiler_params=pltpu.CompilerParams(dimension_semantics=("arbitrary",),
                                             vmem_limit_bytes=VMEM_LIMIT),
        name="routed_experts",
    )(first_block, n_block, n_used, xg, w1, w3, w2)


def _combine_kernel(y_ref, x1_ref, sw1_ref, sw3_ref, sw2_ref, g2_ref, b2_ref, out_ref):
    x1 = x1_ref[...]
    xb = x1.astype(BF16)
    h1 = _dot(xb, sw1_ref[...])
    h3 = _dot(xb, sw3_ref[...])
    ffn = _dot((h1 * jax.nn.sigmoid(h1) * h3).astype(BF16), sw2_ref[...])
    ffn = ffn + _unpack_bf16_pairs(lax.bitcast_convert_type(y_ref[...], jnp.uint32))
    out_ref[...] = _layer_norm(DEEPNORM_ALPHA * x1 + ffn, g2_ref[...], b2_ref[...])


def _combine(y_routed, x1, sw1, sw3, sw2, g2, b2, tn=512):
    t, d = x1.shape
    w = y_routed.shape[1]
    row = lambda p: p.reshape(1, -1)
    operands = (y_routed, x1, sw1.astype(BF16), sw3.astype(BF16), sw2.astype(BF16), row(g2), row(b2))
    in_specs = [pl.BlockSpec((tn, w), lambda i: (i, 0)),
                pl.BlockSpec((tn, d), lambda i: (i, 0))]
    in_specs += [_const_spec(op.shape) for op in operands[2:]]
    return pl.pallas_call(
        _combine_kernel,
        grid=(t // tn,),
        in_specs=in_specs,
        out_specs=pl.BlockSpec((tn, d), lambda i: (i, 0)),
        out_shape=jax.ShapeDtypeStruct((t, d), F32),
        compiler_params=pltpu.CompilerParams(dimension_semantics=("arbitrary",),
                                             vmem_limit_bytes=VMEM_LIMIT),
        name="combine_shared_ln",
    )(*operands)


def _block_table(counts, bm):
    n_block = (counts + bm - 1) // bm
    ends = jnp.cumsum(n_block)
    first_block = ends - n_block
    i32 = lambda v: v.astype(jnp.int32)
    return i32(first_block * bm), i32(first_block), i32(n_block), i32(ends[-1:])


def _moe(x1, x1p, logits, router_bias, w1, w3, w2, sw1, sw3, sw2, g2, b2, bm=256):
    t, d = x1.shape
    n_exp = logits.shape[1]
    idx, gate, rank, counts = _route(logits, router_bias)
    n_blocks = (t * TOP_K) // bm + n_exp
    starts, first_block, n_block, n_used = _block_table(counts[:, 0], bm)
    slots, gate_words = _slots(idx, rank, gate, starts)
    xg = _sc_dispatch(slots, x1p, n_blocks * bm)
    yg = _experts(first_block, n_block, n_used, xg, w1, w3, w2, bm)
    y_routed = _sc_combine(slots, gate_words, yg)
    return _combine(y_routed, x1, sw1, sw3, sw2, g2, b2)


def kernel(x, w_in, conv_w, conv_b, lru_wa, lru_ba, lru_wx, lru_bx, lru_lambda, pool_w, pool_scale, w_out, ln1_g, ln1_b, router_w, router_bias, exp_w1, exp_w3, exp_w2, sh_w1, sh_w3, sh_w2, ln2_g, ln2_b):
    nb, s, d = x.shape
    for l in range(DEPTH):
        x1, x1p, logits = _mixer(x, w_in[l], conv_w[l], conv_b[l], lru_wa[l], lru_ba[l], lru_wx[l], lru_bx[l],
                                 lru_lambda[l], pool_w[l], pool_scale[l], w_out[l], ln1_g[l], ln1_b[l],
                                 router_w[l])
        t = nb * s
        x = _moe(x1.reshape(t, d), x1p.reshape(t, d // 2), logits.reshape(t, -1), router_bias[l],
                 exp_w1[l], exp_w3[l], exp_w2[l], sh_w1[l], sh_w3[l], sh_w2[l], ln2_g[l], ln2_b[l])
        x = x.reshape(nb, s, d)
    return x
```

```python
import functools
import math

import jax
import jax.numpy as jnp
from jax import lax
from jax.experimental import pallas as pl
from jax.experimental.pallas import tpu as pltpu
from jax.experimental.pallas import tpu_sc as plsc

LRU_HEADS = 8
CONV_WIDTH = 4
LRU_C = 8.0
POOL_WINDOWS = (2, 4, 8, 16)
N_EXPERT_GROUPS = 8
TOPK_GROUPS = 4
TOP_K = 8
ROUTED_SCALE = 2.5
LN_EPS = 1e-5
DEPTH = 1
DEEPNORM_ALPHA = (2.0 * DEPTH) ** 0.25

MXU_DIM = 256
POOL_GROUP_DIM = 128
CONV_HIST = 8
POOL_HIST = 16
SC_CHUNK = 128
SC_SUB = 8
MIXER_PARTS = 4
WEIGHT_BUFFERS = 2
WEIGHT_CHUNKS = 8
BLOCK_GROUP = 4
ROW_RING = 8
VMEM_LIMIT = 56 * 1024 * 1024

F32 = jnp.float32
BF16 = jnp.bfloat16


def _dot(a, b):
    return jnp.dot(a, b, preferred_element_type=F32)


def _layer_norm(z, g, b):
    mu = jnp.mean(z, axis=-1, keepdims=True)
    zc = z - mu
    var = jnp.mean(zc * zc, axis=-1, keepdims=True)
    return zc * lax.rsqrt(var + LN_EPS) * g + b


def _block_diag_dot(xb, w_ref):
    n = w_ref.shape[0]
    return jnp.concatenate(
        [_dot(xb[:, i * MXU_DIM:(i + 1) * MXU_DIM], w_ref[i]) for i in range(n)], axis=1)


def _causal_conv(u, hist_ref, part, conv_w, conv_b):
    pb, ts, c = u.shape
    ext = jnp.concatenate([hist_ref[part], u], axis=1)
    hist_ref[part] = u[:, ts - CONV_HIST:, :]
    uc = jnp.broadcast_to(conv_b.reshape(1, 1, c), (pb, ts, c))
    for k in range(CONV_WIDTH):
        off = CONV_HIST - (CONV_WIDTH - 1) + k
        uc = uc + conv_w[k:k + 1, :].reshape(1, 1, c) * ext[:, off:off + ts, :]
    return uc


def _lru_scan(uc, ga, gx, lam, carry_ref, part, t0):
    pb, ts, c = uc.shape
    r = jax.nn.sigmoid(ga)
    i_gate = jax.nn.sigmoid(gx)
    softplus_neg_lam = jnp.maximum(-lam, 0.0) + jnp.log1p(jnp.exp(-jnp.abs(lam)))
    log_a = (-LRU_C) * r * softplus_neg_lam
    a = jnp.exp(log_a)
    mult = jnp.sqrt(jnp.tanh(-log_a) * (1.0 + a * a))
    t_local = lax.broadcasted_iota(jnp.int32, (pb, ts, c), 1)
    first = (t_local + t0) == 0
    a3 = a.reshape(pb, ts, c)
    x3 = jnp.where(first, 1.0, mult.reshape(pb, ts, c)) * i_gate.reshape(pb, ts, c) * uc

    shift = 1
    while shift < ts:
        valid = t_local >= shift
        a_prev = pltpu.roll(a3, shift, axis=1)
        x_prev = pltpu.roll(x3, shift, axis=1)
        x3 = jnp.where(valid, a3 * x_prev, 0.0) + x3
        a3 = jnp.where(valid, a3 * a_prev, a3)
        shift *= 2
    h = x3 + a3 * carry_ref[part][:, None, :]
    carry_ref[part] = h[:, ts - 1, :]
    return h


def _multiscale_pool(v, hist_ref, part, t0):
    pb, ts, c = v.shape
    ext = jnp.concatenate([hist_ref[part], v], axis=1)
    hist_ref[part] = v[:, ts - POOL_HIST:, :]
    t_glob = lax.broadcasted_iota(jnp.int32, (1, ts, POOL_GROUP_DIM), 1) + t0
    pooled = []
    for g, win in enumerate(POOL_WINDOWS):
        lanes = slice(g * POOL_GROUP_DIM, (g + 1) * POOL_GROUP_DIM)
        s = ext[:, :, lanes]
        shift = 1
        while shift < win:
            s = s + pltpu.roll(s, shift, axis=1)
            shift *= 2
        count = jnp.minimum(t_glob + 1, win).astype(F32)
        pooled.append(s[:, POOL_HIST:, :] / count - v[:, :, lanes])
    return jnp.concatenate(pooled, axis=-1)


def _mixer_kernel(x_ref, w_in_ref, conv_w_ref, conv_b_ref, wa_ref, ba_ref, wx_ref, bx_ref, lam_ref,
                  pool_w_ref, pool_scale_ref, w_out_ref, g1_ref, b1_ref, rw_hi_ref, rw_lo_ref,
                  x1_ref, x1p_ref, logit_ref, hist_u, hist_v, carry):
    j = pl.program_id(0)
    nb, ts, d = x_ref.shape
    c = hist_u.shape[-1]
    pb = nb // MIXER_PARTS
    rows = pb * ts
    t0 = j * ts
    parts = [slice(p * pb, (p + 1) * pb) for p in range(MIXER_PARTS)]

    @pl.when(j == 0)
    def _():
        hist_u[...] = jnp.zeros_like(hist_u)
        hist_v[...] = jnp.zeros_like(hist_v)
        carry[...] = jnp.zeros_like(carry)

    xs = [x_ref[p].reshape(rows, d) for p in parts]
    projs = [_dot(x.astype(BF16), w_in_ref[...]) for x in xs]
    ucs = [_causal_conv(proj[:, :c].reshape(pb, ts, c), hist_u, p, conv_w_ref[...], conv_b_ref[...])
           for proj, p in zip(projs, parts)]
    ucbs = [uc.reshape(rows, c).astype(BF16) for uc in ucs]
    gas = [_block_diag_dot(ucb, wa_ref) + ba_ref[...] for ucb in ucbs]
    gxs = [_block_diag_dot(ucb, wx_ref) + bx_ref[...] for ucb in ucbs]
    hs = [_lru_scan(uc, ga, gx, lam_ref[...], carry, p, t0) for uc, ga, gx, p in zip(ucs, gas, gxs, parts)]
    y_lrus = []
    for h, proj in zip(hs, projs):
        gate = proj[:, c:2 * c]
        gelu_gate = 0.5 * gate * (1.0 + jnp.tanh(math.sqrt(2.0 / math.pi) * (gate + 0.044715 * gate * gate * gate)))
        y_lrus.append(h.reshape(rows, c) * gelu_gate)
    pooled = [_multiscale_pool(proj[:, 2 * c:].reshape(pb, ts, c), hist_v, p, t0).reshape(rows, c).astype(BF16)
              for proj, p in zip(projs, parts)]
    y_pools = [_block_diag_dot(pl_, pool_w_ref) * pool_scale_ref[...] for pl_ in pooled]
    mixes = [_dot(jnp.concatenate([y_lru, y_pool], axis=1).astype(BF16), w_out_ref[...])
             for y_lru, y_pool in zip(y_lrus, y_pools)]
    x1s = [_layer_norm(DEEPNORM_ALPHA * x + mix, g1_ref[...], b1_ref[...]) for x, mix in zip(xs, mixes)]

    for p, x1 in zip(parts, x1s):
        x1_ref[p] = x1.reshape(pb, ts, d)
        hi = x1.astype(BF16)
        hi_f = hi.astype(F32)
        lo = (x1 - hi_f).astype(BF16)
        logits = _dot(hi, rw_hi_ref[...]) + _dot(lo, rw_hi_ref[...]) + _dot(hi, rw_lo_ref[...])
        logit_ref[p] = logits.reshape(pb, ts, logits.shape[-1])
        bits = lax.bitcast_convert_type(hi_f, jnp.uint32)
        packed = bits[:, :d // 2] | (bits[:, d // 2:] >> 16)
        x1p_ref[p] = packed.reshape(pb, ts, d // 2)


def _const_spec(shape):
    return pl.BlockSpec(shape, lambda j: (0,) * len(shape), pipeline_mode=pl.Buffered(1))


def _regroup_block_diag(w, width):
    heads, hd, _ = w.shape
    per = width // hd
    w = w.reshape(heads // per, per, hd, hd)
    eye = jnp.eye(per, dtype=w.dtype)
    return jnp.einsum("gpij,pq->gpiqj", w, eye).reshape(heads // per, width, width)


def _mixer(x, w_in, conv_w, conv_b, wa, ba, wx, bx, lam, pool_w, pool_scale, w_out, g1, b1, router_w, ts=64):
    nb, s, d = x.shape
    c = conv_w.shape[-1]
    n_exp = router_w.shape[-1]
    rw_hi = router_w.astype(BF16)
    rw_lo = (router_w - rw_hi.astype(F32)).astype(BF16)
    row = lambda p: p.reshape(1, -1)
    operands = (
        x, w_in.astype(BF16), conv_w, row(conv_b),
        _regroup_block_diag(wa, MXU_DIM).astype(BF16), row(ba),
        _regroup_block_diag(wx, MXU_DIM).astype(BF16), row(bx), row(lam),
        _regroup_block_diag(pool_w, MXU_DIM).astype(BF16), row(pool_scale),
        w_out.astype(BF16), row(g1), row(b1), rw_hi, rw_lo)
    tile = lambda width: pl.BlockSpec((nb, ts, width), lambda j: (0, j, 0))
    in_specs = [tile(d)] + [_const_spec(op.shape) for op in operands[1:]]
    return pl.pallas_call(
        _mixer_kernel,
        grid=(s // ts,),
        in_specs=in_specs,
        out_specs=[tile(d), tile(d // 2), tile(n_exp)],
        out_shape=[jax.ShapeDtypeStruct((nb, s, d), F32),
                   jax.ShapeDtypeStruct((nb, s, d // 2), jnp.uint32),
                   jax.ShapeDtypeStruct((nb, s, n_exp), F32)],
        scratch_shapes=[pltpu.VMEM((nb, CONV_HIST, c), F32),
                        pltpu.VMEM((nb, POOL_HIST, c), F32),
                        pltpu.VMEM((nb, c), F32)],
        compiler_params=pltpu.CompilerParams(dimension_semantics=("arbitrary",),
                                             vmem_limit_bytes=VMEM_LIMIT),
        name="mixer_ln_router",
    )(*operands)


def _first_row_of(cond, rows_rev, n):
    return n - jnp.max(jnp.where(cond, rows_rev, 0.0), axis=0, keepdims=True)


def _route_kernel(logit_ref, bias_ref, idx_ref, gate_ref, rank_ref, count_ref, count_scr):
    i = pl.program_id(0)

    @pl.when(i == 0)
    def _():
        count_scr[...] = jnp.zeros_like(count_scr)

    scores = jax.nn.sigmoid(logit_ref[...].T)
    n_exp, tn = scores.shape
    gsz = n_exp // N_EXPERT_GROUPS
    neg = -jnp.inf
    biased = scores + bias_ref[...]
    row = lax.broadcasted_iota(jnp.int32, (n_exp, tn), 0).astype(F32)
    row_rev = n_exp - row

    g_row = lax.broadcasted_iota(jnp.int32, (N_EXPERT_GROUPS, tn), 0)
    group_score = jnp.zeros((N_EXPERT_GROUPS, tn), F32)
    grp_row = lax.broadcasted_iota(jnp.int32, (gsz, tn), 0).astype(F32)
    grp_rev = gsz - grp_row
    for g in range(N_EXPERT_GROUPS):
        blk = biased[g * gsz:(g + 1) * gsz]
        m1 = jnp.max(blk, axis=0, keepdims=True)
        i1 = _first_row_of(blk == m1, grp_rev, gsz)
        m2 = jnp.max(jnp.where(grp_row == i1, neg, blk), axis=0, keepdims=True)
        group_score = jnp.where(g_row == g, m1 + m2, group_score)

    beaten_by = jnp.zeros((N_EXPERT_GROUPS, tn), F32)
    for g in range(N_EXPERT_GROUPS):
        other = jnp.max(jnp.where(g_row == g, group_score, neg), axis=0, keepdims=True)
        wins = (other > group_score) | ((other == group_score) & (g < g_row))
        beaten_by = beaten_by + jnp.where(wins, 1.0, 0.0)
    masked = []
    for g in range(N_EXPERT_GROUPS):
        beaten_g = jnp.max(jnp.where(g_row == g, beaten_by, 0.0), axis=0, keepdims=True)
        masked.append(jnp.where(beaten_g < TOPK_GROUPS, biased[g * gsz:(g + 1) * gsz], neg))
    masked = jnp.concatenate(masked, axis=0)

    k_row = lax.broadcasted_iota(jnp.int32, (TOP_K, tn), 0)
    selected = jnp.zeros((n_exp, tn), F32)
    idx = jnp.zeros((TOP_K, tn), F32)
    gate = jnp.zeros((TOP_K, tn), F32)
    hits = []
    for k in range(TOP_K):
        m = jnp.max(masked, axis=0, keepdims=True)
        ik = _first_row_of(masked == m, row_rev, n_exp)
        hit = row == ik
        gk = jnp.sum(jnp.where(hit, scores, 0.0), axis=0, keepdims=True)
        masked = jnp.where(hit, neg, masked)
        selected = jnp.where(hit, 1.0, selected)
        idx = jnp.where(k_row == k, ik, idx)
        gate = jnp.where(k_row == k, gk, gate)
        hits.append(hit)
    gate = gate / jnp.sum(gate, axis=0, keepdims=True) * ROUTED_SCALE

    earlier = (lax.broadcasted_iota(jnp.int32, (tn, tn), 0) < lax.broadcasted_iota(jnp.int32, (tn, tn), 1))
    before = _dot(selected.astype(BF16), jnp.where(earlier, 1.0, 0.0).astype(BF16)) + count_scr[...]
    rank = jnp.zeros((TOP_K, tn), F32)
    for k in range(TOP_K):
        rk = jnp.sum(jnp.where(hits[k], before, 0.0), axis=0, keepdims=True)
        rank = jnp.where(k_row == k, rk, rank)
    count_scr[...] = count_scr[...] + jnp.sum(selected, axis=1, keepdims=True)

    idx_ref[...] = idx.astype(jnp.int32)
    gate_ref[...] = gate
    rank_ref[...] = rank.astype(jnp.int32)
    count_ref[...] = count_scr[...].astype(jnp.int32)


def _route(logits, bias, tn=512):
    t, n_exp = logits.shape
    return pl.pallas_call(
        _route_kernel,
        grid=(t // tn,),
        in_specs=[pl.BlockSpec((tn, n_exp), lambda i: (i, 0)),
                  pl.BlockSpec((n_exp, 1), lambda i: (0, 0))],
        out_specs=[pl.BlockSpec((TOP_K, tn), lambda i: (0, i)),
                   pl.BlockSpec((TOP_K, tn), lambda i: (0, i)),
                   pl.BlockSpec((TOP_K, tn), lambda i: (0, i)),
                   pl.BlockSpec((n_exp, 1), lambda i: (0, 0))],
        out_shape=[jax.ShapeDtypeStruct((TOP_K, t), jnp.int32),
                   jax.ShapeDtypeStruct((TOP_K, t), F32),
                   jax.ShapeDtypeStruct((TOP_K, t), jnp.int32),
                   jax.ShapeDtypeStruct((n_exp, 1), jnp.int32)],
        scratch_shapes=[pltpu.VMEM((n_exp, 1), F32)],
        compiler_params=pltpu.CompilerParams(dimension_semantics=("arbitrary",),
                                             vmem_limit_bytes=VMEM_LIMIT),
        name="route_topk_rank",
    )(logits, bias.reshape(n_exp, 1))


def _slot_kernel(idx_ref, rank_ref, gate_ref, start_ref, slot_ref, gate_word_ref):
    idx = idx_ref[...]
    n_exp = start_ref.shape[0]
    tn = idx.shape[1]
    row = lax.broadcasted_iota(jnp.int32, (n_exp, tn), 0)
    start = start_ref[...]
    first = [jnp.sum(jnp.where(row == idx[k:k + 1], start, 0), axis=0, keepdims=True) for k in range(TOP_K)]
    slots = jnp.concatenate(first, axis=0) + rank_ref[...]
    bits = lax.bitcast_convert_type(gate_ref[...].astype(BF16).astype(F32), jnp.uint32)
    words = lax.bitcast_convert_type(bits | (bits >> 16), jnp.int32)
    chunk = slot_ref.shape[-1]
    for c in range(slot_ref.shape[0]):
        slot_ref[c] = slots[:, c * chunk:(c + 1) * chunk]
        gate_word_ref[c] = words[:, c * chunk:(c + 1) * chunk]


def _slots(idx, rank, gate, expert_start, tn=512):
    k, t = idx.shape
    n_exp = expert_start.shape[0]
    per_step = tn // SC_CHUNK
    tile = pl.BlockSpec((k, tn), lambda i: (0, i))
    chunked = pl.BlockSpec((per_step, k, SC_CHUNK), lambda i: (i, 0, 0))
    chunked_shape = jax.ShapeDtypeStruct((t // SC_CHUNK, k, SC_CHUNK), jnp.int32)
    return pl.pallas_call(
        _slot_kernel,
        grid=(t // tn,),
        in_specs=[tile, tile, tile, pl.BlockSpec((n_exp, 1), lambda i: (0, 0))],
        out_specs=[chunked, chunked],
        out_shape=[chunked_shape, chunked_shape],
        compiler_params=pltpu.CompilerParams(dimension_semantics=("arbitrary",)),
        name="dispatch_slots",
    )(idx, rank, gate, expert_start.reshape(n_exp, 1))


def _unpack_bf16_pairs(p):
    hi = lax.bitcast_convert_type(p & jnp.uint32(0xFFFF0000), F32)
    lo = lax.bitcast_convert_type(p << 16, F32)
    return jnp.concatenate([hi, lo], axis=1)


def _pack_bf16_pairs(y):
    n = y.shape[1] // 2
    bits = lax.bitcast_convert_type(y.astype(BF16).astype(F32), jnp.uint32)
    return bits[:, :n] | (bits[:, n:] >> 16)


def _sc_worker_layout(n_chunks_total):
    info = plsc.get_sparse_core_info()
    n_workers = info.num_cores * info.num_subcores
    return info.num_cores, n_chunks_total // n_workers


def _sc_dispatch(slots, x1p, n_rows):
    n_chunks_total, n_k, chunk = slots.shape
    t, w = x1p.shape
    n_cores, n_chunks = _sc_worker_layout(n_chunks_total)
    mesh = plsc.VectorSubcoreMesh(core_axis_name="c", subcore_axis_name="s")

    @functools.partial(
        pl.kernel, mesh=mesh, name="sc_dispatch_rows",
        out_type=jax.ShapeDtypeStruct((n_rows, w), x1p.dtype),
        scratch_types=[pltpu.VMEM((n_k, chunk), jnp.int32), pltpu.VMEM((chunk, w), x1p.dtype),
                       pltpu.SemaphoreType.DMA])
    def dispatch(slots_hbm, x_hbm, xg_hbm, idx_v, rows_v, sem):
        wid = lax.axis_index("s") * n_cores + lax.axis_index("c")

        @pl.loop(0, n_chunks)
        def _(ci):
            chunk_id = wid * n_chunks + ci
            pltpu.sync_copy(slots_hbm.at[chunk_id], idx_v)
            pltpu.sync_copy(x_hbm.at[pl.ds(chunk_id * chunk, chunk)], rows_v)
            copies = [pltpu.make_async_copy(rows_v, xg_hbm.at[idx_v.at[k]], sem) for k in range(n_k)]
            for cp in copies:
                cp.start()
            for cp in copies:
                cp.wait()

    return dispatch(slots, x1p)


def _sc_combine(slots, gate_words, yg):
    n_chunks_total, n_k, chunk = slots.shape
    w = yg.shape[1]
    t = n_chunks_total * chunk
    n_cores, n_chunks = _sc_worker_layout(n_chunks_total)
    lanes = plsc.get_sparse_core_info().num_lanes
    n_sub = chunk // SC_SUB
    mesh = plsc.VectorSubcoreMesh(core_axis_name="c", subcore_axis_name="s")
    rows_t = pltpu.VMEM((n_k, SC_SUB, w), jnp.int32)
    out_t = pltpu.VMEM((SC_SUB, w), jnp.int32)
    dma = pltpu.SemaphoreType.DMA

    @functools.partial(
        pl.kernel, mesh=mesh, name="sc_combine_rows",
        compiler_params=pltpu.CompilerParams(needs_layout_passes=False),
        out_type=jax.ShapeDtypeStruct((t, w), jnp.int32),
        scratch_types=[pltpu.VMEM((n_k, chunk), jnp.int32), pltpu.VMEM((n_k, chunk), jnp.int32),
                       rows_t, rows_t, out_t, out_t, dma, dma, dma, dma])
    def combine(slots_hbm, gate_hbm, yg_hbm, out_hbm, idx_v, gate_v, rows0, rows1, out0, out1,
                row_sem0, row_sem1, out_sem0, out_sem1):
        wid = lax.axis_index("s") * n_cores + lax.axis_index("c")
        rows, outs = (rows0, rows1), (out0, out1)
        row_sems, out_sems = (row_sem0, row_sem1), (out_sem0, out_sem1)

        def gathers(sub, b):
            return [pltpu.make_async_copy(yg_hbm.at[idx_v.at[k, pl.ds(sub * SC_SUB, SC_SUB)]], rows[b].at[k],
                                          row_sems[b]) for k in range(n_k)]

        @pl.loop(0, n_chunks)
        def _(ci):
            chunk_id = wid * n_chunks + ci
            tok0 = chunk_id * chunk
            pltpu.sync_copy(slots_hbm.at[chunk_id], idx_v)
            pltpu.sync_copy(gate_hbm.at[chunk_id], gate_v)

            def store(sub, b):
                return pltpu.make_async_copy(outs[b], out_hbm.at[pl.ds(tok0 + sub * SC_SUB, SC_SUB)], out_sems[b])

            for cp in gathers(0, 0):
                cp.start()

            @pl.loop(0, n_sub // 2)
            def _(pair):
                for b in (0, 1):
                    sub = 2 * pair + b

                    @pl.when(sub + 1 < n_sub)
                    def _():
                        for cp in gathers(sub + 1, 1 - b):
                            cp.start()

                    for cp in gathers(sub, b):
                        cp.wait()

                    @pl.when(sub >= 2)
                    def _():
                        store(sub - 2, b).wait()

                    buf, out = rows[b], outs[b]

                    @pl.loop(0, SC_SUB)
                    def _(tt):
                        col = jnp.full((lanes,), sub * SC_SUB + tt, jnp.int32)
                        gates = [plsc.bitcast(plsc.load_gather(gate_v, [jnp.full((lanes,), k, jnp.int32), col]), BF16)
                                 for k in range(n_k)]
                        for v in range(w // lanes):
                            terms = [gates[k] * plsc.bitcast(buf[k, tt, pl.ds(v * lanes, lanes)], BF16)
                                     for k in range(n_k)]
                            while len(terms) > 1:
                                terms = [terms[i] + terms[i + 1] for i in range(0, len(terms), 2)]
                            out[tt, pl.ds(v * lanes, lanes)] = plsc.bitcast(terms[0], jnp.int32)

                    store(sub, b).start()

            store(n_sub - 2, 0).wait()
            store(n_sub - 1, 1).wait()

    return combine(slots, gate_words, yg)


def _expert_kernel(first_block, n_block, n_used, xg_hbm, w1_hbm, w3_hbm, w2_hbm, yg_hbm,
                   x_buf, y_buf, w1_f32, w3_f32, w2_f32, w1_bf, w3_bf, w2_bf, in_sem, out_sem, w_sem):
    e = pl.program_id(0)
    n_exp = pl.num_programs(0)
    n_slot, bm, _ = x_buf.shape
    ahead = n_slot - BLOCK_GROUP
    total = n_used[0]

    def fetch(g):
        slot = g % n_slot
        return pltpu.make_async_copy(xg_hbm.at[pl.ds(g * bm, bm)], x_buf.at[slot], in_sem.at[slot])

    def write_back(g):
        slot = g % n_slot
        return pltpu.make_async_copy(y_buf.at[slot], yg_hbm.at[pl.ds(g * bm, bm)], out_sem.at[slot])

    def weight_fetches(ex):
        slot = ex % WEIGHT_BUFFERS
        copies = []
        for i, (hbm, buf) in enumerate(((w1_hbm, w1_f32), (w3_hbm, w3_f32), (w2_hbm, w2_f32))):
            rows = buf.shape[1] // WEIGHT_CHUNKS
            for c in range(WEIGHT_CHUNKS):
                part = pl.ds(c * rows, rows)
                copies.append(pltpu.make_async_copy(hbm.at[ex, part], buf.at[slot, part], w_sem.at[i, slot]))
        return copies

    @pl.when(e == 0)
    def _():
        for g in range(ahead):
            @pl.when(g < total)
            def _():
                fetch(g).start()
        for ex in range(WEIGHT_BUFFERS - 1):
            for cp in weight_fetches(ex):
                cp.start()

    @pl.when(e + WEIGHT_BUFFERS - 1 < n_exp)
    def _():
        for cp in weight_fetches(e + WEIGHT_BUFFERS - 1):
            cp.start()

    for cp in weight_fetches(e):
        cp.wait()
    w_slot = e % WEIGHT_BUFFERS

    def cast_weights():
        ws = [w1_f32[w_slot].astype(BF16), w3_f32[w_slot].astype(BF16), w2_f32[w_slot].astype(BF16)]
        for ref, wb in zip((w1_bf, w3_bf, w2_bf), ws):
            ref[...] = wb
        return ws

    def swiglu(slots, weights):
        w1b, w3b, w2b = weights
        xs = [_unpack_bf16_pairs(x_buf[s]).astype(BF16) for s in slots]
        up = [(_dot(xb, w1b), _dot(xb, w3b)) for xb in xs]
        hs = [(h1 * jax.nn.sigmoid(h1) * h3).astype(BF16) for h1, h3 in up]
        ys = [_dot(h, w2b) for h in hs]
        return [lax.bitcast_convert_type(_pack_bf16_pairs(y), y_buf.dtype) for y in ys]

    def process(blocks, cast_here=False):
        for g in blocks:
            fetch(g).wait()
        for g in blocks:
            @pl.when(g + ahead < total)
            def _():
                fetch(g + ahead).start()
        weights = cast_weights() if cast_here else (w1_bf[...], w3_bf[...], w2_bf[...])
        ys = swiglu([g % n_slot for g in blocks], weights)
        for g in blocks:
            @pl.when(g >= n_slot)
            def _():
                write_back(g - n_slot).wait()
        for g, y in zip(blocks, ys):
            y_buf[g % n_slot] = y
            write_back(g).start()

    n_full = n_block[e] // BLOCK_GROUP

    @pl.when(n_full > 0)
    def _():
        process([first_block[e] + j for j in range(BLOCK_GROUP)], cast_here=True)

    @pl.when((n_full == 0) & (n_block[e] > 0))
    def _():
        cast_weights()

    @pl.loop(1, n_full)
    def _(i):
        process([first_block[e] + i * BLOCK_GROUP + j for j in range(BLOCK_GROUP)])

    done = n_full * BLOCK_GROUP
    size = BLOCK_GROUP // 2
    while size >= 1:
        left = n_block[e] - done
        start = first_block[e] + done

        @pl.when(left >= size)
        def _(start=start, size=size):
            process([start + j for j in range(size)])

        done = done + left // size * size
        size //= 2

    @pl.when(e == n_exp - 1)
    def _():
        for back in range(1, n_slot + 1):
            @pl.when(total >= back)
            def _():
                write_back(total - back).wait()


def _experts(first_block, n_block, n_used, xg, w1, w3, w2, bm):
    n_rows, w = xg.shape
    n_exp, d, de = w1.shape
    assert n_exp >= WEIGHT_BUFFERS
    hbm = pl.BlockSpec(memory_space=pl.ANY)
    return pl.pallas_call(
        _expert_kernel,
        grid_spec=pltpu.PrefetchScalarGridSpec(
            num_scalar_prefetch=3,
            grid=(n_exp,),
            in_specs=[hbm, hbm, hbm, hbm],
            out_specs=hbm,
            scratch_shapes=[pltpu.VMEM((ROW_RING, bm, w), xg.dtype), pltpu.VMEM((ROW_RING, bm, w), jnp.int32),
                            pltpu.VMEM((WEIGHT_BUFFERS, d, de), F32), pltpu.VMEM((WEIGHT_BUFFERS, d, de), F32),
                            pltpu.VMEM((WEIGHT_BUFFERS, de, d), F32),
                            pltpu.VMEM((d, de), BF16), pltpu.VMEM((d, de), BF16), pltpu.VMEM((de, d), BF16),
                            pltpu.SemaphoreType.DMA((ROW_RING,)), pltpu.SemaphoreType.DMA((ROW_RING,)),
                            pltpu.SemaphoreType.DMA((3, WEIGHT_BUFFERS))]),
        out_shape=jax.ShapeDtypeStruct((n_rows, w), jnp.int32),
        compiler_params=pltpu.CompilerParams(dimension_semantics=("arbitrary",),
                                             vmem_limit_bytes=VMEM_LIMIT),
        name="routed_experts",
    )(first_block, n_block, n_used, xg, w1, w3, w2)


def _combine_kernel(y_ref, x1_ref, sw1_ref, sw3_ref, sw2_ref, g2_ref, b2_ref, out_ref):
    x1 = x1_ref[...]
    xb = x1.astype(BF16)
    h1 = _dot(xb, sw1_ref[...])
    h3 = _dot(xb, sw3_ref[...])
    ffn = _dot((h1 * jax.nn.sigmoid(h1) * h3).astype(BF16), sw2_ref[...])
    ffn = ffn + _unpack_bf16_pairs(lax.bitcast_convert_type(y_ref[...], jnp.uint32))
    out_ref[...] = _layer_norm(DEEPNORM_ALPHA * x1 + ffn, g2_ref[...], b2_ref[...])


def _combine(y_routed, x1, sw1, sw3, sw2, g2, b2, tn=512):
    t, d = x1.shape
    w = y_routed.shape[1]
    row = lambda p: p.reshape(1, -1)
    operands = (y_routed, x1, sw1.astype(BF16), sw3.astype(BF16), sw2.astype(BF16), row(g2), row(b2))
    in_specs = [pl.BlockSpec((tn, w), lambda i: (i, 0)),
                pl.BlockSpec((tn, d), lambda i: (i, 0))]
    in_specs += [_const_spec(op.shape) for op in operands[2:]]
    return pl.pallas_call(
        _combine_kernel,
        grid=(t // tn,),
        in_specs=in_specs,
        out_specs=pl.BlockSpec((tn, d), lambda i: (i, 0)),
        out_shape=jax.ShapeDtypeStruct((t, d), F32),
        compiler_params=pltpu.CompilerParams(dimension_semantics=("arbitrary",),
                                             vmem_limit_bytes=VMEM_LIMIT),
        name="combine_shared_ln",
    )(*operands)


def _block_table(counts, bm):
    n_block = (counts + bm - 1) // bm
    ends = jnp.cumsum(n_block)
    first_block = ends - n_block
    i32 = lambda v: v.astype(jnp.int32)
    return i32(first_block * bm), i32(first_block), i32(n_block), i32(ends[-1:])


def _moe(x1, x1p, logits, router_bias, w1, w3, w2, sw1, sw3, sw2, g2, b2, bm=256):
    t, d = x1.shape
    n_exp = logits.shape[1]
    idx, gate, rank, counts = _route(logits, router_bias)
    n_blocks = (t * TOP_K) // bm + n_exp
    starts, first_block, n_block, n_used = _block_table(counts[:, 0], bm)
    slots, gate_words = _slots(idx, rank, gate, starts)
    xg = _sc_dispatch(slots, x1p, n_blocks * bm)
    yg = _experts(first_block, n_block, n_used, xg, w1, w3, w2, bm)
    y_routed = _sc_combine(slots, gate_words, yg)
    return _combine(y_routed, x1, sw1, sw3, sw2, g2, b2)


def kernel(x, w_in, conv_w, conv_b, lru_wa, lru_ba, lru_wx, lru_bx, lru_lambda, pool_w, pool_scale, w_out, ln1_g, ln1_b, router_w, router_bias, exp_w1, exp_w3, exp_w2, sh_w1, sh_w3, sh_w2, ln2_g, ln2_b):
    nb, s, d = x.shape
    for l in range(DEPTH):
        x1, x1p, logits = _mixer(x, w_in[l], conv_w[l], conv_b[l], lru_wa[l], lru_ba[l], lru_wx[l], lru_bx[l],
                                 lru_lambda[l], pool_w[l], pool_scale[l], w_out[l], ln1_g[l], ln1_b[l],
                                 router_w[l])
        t = nb * s
        x = _moe(x1.reshape(t, d), x1p.reshape(t, d // 2), logits.reshape(t, -1), router_bias[l],
                 exp_w1[l], exp_w3[l], exp_w2[l], sh_w1[l], sh_w3[l], sh_w2[l], ln2_g[l], ln2_b[l])
        x = x.reshape(nb, s, d)
    return x
```

```python
import functools
import math

import jax
import jax.numpy as jnp
from jax import lax
from jax.experimental import pallas as pl
from jax.experimental.pallas import tpu as pltpu
from jax.experimental.pallas import tpu_sc as plsc

LRU_HEADS = 8
CONV_WIDTH = 4
LRU_C = 8.0
POOL_WINDOWS = (2, 4, 8, 16)
N_EXPERT_GROUPS = 8
TOPK_GROUPS = 4
TOP_K = 8
ROUTED_SCALE = 2.5
LN_EPS = 1e-5
DEPTH = 1
DEEPNORM_ALPHA = (2.0 * DEPTH) ** 0.25

MXU_DIM = 256
POOL_GROUP_DIM = 128
CONV_HIST = 8
POOL_HIST = 16
SC_CHUNK = 128
SC_SUB = 8
MIXER_PARTS = 4
BLOCK_GROUP = 8
ROW_RING = 16
VMEM_LIMIT = 56 * 1024 * 1024

F32 = jnp.float32
BF16 = jnp.bfloat16


def _dot(a, b):
    return jnp.dot(a, b, preferred_element_type=F32)


def _layer_norm(z, g, b):
    mu = jnp.mean(z, axis=-1, keepdims=True)
    zc = z - mu
    var = jnp.mean(zc * zc, axis=-1, keepdims=True)
    return zc * lax.rsqrt(var + LN_EPS) * g + b


def _block_diag_dot(xb, w_ref):
    n = w_ref.shape[0]
    return jnp.concatenate(
        [_dot(xb[:, i * MXU_DIM:(i + 1) * MXU_DIM], w_ref[i]) for i in range(n)], axis=1)


def _causal_conv(u, hist_ref, part, conv_w, conv_b):
    pb, ts, c = u.shape
    ext = jnp.concatenate([hist_ref[part], u], axis=1)
    hist_ref[part] = u[:, ts - CONV_HIST:, :]
    uc = jnp.broadcast_to(conv_b.reshape(1, 1, c), (pb, ts, c))
    for k in range(CONV_WIDTH):
        off = CONV_HIST - (CONV_WIDTH - 1) + k
        uc = uc + conv_w[k:k + 1, :].reshape(1, 1, c) * ext[:, off:off + ts, :]
    return uc


def _lru_scan(uc, ga, gx, lam, carry_ref, part, t0):
    pb, ts, c = uc.shape
    r = jax.nn.sigmoid(ga)
    i_gate = jax.nn.sigmoid(gx)
    softplus_neg_lam = jnp.maximum(-lam, 0.0) + jnp.log1p(jnp.exp(-jnp.abs(lam)))
    log_a = (-LRU_C) * r * softplus_neg_lam
    a = jnp.exp(log_a)
    mult = jnp.sqrt(jnp.tanh(-log_a) * (1.0 + a * a))
    t_local = lax.broadcasted_iota(jnp.int32, (pb, ts, c), 1)
    first = (t_local + t0) == 0
    a3 = a.reshape(pb, ts, c)
    x3 = jnp.where(first, 1.0, mult.reshape(pb, ts, c)) * i_gate.reshape(pb, ts, c) * uc

    shift = 1
    while shift < ts:
        valid = t_local >= shift
        a_prev = pltpu.roll(a3, shift, axis=1)
        x_prev = pltpu.roll(x3, shift, axis=1)
        x3 = jnp.where(valid, a3 * x_prev, 0.0) + x3
        a3 = jnp.where(valid, a3 * a_prev, a3)
        shift *= 2
    h = x3 + a3 * carry_ref[part][:, None, :]
    carry_ref[part] = h[:, ts - 1, :]
    return h


def _multiscale_pool(v, hist_ref, part, t0):
    pb, ts, c = v.shape
    ext = jnp.concatenate([hist_ref[part], v], axis=1)
    hist_ref[part] = v[:, ts - POOL_HIST:, :]
    t_glob = lax.broadcasted_iota(jnp.int32, (1, ts, POOL_GROUP_DIM), 1) + t0
    pooled = []
    for g, win in enumerate(POOL_WINDOWS):
        lanes = slice(g * POOL_GROUP_DIM, (g + 1) * POOL_GROUP_DIM)
        s = ext[:, :, lanes]
        shift = 1
        while shift < win:
            s = s + pltpu.roll(s, shift, axis=1)
            shift *= 2
        count = jnp.minimum(t_glob + 1, win).astype(F32)
        pooled.append(s[:, POOL_HIST:, :] / count - v[:, :, lanes])
    return jnp.concatenate(pooled, axis=-1)


def _mixer_kernel(x_ref, w_in_ref, conv_w_ref, conv_b_ref, wa_ref, ba_ref, wx_ref, bx_ref, lam_ref,
                  pool_w_ref, pool_scale_ref, w_out_ref, g1_ref, b1_ref, rw_hi_ref, rw_lo_ref,
                  x1_ref, x1p_ref, logit_ref, hist_u, hist_v, carry):
    j = pl.program_id(0)
    nb, ts, d = x_ref.shape
    c = hist_u.shape[-1]
    pb = nb // MIXER_PARTS
    rows = pb * ts
    t0 = j * ts
    parts = [slice(p * pb, (p + 1) * pb) for p in range(MIXER_PARTS)]

    @pl.when(j == 0)
    def _():
        hist_u[...] = jnp.zeros_like(hist_u)
        hist_v[...] = jnp.zeros_like(hist_v)
        carry[...] = jnp.zeros_like(carry)

    xs = [x_ref[p].reshape(rows, d) for p in parts]
    projs = [_dot(x.astype(BF16), w_in_ref[...]) for x in xs]
    ucs = [_causal_conv(proj[:, :c].reshape(pb, ts, c), hist_u, p, conv_w_ref[...], conv_b_ref[...])
           for proj, p in zip(projs, parts)]
    ucbs = [uc.reshape(rows, c).astype(BF16) for uc in ucs]
    gas = [_block_diag_dot(ucb, wa_ref) + ba_ref[...] for ucb in ucbs]
    gxs = [_block_diag_dot(ucb, wx_ref) + bx_ref[...] for ucb in ucbs]
    hs = [_lru_scan(uc, ga, gx, lam_ref[...], carry, p, t0) for uc, ga, gx, p in zip(ucs, gas, gxs, parts)]
    y_lrus = []
    for h, proj in zip(hs, projs):
        gate = proj[:, c:2 * c]
        gelu_gate = 0.5 * gate * (1.0 + jnp.tanh(math.sqrt(2.0 / math.pi) * (gate + 0.044715 * gate * gate * gate)))
        y_lrus.append(h.reshape(rows, c) * gelu_gate)
    pooled = [_multiscale_pool(proj[:, 2 * c:].reshape(pb, ts, c), hist_v, p, t0).reshape(rows, c).astype(BF16)
              for proj, p in zip(projs, parts)]
    y_pools = [_block_diag_dot(pl_, pool_w_ref) * pool_scale_ref[...] for pl_ in pooled]
    mixes = [_dot(jnp.concatenate([y_lru, y_pool], axis=1).astype(BF16), w_out_ref[...])
             for y_lru, y_pool in zip(y_lrus, y_pools)]
    x1s = [_layer_norm(DEEPNORM_ALPHA * x + mix, g1_ref[...], b1_ref[...]) for x, mix in zip(xs, mixes)]

    for p, x1 in zip(parts, x1s):
        x1_ref[p] = x1.reshape(pb, ts, d)
        hi = x1.astype(BF16)
        hi_f = hi.astype(F32)
        lo = (x1 - hi_f).astype(BF16)
        logits = _dot(hi, rw_hi_ref[...]) + _dot(lo, rw_hi_ref[...]) + _dot(hi, rw_lo_ref[...])
        logit_ref[p] = logits.reshape(pb, ts, logits.shape[-1])
        bits = lax.bitcast_convert_type(hi_f, jnp.uint32)
        packed = bits[:, :d // 2] | (bits[:, d // 2:] >> 16)
        x1p_ref[p] = packed.reshape(pb, ts, d // 2)


def _const_spec(shape):
    return pl.BlockSpec(shape, lambda j: (0,) * len(shape), pipeline_mode=pl.Buffered(1))


def _regroup_block_diag(w, width):
    heads, hd, _ = w.shape
    per = width // hd
    w = w.reshape(heads // per, per, hd, hd)
    eye = jnp.eye(per, dtype=w.dtype)
    return jnp.einsum("gpij,pq->gpiqj", w, eye).reshape(heads // per, width, width)


def _mixer(x, w_in, conv_w, conv_b, wa, ba, wx, bx, lam, pool_w, pool_scale, w_out, g1, b1, router_w, ts=64):
    nb, s, d = x.shape
    c = conv_w.shape[-1]
    n_exp = router_w.shape[-1]
    rw_hi = router_w.astype(BF16)
    rw_lo = (router_w - rw_hi.astype(F32)).astype(BF16)
    row = lambda p: p.reshape(1, -1)
    operands = (
        x, w_in.astype(BF16), conv_w, row(conv_b),
        _regroup_block_diag(wa, MXU_DIM).astype(BF16), row(ba),
        _regroup_block_diag(wx, MXU_DIM).astype(BF16), row(bx), row(lam),
        _regroup_block_diag(pool_w, MXU_DIM).astype(BF16), row(pool_scale),
        w_out.astype(BF16), row(g1), row(b1), rw_hi, rw_lo)
    tile = lambda width: pl.BlockSpec((nb, ts, width), lambda j: (0, j, 0))
    in_specs = [tile(d)] + [_const_spec(op.shape) for op in operands[1:]]
    return pl.pallas_call(
        _mixer_kernel,
        grid=(s // ts,),
        in_specs=in_specs,
        out_specs=[tile(d), tile(d // 2), tile(n_exp)],
        out_shape=[jax.ShapeDtypeStruct((nb, s, d), F32),
                   jax.ShapeDtypeStruct((nb, s, d // 2), jnp.uint32),
                   jax.ShapeDtypeStruct((nb, s, n_exp), F32)],
        scratch_shapes=[pltpu.VMEM((nb, CONV_HIST, c), F32),
                        pltpu.VMEM((nb, POOL_HIST, c), F32),
                        pltpu.VMEM((nb, c), F32)],
        compiler_params=pltpu.CompilerParams(dimension_semantics=("arbitrary",),
                                             vmem_limit_bytes=VMEM_LIMIT),
        name="mixer_ln_router",
    )(*operands)


def _first_row_of(cond, rows_rev, n):
    return n - jnp.max(jnp.where(cond, rows_rev, 0.0), axis=0, keepdims=True)


def _route_kernel(logit_ref, bias_ref, idx_ref, gate_ref, rank_ref, count_ref, count_scr):
    i = pl.program_id(0)

    @pl.when(i == 0)
    def _():
        count_scr[...] = jnp.zeros_like(count_scr)

    scores = jax.nn.sigmoid(logit_ref[...].T)
    n_exp, tn = scores.shape
    gsz = n_exp // N_EXPERT_GROUPS
    neg = -jnp.inf
    biased = scores + bias_ref[...]
    row = lax.broadcasted_iota(jnp.int32, (n_exp, tn), 0).astype(F32)
    row_rev = n_exp - row

    g_row = lax.broadcasted_iota(jnp.int32, (N_EXPERT_GROUPS, tn), 0)
    group_score = jnp.zeros((N_EXPERT_GROUPS, tn), F32)
    grp_row = lax.broadcasted_iota(jnp.int32, (gsz, tn), 0).astype(F32)
    grp_rev = gsz - grp_row
    for g in range(N_EXPERT_GROUPS):
        blk = biased[g * gsz:(g + 1) * gsz]
        m1 = jnp.max(blk, axis=0, keepdims=True)
        i1 = _first_row_of(blk == m1, grp_rev, gsz)
        m2 = jnp.max(jnp.where(grp_row == i1, neg, blk), axis=0, keepdims=True)
        group_score = jnp.where(g_row == g, m1 + m2, group_score)

    beaten_by = jnp.zeros((N_EXPERT_GROUPS, tn), F32)
    for g in range(N_EXPERT_GROUPS):
        other = jnp.max(jnp.where(g_row == g, group_score, neg), axis=0, keepdims=True)
        wins = (other > group_score) | ((other == group_score) & (g < g_row))
        beaten_by = beaten_by + jnp.where(wins, 1.0, 0.0)
    masked = []
    for g in range(N_EXPERT_GROUPS):
        beaten_g = jnp.max(jnp.where(g_row == g, beaten_by, 0.0), axis=0, keepdims=True)
        masked.append(jnp.where(beaten_g < TOPK_GROUPS, biased[g * gsz:(g + 1) * gsz], neg))
    masked = jnp.concatenate(masked, axis=0)

    k_row = lax.broadcasted_iota(jnp.int32, (TOP_K, tn), 0)
    selected = jnp.zeros((n_exp, tn), F32)
    idx = jnp.zeros((TOP_K, tn), F32)
    gate = jnp.zeros((TOP_K, tn), F32)
    hits = []
    for k in range(TOP_K):
        m = jnp.max(masked, axis=0, keepdims=True)
        ik = _first_row_of(masked == m, row_rev, n_exp)
        hit = row == ik
        gk = jnp.sum(jnp.where(hit, scores, 0.0), axis=0, keepdims=True)
        masked = jnp.where(hit, neg, masked)
        selected = jnp.where(hit, 1.0, selected)
        idx = jnp.where(k_row == k, ik, idx)
        gate = jnp.where(k_row == k, gk, gate)
        hits.append(hit)
    gate = gate / jnp.sum(gate, axis=0, keepdims=True) * ROUTED_SCALE

    earlier = (lax.broadcasted_iota(jnp.int32, (tn, tn), 0) < lax.broadcasted_iota(jnp.int32, (tn, tn), 1))
    before = _dot(selected.astype(BF16), jnp.where(earlier, 1.0, 0.0).astype(BF16)) + count_scr[...]
    rank = jnp.zeros((TOP_K, tn), F32)
    for k in range(TOP_K):
        rk = jnp.sum(jnp.where(hits[k], before, 0.0), axis=0, keepdims=True)
        rank = jnp.where(k_row == k, rk, rank)
    count_scr[...] = count_scr[...] + jnp.sum(selected, axis=1, keepdims=True)

    idx_ref[...] = idx.astype(jnp.int32)
    gate_ref[...] = gate
    rank_ref[...] = rank.astype(jnp.int32)
    count_ref[...] = count_scr[...].astype(jnp.int32)


def _route(logits, bias, tn=512):
    t, n_exp = logits.shape
    return pl.pallas_call(
        _route_kernel,
        grid=(t // tn,),
        in_specs=[pl.BlockSpec((tn, n_exp), lambda i: (i, 0)),
                  pl.BlockSpec((n_exp, 1), lambda i: (0, 0))],
        out_specs=[pl.BlockSpec((TOP_K, tn), lambda i: (0, i)),
                   pl.BlockSpec((TOP_K, tn), lambda i: (0, i)),
                   pl.BlockSpec((TOP_K, tn), lambda i: (0, i)),
                   pl.BlockSpec((n_exp, 1), lambda i: (0, 0))],
        out_shape=[jax.ShapeDtypeStruct((TOP_K, t), jnp.int32),
                   jax.ShapeDtypeStruct((TOP_K, t), F32),
                   jax.ShapeDtypeStruct((TOP_K, t), jnp.int32),
                   jax.ShapeDtypeStruct((n_exp, 1), jnp.int32)],
        scratch_shapes=[pltpu.VMEM((n_exp, 1), F32)],
        compiler_params=pltpu.CompilerParams(dimension_semantics=("arbitrary",),
                                             vmem_limit_bytes=VMEM_LIMIT),
        name="route_topk_rank",
    )(logits, bias.reshape(n_exp, 1))


def _slot_kernel(idx_ref, rank_ref, gate_ref, start_ref, slot_ref, gate_word_ref):
    idx = idx_ref[...]
    n_exp = start_ref.shape[0]
    tn = idx.shape[1]
    row = lax.broadcasted_iota(jnp.int32, (n_exp, tn), 0)
    start = start_ref[...]
    first = [jnp.sum(jnp.where(row == idx[k:k + 1], start, 0), axis=0, keepdims=True) for k in range(TOP_K)]
    slots = jnp.concatenate(first, axis=0) + rank_ref[...]
    bits = lax.bitcast_convert_type(gate_ref[...].astype(BF16).astype(F32), jnp.uint32)
    words = lax.bitcast_convert_type(bits | (bits >> 16), jnp.int32)
    chunk = slot_ref.shape[-1]
    for c in range(slot_ref.shape[0]):
        slot_ref[c] = slots[:, c * chunk:(c + 1) * chunk]
        gate_word_ref[c] = words[:, c * chunk:(c + 1) * chunk]


def _slots(idx, rank, gate, expert_start, tn=512):
    k, t = idx.shape
    n_exp = expert_start.shape[0]
    per_step = tn // SC_CHUNK
    tile = pl.BlockSpec((k, tn), lambda i: (0, i))
    chunked = pl.BlockSpec((per_step, k, SC_CHUNK), lambda i: (i, 0, 0))
    chunked_shape = jax.ShapeDtypeStruct((t // SC_CHUNK, k, SC_CHUNK), jnp.int32)
    return pl.pallas_call(
        _slot_kernel,
        grid=(t // tn,),
        in_specs=[tile, tile, tile, pl.BlockSpec((n_exp, 1), lambda i: (0, 0))],
        out_specs=[chunked, chunked],
        out_shape=[chunked_shape, chunked_shape],
        compiler_params=pltpu.CompilerParams(dimension_semantics=("arbitrary",)),
        name="dispatch_slots",
    )(idx, rank, gate, expert_start.reshape(n_exp, 1))


def _unpack_bf16_pairs(p):
    hi = lax.bitcast_convert_type(p & jnp.uint32(0xFFFF0000), F32)
    lo = lax.bitcast_convert_type(p << 16, F32)
    return jnp.concatenate([hi, lo], axis=1)


def _pack_bf16_pairs(y):
    n = y.shape[1] // 2
    bits = lax.bitcast_convert_type(y.astype(BF16).astype(F32), jnp.uint32)
    return bits[:, :n] | (bits[:, n:] >> 16)


def _sc_worker_layout(n_chunks_total):
    info = plsc.get_sparse_core_info()
    n_workers = info.num_cores * info.num_subcores
    return info.num_cores, n_chunks_total // n_workers


def _sc_dispatch(slots, x1p, n_rows):
    n_chunks_total, n_k, chunk = slots.shape
    t, w = x1p.shape
    n_cores, n_chunks = _sc_worker_layout(n_chunks_total)
    mesh = plsc.VectorSubcoreMesh(core_axis_name="c", subcore_axis_name="s")

    @functools.partial(
        pl.kernel, mesh=mesh, name="sc_dispatch_rows",
        out_type=jax.ShapeDtypeStruct((n_rows, w), x1p.dtype),
        scratch_types=[pltpu.VMEM((n_k, chunk), jnp.int32), pltpu.VMEM((chunk, w), x1p.dtype),
                       pltpu.SemaphoreType.DMA])
    def dispatch(slots_hbm, x_hbm, xg_hbm, idx_v, rows_v, sem):
        wid = lax.axis_index("s") * n_cores + lax.axis_index("c")

        @pl.loop(0, n_chunks)
        def _(ci):
            chunk_id = wid * n_chunks + ci
            pltpu.sync_copy(slots_hbm.at[chunk_id], idx_v)
            pltpu.sync_copy(x_hbm.at[pl.ds(chunk_id * chunk, chunk)], rows_v)
            copies = [pltpu.make_async_copy(rows_v, xg_hbm.at[idx_v.at[k]], sem) for k in range(n_k)]
            for cp in copies:
                cp.start()
            for cp in copies:
                cp.wait()

    return dispatch(slots, x1p)


def _sc_combine(slots, gate_words, yg):
    n_chunks_total, n_k, chunk = slots.shape
    w = yg.shape[1]
    t = n_chunks_total * chunk
    n_cores, n_chunks = _sc_worker_layout(n_chunks_total)
    lanes = plsc.get_sparse_core_info().num_lanes
    n_sub = chunk // SC_SUB
    mesh = plsc.VectorSubcoreMesh(core_axis_name="c", subcore_axis_name="s")
    rows_t = pltpu.VMEM((n_k, SC_SUB, w), jnp.int32)
    out_t = pltpu.VMEM((SC_SUB, w), jnp.int32)
    dma = pltpu.SemaphoreType.DMA

    @functools.partial(
        pl.kernel, mesh=mesh, name="sc_combine_rows",
        compiler_params=pltpu.CompilerParams(needs_layout_passes=False),
        out_type=jax.ShapeDtypeStruct((t, w), jnp.int32),
        scratch_types=[pltpu.VMEM((n_k, chunk), jnp.int32), pltpu.VMEM((n_k, chunk), jnp.int32),
                       rows_t, rows_t, out_t, out_t, dma, dma, dma, dma])
    def combine(slots_hbm, gate_hbm, yg_hbm, out_hbm, idx_v, gate_v, rows0, rows1, out0, out1,
                row_sem0, row_sem1, out_sem0, out_sem1):
        wid = lax.axis_index("s") * n_cores + lax.axis_index("c")
        rows, outs = (rows0, rows1), (out0, out1)
        row_sems, out_sems = (row_sem0, row_sem1), (out_sem0, out_sem1)

        def gathers(sub, b):
            return [pltpu.make_async_copy(yg_hbm.at[idx_v.at[k, pl.ds(sub * SC_SUB, SC_SUB)]], rows[b].at[k],
                                          row_sems[b]) for k in range(n_k)]

        @pl.loop(0, n_chunks)
        def _(ci):
            chunk_id = wid * n_chunks + ci
            tok0 = chunk_id * chunk
            pltpu.sync_copy(slots_hbm.at[chunk_id], idx_v)
            pltpu.sync_copy(gate_hbm.at[chunk_id], gate_v)

            def store(sub, b):
                return pltpu.make_async_copy(outs[b], out_hbm.at[pl.ds(tok0 + sub * SC_SUB, SC_SUB)], out_sems[b])

            for cp in gathers(0, 0):
                cp.start()

            @pl.loop(0, n_sub // 2)
            def _(pair):
                for b in (0, 1):
                    sub = 2 * pair + b

                    @pl.when(sub + 1 < n_sub)
                    def _():
                        for cp in gathers(sub + 1, 1 - b):
                            cp.start()

                    for cp in gathers(sub, b):
                        cp.wait()

                    @pl.when(sub >= 2)
                    def _():
                        store(sub - 2, b).wait()

                    buf, out = rows[b], outs[b]

                    @pl.loop(0, SC_SUB)
                    def _(tt):
                        col = jnp.full((lanes,), sub * SC_SUB + tt, jnp.int32)
                        gates = [plsc.bitcast(plsc.load_gather(gate_v, [jnp.full((lanes,), k, jnp.int32), col]), BF16)
                                 for k in range(n_k)]
                        for v in range(w // lanes):
                            terms = [gates[k] * plsc.bitcast(buf[k, tt, pl.ds(v * lanes, lanes)], BF16)
                                     for k in range(n_k)]
                            while len(terms) > 1:
                                terms = [terms[i] + terms[i + 1] for i in range(0, len(terms), 2)]
                            out[tt, pl.ds(v * lanes, lanes)] = plsc.bitcast(terms[0], jnp.int32)

                    store(sub, b).start()

            store(n_sub - 2, 0).wait()
            store(n_sub - 1, 1).wait()

    return combine(slots, gate_words, yg)


def _expert_kernel(first_block, n_block, n_used, xg_hbm, w1_ref, w3_ref, w2_ref, yg_hbm,
                   x_buf, y_buf, w1_bf, w3_bf, w2_bf, in_sem, out_sem):
    e = pl.program_id(0)
    n_slot, bm, _ = x_buf.shape
    ahead = n_slot - BLOCK_GROUP
    total = n_used[0]

    def fetch(g):
        slot = g % n_slot
        return pltpu.make_async_copy(xg_hbm.at[pl.ds(g * bm, bm)], x_buf.at[slot], in_sem.at[slot])

    def write_back(g):
        slot = g % n_slot
        return pltpu.make_async_copy(y_buf.at[slot], yg_hbm.at[pl.ds(g * bm, bm)], out_sem.at[slot])

    @pl.when(e == 0)
    def _():
        for g in range(ahead):
            @pl.when(g < total)
            def _():
                fetch(g).start()

    def cast_weights():
        ws = [w1_ref[0].astype(BF16), w3_ref[0].astype(BF16), w2_ref[0].astype(BF16)]
        for ref, wb in zip((w1_bf, w3_bf, w2_bf), ws):
            ref[...] = wb
        return ws

    def swiglu(slots, weights):
        w1b, w3b, w2b = weights
        halves = [slots] if len(slots) == 1 else [slots[:len(slots) // 2], slots[len(slots) // 2:]]
        xs = [jnp.concatenate([_unpack_bf16_pairs(x_buf[s]).astype(BF16) for s in half], axis=0) for half in halves]
        up = [(_dot(xb, w1b), _dot(xb, w3b)) for xb in xs]
        hs = [(h1 * jax.nn.sigmoid(h1) * h3).astype(BF16) for h1, h3 in up]
        ys = [lax.bitcast_convert_type(_pack_bf16_pairs(_dot(h, w2b)), y_buf.dtype) for h in hs]
        return [y[i * bm:(i + 1) * bm] for y, half in zip(ys, halves) for i in range(len(half))]

    def process(blocks, cast_here=False):
        for g in blocks:
            fetch(g).wait()
        for g in blocks:
            @pl.when(g + ahead < total)
            def _():
                fetch(g + ahead).start()
        weights = cast_weights() if cast_here else (w1_bf[...], w3_bf[...], w2_bf[...])
        ys = swiglu([g % n_slot for g in blocks], weights)
        for g in blocks:
            @pl.when(g >= n_slot)
            def _():
                write_back(g - n_slot).wait()
        for g, y in zip(blocks, ys):
            y_buf[g % n_slot] = y
            write_back(g).start()

    n_full = n_block[e] // BLOCK_GROUP

    @pl.when(n_full > 0)
    def _():
        process([first_block[e] + j for j in range(BLOCK_GROUP)], cast_here=True)

    @pl.when((n_full == 0) & (n_block[e] > 0))
    def _():
        cast_weights()

    @pl.loop(1, n_full)
    def _(i):
        process([first_block[e] + i * BLOCK_GROUP + j for j in range(BLOCK_GROUP)])

    done = n_full * BLOCK_GROUP
    size = BLOCK_GROUP // 2
    while size >= 1:
        left = n_block[e] - done
        start = first_block[e] + done

        @pl.when(left >= size)
        def _(start=start, size=size):
            process([start + j for j in range(size)])

        done = done + left // size * size
        size //= 2

    @pl.when(e == pl.num_programs(0) - 1)
    def _():
        for back in range(1, n_slot + 1):
            @pl.when(total >= back)
            def _():
                write_back(total - back).wait()


def _experts(first_block, n_block, n_used, xg, w1, w3, w2, bm):
    n_rows, w = xg.shape
    n_exp, d, de = w1.shape
    weight = lambda e, *_: (e, 0, 0)
    return pl.pallas_call(
        _expert_kernel,
        grid_spec=pltpu.PrefetchScalarGridSpec(
            num_scalar_prefetch=3,
            grid=(n_exp,),
            in_specs=[pl.BlockSpec(memory_space=pl.ANY),
                      pl.BlockSpec((1, d, de), weight),
                      pl.BlockSpec((1, d, de), weight),
                      pl.BlockSpec((1, de, d), weight)],
            out_specs=pl.BlockSpec(memory_space=pl.ANY),
            scratch_shapes=[pltpu.VMEM((ROW_RING, bm, w), xg.dtype), pltpu.VMEM((ROW_RING, bm, w), jnp.int32),
                            pltpu.VMEM((d, de), BF16), pltpu.VMEM((d, de), BF16), pltpu.VMEM((de, d), BF16),
                            pltpu.SemaphoreType.DMA((ROW_RING,)), pltpu.SemaphoreType.DMA((ROW_RING,))]),
        out_shape=jax.ShapeDtypeStruct((n_rows, w), jnp.int32),
        compiler_params=pltpu.CompilerParams(dimension_semantics=("arbitrary",),
                                             vmem_limit_bytes=VMEM_LIMIT),
        name="routed_experts",
    )(first_block, n_block, n_used, xg, w1, w3, w2)


def _combine_kernel(y_ref, x1_ref, sw1_ref, sw3_ref, sw2_ref, g2_ref, b2_ref, out_ref):
    x1 = x1_ref[...]
    xb = x1.astype(BF16)
    h1 = _dot(xb, sw1_ref[...])
    h3 = _dot(xb, sw3_ref[...])
    ffn = _dot((h1 * jax.nn.sigmoid(h1) * h3).astype(BF16), sw2_ref[...])
    ffn = ffn + _unpack_bf16_pairs(lax.bitcast_convert_type(y_ref[...], jnp.uint32))
    out_ref[...] = _layer_norm(DEEPNORM_ALPHA * x1 + ffn, g2_ref[...], b2_ref[...])


def _combine(y_routed, x1, sw1, sw3, sw2, g2, b2, tn=512):
    t, d = x1.shape
    w = y_routed.shape[1]
    row = lambda p: p.reshape(1, -1)
    operands = (y_routed, x1, sw1.astype(BF16), sw3.astype(BF16), sw2.astype(BF16), row(g2), row(b2))
    in_specs = [pl.BlockSpec((tn, w), lambda i: (i, 0)),
                pl.BlockSpec((tn, d), lambda i: (i, 0))]
    in_specs += [_const_spec(op.shape) for op in operands[2:]]
    return pl.pallas_call(
        _combine_kernel,
        grid=(t // tn,),
        in_specs=in_specs,
        out_specs=pl.BlockSpec((tn, d), lambda i: (i, 0)),
        out_shape=jax.ShapeDtypeStruct((t, d), F32),
        compiler_params=pltpu.CompilerParams(dimension_semantics=("arbitrary",),
                                             vmem_limit_bytes=VMEM_LIMIT),
        name="combine_shared_ln",
    )(*operands)


def _block_table(counts, bm):
    n_block = (counts + bm - 1) // bm
    ends = jnp.cumsum(n_block)
    first_block = ends - n_block
    i32 = lambda v: v.astype(jnp.int32)
    return i32(first_block * bm), i32(first_block), i32(n_block), i32(ends[-1:])


def _moe(x1, x1p, logits, router_bias, w1, w3, w2, sw1, sw3, sw2, g2, b2, bm=128):
    t, d = x1.shape
    n_exp = logits.shape[1]
    idx, gate, rank, counts = _route(logits, router_bias)
    n_blocks = (t * TOP_K) // bm + n_exp
    starts, first_block, n_block, n_used = _block_table(counts[:, 0], bm)
    slots, gate_words = _slots(idx, rank, gate, starts)
    xg = _sc_dispatch(slots, x1p, n_blocks * bm)
    yg = _experts(first_block, n_block, n_used, xg, w1, w3, w2, bm)
    y_routed = _sc_combine(slots, gate_words, yg)
    return _combine(y_routed, x1, sw1, sw3, sw2, g2, b2)


def kernel(x, w_in, conv_w, conv_b, lru_wa, lru_ba, lru_wx, lru_bx, lru_lambda, pool_w, pool_scale, w_out, ln1_g, ln1_b, router_w, router_bias, exp_w1, exp_w3, exp_w2, sh_w1, sh_w3, sh_w2, ln2_g, ln2_b):
    nb, s, d = x.shape
    for l in range(DEPTH):
        x1, x1p, logits = _mixer(x, w_in[l], conv_w[l], conv_b[l], lru_wa[l], lru_ba[l], lru_wx[l], lru_bx[l],
                                 lru_lambda[l], pool_w[l], pool_scale[l], w_out[l], ln1_g[l], ln1_b[l],
                                 router_w[l])
        t = nb * s
        x = _moe(x1.reshape(t, d), x1p.reshape(t, d // 2), logits.reshape(t, -1), router_bias[l],
                 exp_w1[l], exp_w3[l], exp_w2[l], sh_w1[l], sh_w3[l], sh_w2[l], ln2_g[l], ln2_b[l])
        x = x.reshape(nb, s, d)
    return x
```

```python
import functools
import math

import jax
import jax.numpy as jnp
from jax import lax
from jax.experimental import pallas as pl
from jax.experimental.pallas import tpu as pltpu
from jax.experimental.pallas import tpu_sc as plsc

LRU_HEADS = 8
CONV_WIDTH = 4
LRU_C = 8.0
POOL_WINDOWS = (2, 4, 8, 16)
N_EXPERT_GROUPS = 8
TOPK_GROUPS = 4
TOP_K = 8
ROUTED_SCALE = 2.5
LN_EPS = 1e-5
DEPTH = 1
DEEPNORM_ALPHA = (2.0 * DEPTH) ** 0.25

MXU_DIM = 256
POOL_GROUP_DIM = 128
CONV_HIST = 8
POOL_HIST = 16
SC_CHUNK = 128
SC_SUB = 8
MIXER_PARTS = 4
BLOCK_GROUP = 4
ROW_RING = 8
VMEM_LIMIT = 56 * 1024 * 1024

F32 = jnp.float32
BF16 = jnp.bfloat16


def _dot(a, b):
    return jnp.dot(a, b, preferred_element_type=F32)


def _layer_norm(z, g, b):
    mu = jnp.mean(z, axis=-1, keepdims=True)
    zc = z - mu
    var = jnp.mean(zc * zc, axis=-1, keepdims=True)
    return zc * lax.rsqrt(var + LN_EPS) * g + b


def _block_diag_dot(xb, w_ref):
    n = w_ref.shape[0]
    return jnp.concatenate(
        [_dot(xb[:, i * MXU_DIM:(i + 1) * MXU_DIM], w_ref[i]) for i in range(n)], axis=1)


def _causal_conv(u, hist_ref, part, conv_w, conv_b):
    pb, ts, c = u.shape
    ext = jnp.concatenate([hist_ref[part], u], axis=1)
    hist_ref[part] = u[:, ts - CONV_HIST:, :]
    uc = jnp.broadcast_to(conv_b.reshape(1, 1, c), (pb, ts, c))
    for k in range(CONV_WIDTH):
        off = CONV_HIST - (CONV_WIDTH - 1) + k
        uc = uc + conv_w[k:k + 1, :].reshape(1, 1, c) * ext[:, off:off + ts, :]
    return uc


def _lru_scan(uc, ga, gx, lam, carry_ref, part, t0):
    pb, ts, c = uc.shape
    r = jax.nn.sigmoid(ga)
    i_gate = jax.nn.sigmoid(gx)
    softplus_neg_lam = jnp.maximum(-lam, 0.0) + jnp.log1p(jnp.exp(-jnp.abs(lam)))
    log_a = (-LRU_C) * r * softplus_neg_lam
    a = jnp.exp(log_a)
    mult = jnp.sqrt(jnp.tanh(-log_a) * (1.0 + a * a))
    t_local = lax.broadcasted_iota(jnp.int32, (pb, ts, c), 1)
    first = (t_local + t0) == 0
    a3 = a.reshape(pb, ts, c)
    x3 = jnp.where(first, 1.0, mult.reshape(pb, ts, c)) * i_gate.reshape(pb, ts, c) * uc

    shift = 1
    while shift < ts:
        valid = t_local >= shift
        a_prev = pltpu.roll(a3, shift, axis=1)
        x_prev = pltpu.roll(x3, shift, axis=1)
        x3 = jnp.where(valid, a3 * x_prev, 0.0) + x3
        a3 = jnp.where(valid, a3 * a_prev, a3)
        shift *= 2
    h = x3 + a3 * carry_ref[part][:, None, :]
    carry_ref[part] = h[:, ts - 1, :]
    return h


def _multiscale_pool(v, hist_ref, part, t0):
    pb, ts, c = v.shape
    ext = jnp.concatenate([hist_ref[part], v], axis=1)
    hist_ref[part] = v[:, ts - POOL_HIST:, :]
    t_glob = lax.broadcasted_iota(jnp.int32, (1, ts, POOL_GROUP_DIM), 1) + t0
    pooled = []
    for g, win in enumerate(POOL_WINDOWS):
        lanes = slice(g * POOL_GROUP_DIM, (g + 1) * POOL_GROUP_DIM)
        s = ext[:, :, lanes]
        shift = 1
        while shift < win:
            s = s + pltpu.roll(s, shift, axis=1)
            shift *= 2
        count = jnp.minimum(t_glob + 1, win).astype(F32)
        pooled.append(s[:, POOL_HIST:, :] / count - v[:, :, lanes])
    return jnp.concatenate(pooled, axis=-1)


def _mixer_kernel(x_ref, w_in_ref, conv_w_ref, conv_b_ref, wa_ref, ba_ref, wx_ref, bx_ref, lam_ref,
                  pool_w_ref, pool_scale_ref, w_out_ref, g1_ref, b1_ref, rw_hi_ref, rw_lo_ref,
                  x1_ref, x1p_ref, logit_ref, hist_u, hist_v, carry):
    j = pl.program_id(0)
    nb, ts, d = x_ref.shape
    c = hist_u.shape[-1]
    pb = nb // MIXER_PARTS
    rows = pb * ts
    t0 = j * ts
    parts = [slice(p * pb, (p + 1) * pb) for p in range(MIXER_PARTS)]

    @pl.when(j == 0)
    def _():
        hist_u[...] = jnp.zeros_like(hist_u)
        hist_v[...] = jnp.zeros_like(hist_v)
        carry[...] = jnp.zeros_like(carry)

    xs = [x_ref[p].reshape(rows, d) for p in parts]
    projs = [_dot(x.astype(BF16), w_in_ref[...]) for x in xs]
    ucs = [_causal_conv(proj[:, :c].reshape(pb, ts, c), hist_u, p, conv_w_ref[...], conv_b_ref[...])
           for proj, p in zip(projs, parts)]
    ucbs = [uc.reshape(rows, c).astype(BF16) for uc in ucs]
    gas = [_block_diag_dot(ucb, wa_ref) + ba_ref[...] for ucb in ucbs]
    gxs = [_block_diag_dot(ucb, wx_ref) + bx_ref[...] for ucb in ucbs]
    hs = [_lru_scan(uc, ga, gx, lam_ref[...], carry, p, t0) for uc, ga, gx, p in zip(ucs, gas, gxs, parts)]
    y_lrus = []
    for h, proj in zip(hs, projs):
        gate = proj[:, c:2 * c]
        gelu_gate = 0.5 * gate * (1.0 + jnp.tanh(math.sqrt(2.0 / math.pi) * (gate + 0.044715 * gate * gate * gate)))
        y_lrus.append(h.reshape(rows, c) * gelu_gate)
    pooled = [_multiscale_pool(proj[:, 2 * c:].reshape(pb, ts, c), hist_v, p, t0).reshape(rows, c).astype(BF16)
              for proj, p in zip(projs, parts)]
    y_pools = [_block_diag_dot(pl_, pool_w_ref) * pool_scale_ref[...] for pl_ in pooled]
    mixes = [_dot(jnp.concatenate([y_lru, y_pool], axis=1).astype(BF16), w_out_ref[...])
             for y_lru, y_pool in zip(y_lrus, y_pools)]
    x1s = [_layer_norm(DEEPNORM_ALPHA * x + mix, g1_ref[...], b1_ref[...]) for x, mix in zip(xs, mixes)]

    for p, x1 in zip(parts, x1s):
        x1_ref[p] = x1.reshape(pb, ts, d)
        hi = x1.astype(BF16)
        hi_f = hi.astype(F32)
        lo = (x1 - hi_f).astype(BF16)
        logits = _dot(hi, rw_hi_ref[...]) + _dot(lo, rw_hi_ref[...]) + _dot(hi, rw_lo_ref[...])
        logit_ref[p] = logits.reshape(pb, ts, logits.shape[-1])
        bits = lax.bitcast_convert_type(hi_f, jnp.uint32)
        packed = bits[:, :d // 2] | (bits[:, d // 2:] >> 16)
        x1p_ref[p] = packed.reshape(pb, ts, d // 2)


def _const_spec(shape):
    return pl.BlockSpec(shape, lambda j: (0,) * len(shape), pipeline_mode=pl.Buffered(1))


def _regroup_block_diag(w, width):
    heads, hd, _ = w.shape
    per = width // hd
    w = w.reshape(heads // per, per, hd, hd)
    eye = jnp.eye(per, dtype=w.dtype)
    return jnp.einsum("gpij,pq->gpiqj", w, eye).reshape(heads // per, width, width)


def _mixer(x, w_in, conv_w, conv_b, wa, ba, wx, bx, lam, pool_w, pool_scale, w_out, g1, b1, router_w, ts=64):
    nb, s, d = x.shape
    c = conv_w.shape[-1]
    n_exp = router_w.shape[-1]
    rw_hi = router_w.astype(BF16)
    rw_lo = (router_w - rw_hi.astype(F32)).astype(BF16)
    row = lambda p: p.reshape(1, -1)
    operands = (
        x, w_in.astype(BF16), conv_w, row(conv_b),
        _regroup_block_diag(wa, MXU_DIM).astype(BF16), row(ba),
        _regroup_block_diag(wx, MXU_DIM).astype(BF16), row(bx), row(lam),
        _regroup_block_diag(pool_w, MXU_DIM).astype(BF16), row(pool_scale),
        w_out.astype(BF16), row(g1), row(b1), rw_hi, rw_lo)
    tile = lambda width: pl.BlockSpec((nb, ts, width), lambda j: (0, j, 0))
    in_specs = [tile(d)] + [_const_spec(op.shape) for op in operands[1:]]
    return pl.pallas_call(
        _mixer_kernel,
        grid=(s // ts,),
        in_specs=in_specs,
        out_specs=[tile(d), tile(d // 2), tile(n_exp)],
        out_shape=[jax.ShapeDtypeStruct((nb, s, d), F32),
                   jax.ShapeDtypeStruct((nb, s, d // 2), jnp.uint32),
                   jax.ShapeDtypeStruct((nb, s, n_exp), F32)],
        scratch_shapes=[pltpu.VMEM((nb, CONV_HIST, c), F32),
                        pltpu.VMEM((nb, POOL_HIST, c), F32),
                        pltpu.VMEM((nb, c), F32)],
        compiler_params=pltpu.CompilerParams(dimension_semantics=("arbitrary",),
                                             vmem_limit_bytes=VMEM_LIMIT),
        name="mixer_ln_router",
    )(*operands)


def _first_row_of(cond, rows_rev, n):
    return n - jnp.max(jnp.where(cond, rows_rev, 0.0), axis=0, keepdims=True)


def _route_kernel(logit_ref, bias_ref, idx_ref, gate_ref, rank_ref, count_ref, count_scr):
    i = pl.program_id(0)

    @pl.when(i == 0)
    def _():
        count_scr[...] = jnp.zeros_like(count_scr)

    scores = jax.nn.sigmoid(logit_ref[...].T)
    n_exp, tn = scores.shape
    gsz = n_exp // N_EXPERT_GROUPS
    neg = -jnp.inf
    biased = scores + bias_ref[...]
    row = lax.broadcasted_iota(jnp.int32, (n_exp, tn), 0).astype(F32)
    row_rev = n_exp - row

    g_row = lax.broadcasted_iota(jnp.int32, (N_EXPERT_GROUPS, tn), 0)
    group_score = jnp.zeros((N_EXPERT_GROUPS, tn), F32)
    grp_row = lax.broadcasted_iota(jnp.int32, (gsz, tn), 0).astype(F32)
    grp_rev = gsz - grp_row
    for g in range(N_EXPERT_GROUPS):
        blk = biased[g * gsz:(g + 1) * gsz]
        m1 = jnp.max(blk, axis=0, keepdims=True)
        i1 = _first_row_of(blk == m1, grp_rev, gsz)
        m2 = jnp.max(jnp.where(grp_row == i1, neg, blk), axis=0, keepdims=True)
        group_score = jnp.where(g_row == g, m1 + m2, group_score)

    beaten_by = jnp.zeros((N_EXPERT_GROUPS, tn), F32)
    for g in range(N_EXPERT_GROUPS):
        other = jnp.max(jnp.where(g_row == g, group_score, neg), axis=0, keepdims=True)
        wins = (other > group_score) | ((other == group_score) & (g < g_row))
        beaten_by = beaten_by + jnp.where(wins, 1.0, 0.0)
    masked = []
    for g in range(N_EXPERT_GROUPS):
        beaten_g = jnp.max(jnp.where(g_row == g, beaten_by, 0.0), axis=0, keepdims=True)
        masked.append(jnp.where(beaten_g < TOPK_GROUPS, biased[g * gsz:(g + 1) * gsz], neg))
    masked = jnp.concatenate(masked, axis=0)

    k_row = lax.broadcasted_iota(jnp.int32, (TOP_K, tn), 0)
    selected = jnp.zeros((n_exp, tn), F32)
    idx = jnp.zeros((TOP_K, tn), F32)
    gate = jnp.zeros((TOP_K, tn), F32)
    hits = []
    for k in range(TOP_K):
        m = jnp.max(masked, axis=0, keepdims=True)
        ik = _first_row_of(masked == m, row_rev, n_exp)
        hit = row == ik
        gk = jnp.sum(jnp.where(hit, scores, 0.0), axis=0, keepdims=True)
        masked = jnp.where(hit, neg, masked)
        selected = jnp.where(hit, 1.0, selected)
        idx = jnp.where(k_row == k, ik, idx)
        gate = jnp.where(k_row == k, gk, gate)
        hits.append(hit)
    gate = gate / jnp.sum(gate, axis=0, keepdims=True) * ROUTED_SCALE

    earlier = (lax.broadcasted_iota(jnp.int32, (tn, tn), 0) < lax.broadcasted_iota(jnp.int32, (tn, tn), 1))
    before = _dot(selected.astype(BF16), jnp.where(earlier, 1.0, 0.0).astype(BF16)) + count_scr[...]
    rank = jnp.zeros((TOP_K, tn), F32)
    for k in range(TOP_K):
        rk = jnp.sum(jnp.where(hits[k], before, 0.0), axis=0, keepdims=True)
        rank = jnp.where(k_row == k, rk, rank)
    count_scr[...] = count_scr[...] + jnp.sum(selected, axis=1, keepdims=True)

    bits = lax.bitcast_convert_type(gate.astype(BF16).astype(F32), jnp.uint32)
    words = lax.bitcast_convert_type(bits | (bits >> 16), jnp.int32)
    chunk = idx_ref.shape[-1]
    for c in range(idx_ref.shape[0]):
        lanes = slice(c * chunk, (c + 1) * chunk)
        idx_ref[c] = idx[:, lanes].astype(jnp.int32)
        rank_ref[c] = rank[:, lanes].astype(jnp.int32)
        gate_ref[c] = words[:, lanes]
    count_ref[...] = count_scr[...].astype(jnp.int32)


def _route(logits, bias, tn=512):
    t, n_exp = logits.shape
    chunked = pl.BlockSpec((tn // SC_CHUNK, TOP_K, SC_CHUNK), lambda i: (i, 0, 0))
    chunked_shape = jax.ShapeDtypeStruct((t // SC_CHUNK, TOP_K, SC_CHUNK), jnp.int32)
    return pl.pallas_call(
        _route_kernel,
        grid=(t // tn,),
        in_specs=[pl.BlockSpec((tn, n_exp), lambda i: (i, 0)),
                  pl.BlockSpec((n_exp, 1), lambda i: (0, 0))],
        out_specs=[chunked, chunked, chunked, pl.BlockSpec((n_exp, 1), lambda i: (0, 0))],
        out_shape=[chunked_shape, chunked_shape, chunked_shape, jax.ShapeDtypeStruct((n_exp, 1), jnp.int32)],
        scratch_shapes=[pltpu.VMEM((n_exp, 1), F32)],
        compiler_params=pltpu.CompilerParams(dimension_semantics=("arbitrary",),
                                             vmem_limit_bytes=VMEM_LIMIT),
        name="route_topk_rank",
    )(logits, bias.reshape(n_exp, 1))


def _unpack_bf16_pairs(p):
    hi = lax.bitcast_convert_type(p & jnp.uint32(0xFFFF0000), F32)
    lo = lax.bitcast_convert_type(p << 16, F32)
    return jnp.concatenate([hi, lo], axis=1)


def _pack_bf16_pairs(y):
    n = y.shape[1] // 2
    bits = lax.bitcast_convert_type(y.astype(BF16).astype(F32), jnp.uint32)
    return bits[:, :n] | (bits[:, n:] >> 16)


def _sc_worker_layout(n_chunks_total):
    info = plsc.get_sparse_core_info()
    n_workers = info.num_cores * info.num_subcores
    return info.num_cores, n_chunks_total // n_workers


def _sc_dispatch(idx, rank, expert_start, x1p, n_rows):
    n_chunks_total, n_k, chunk = idx.shape
    t, w = x1p.shape
    n_cores, n_chunks = _sc_worker_layout(n_chunks_total)
    lanes = plsc.get_sparse_core_info().num_lanes
    mesh = plsc.VectorSubcoreMesh(core_axis_name="c", subcore_axis_name="s")
    table = pltpu.VMEM((n_k, chunk), jnp.int32)

    @functools.partial(
        pl.kernel, mesh=mesh, name="sc_dispatch_rows",
        compiler_params=pltpu.CompilerParams(needs_layout_passes=False),
        out_type=[jax.ShapeDtypeStruct((n_rows, w), x1p.dtype), jax.ShapeDtypeStruct(idx.shape, jnp.int32)],
        scratch_types=[table, table, pltpu.VMEM(expert_start.shape, jnp.int32), pltpu.VMEM((chunk, w), x1p.dtype),
                       pltpu.SemaphoreType.DMA])
    def dispatch(idx_hbm, rank_hbm, start_hbm, x_hbm, xg_hbm, slots_hbm, slot_v, rank_v, start_v, rows_v, sem):
        wid = lax.axis_index("s") * n_cores + lax.axis_index("c")
        pltpu.sync_copy(start_hbm, start_v)

        @pl.loop(0, n_chunks)
        def _(ci):
            chunk_id = wid * n_chunks + ci
            pltpu.sync_copy(idx_hbm.at[chunk_id], slot_v)
            pltpu.sync_copy(rank_hbm.at[chunk_id], rank_v)
            pltpu.sync_copy(x_hbm.at[pl.ds(chunk_id * chunk, chunk)], rows_v)
            for k in range(n_k):
                for j in range(chunk // lanes):
                    part = pl.ds(j * lanes, lanes)
                    slot_v[k, part] = plsc.load_gather(start_v, [slot_v[k, part]]) + rank_v[k, part]
            copies = [pltpu.make_async_copy(rows_v, xg_hbm.at[slot_v.at[k]], sem) for k in range(n_k)]
            for cp in copies:
                cp.start()
            pltpu.sync_copy(slot_v, slots_hbm.at[chunk_id])
            for cp in copies:
                cp.wait()

    return dispatch(idx, rank, expert_start, x1p)


def _sc_combine(slots, gate_words, yg):
    n_chunks_total, n_k, chunk = slots.shape
    w = yg.shape[1]
    t = n_chunks_total * chunk
    n_cores, n_chunks = _sc_worker_layout(n_chunks_total)
    lanes = plsc.get_sparse_core_info().num_lanes
    n_sub = chunk // SC_SUB
    mesh = plsc.VectorSubcoreMesh(core_axis_name="c", subcore_axis_name="s")
    rows_t = pltpu.VMEM((n_k, SC_SUB, w), jnp.int32)
    out_t = pltpu.VMEM((SC_SUB, w), jnp.int32)
    dma = pltpu.SemaphoreType.DMA

    @functools.partial(
        pl.kernel, mesh=mesh, name="sc_combine_rows",
        compiler_params=pltpu.CompilerParams(needs_layout_passes=False),
        out_type=jax.ShapeDtypeStruct((t, w), jnp.int32),
        scratch_types=[pltpu.VMEM((n_k, chunk), jnp.int32), pltpu.VMEM((n_k, chunk), jnp.int32),
                       rows_t, rows_t, out_t, out_t, dma, dma, dma, dma])
    def combine(slots_hbm, gate_hbm, yg_hbm, out_hbm, idx_v, gate_v, rows0, rows1, out0, out1,
                row_sem0, row_sem1, out_sem0, out_sem1):
        wid = lax.axis_index("s") * n_cores + lax.axis_index("c")
        rows, outs = (rows0, rows1), (out0, out1)
        row_sems, out_sems = (row_sem0, row_sem1), (out_sem0, out_sem1)

        def gathers(sub, b):
            return [pltpu.make_async_copy(yg_hbm.at[idx_v.at[k, pl.ds(sub * SC_SUB, SC_SUB)]], rows[b].at[k],
                                          row_sems[b]) for k in range(n_k)]

        @pl.loop(0, n_chunks)
        def _(ci):
            chunk_id = wid * n_chunks + ci
            tok0 = chunk_id * chunk
            pltpu.sync_copy(slots_hbm.at[chunk_id], idx_v)
            pltpu.sync_copy(gate_hbm.at[chunk_id], gate_v)

            def store(sub, b):
                return pltpu.make_async_copy(outs[b], out_hbm.at[pl.ds(tok0 + sub * SC_SUB, SC_SUB)], out_sems[b])

            for cp in gathers(0, 0):
                cp.start()

            @pl.loop(0, n_sub // 2)
            def _(pair):
                for b in (0, 1):
                    sub = 2 * pair + b

                    @pl.when(sub + 1 < n_sub)
                    def _():
                        for cp in gathers(sub + 1, 1 - b):
                            cp.start()

                    for cp in gathers(sub, b):
                        cp.wait()

                    @pl.when(sub >= 2)
                    def _():
                        store(sub - 2, b).wait()

                    buf, out = rows[b], outs[b]

                    @pl.loop(0, SC_SUB)
                    def _(tt):
                        col = jnp.full((lanes,), sub * SC_SUB + tt, jnp.int32)
                        gates = [plsc.bitcast(plsc.load_gather(gate_v, [jnp.full((lanes,), k, jnp.int32), col]), BF16)
                                 for k in range(n_k)]
                        for v in range(w // lanes):
                            terms = [gates[k] * plsc.bitcast(buf[k, tt, pl.ds(v * lanes, lanes)], BF16)
                                     for k in range(n_k)]
                            while len(terms) > 1:
                                terms = [terms[i] + terms[i + 1] for i in range(0, len(terms), 2)]
                            out[tt, pl.ds(v * lanes, lanes)] = plsc.bitcast(terms[0], jnp.int32)

                    store(sub, b).start()

            store(n_sub - 2, 0).wait()
            store(n_sub - 1, 1).wait()

    return combine(slots, gate_words, yg)


def _expert_kernel(first_block, n_block, n_used, xg_hbm, w1_ref, w3_ref, w2_ref, yg_hbm,
                   x_buf, y_buf, w1_bf, w3_bf, w2_bf, in_sem, out_sem):
    e = pl.program_id(0)
    n_slot, bm, _ = x_buf.shape
    ahead = n_slot - BLOCK_GROUP
    total = n_used[0]

    def fetch(g):
        slot = g % n_slot
        return pltpu.make_async_copy(xg_hbm.at[pl.ds(g * bm, bm)], x_buf.at[slot], in_sem.at[slot])

    def write_back(g):
        slot = g % n_slot
        return pltpu.make_async_copy(y_buf.at[slot], yg_hbm.at[pl.ds(g * bm, bm)], out_sem.at[slot])

    @pl.when(e == 0)
    def _():
        for g in range(ahead):
            @pl.when(g < total)
            def _():
                fetch(g).start()

    def cast_weights():
        ws = [w1_ref[0].astype(BF16), w3_ref[0].astype(BF16), w2_ref[0].astype(BF16)]
        for ref, wb in zip((w1_bf, w3_bf, w2_bf), ws):
            ref[...] = wb
        return ws

    def swiglu(slots, weights):
        w1b, w3b, w2b = weights
        xs = [_unpack_bf16_pairs(x_buf[s]).astype(BF16) for s in slots]
        up = [(_dot(xb, w1b), _dot(xb, w3b)) for xb in xs]
        hs = [(h1 * jax.nn.sigmoid(h1) * h3).astype(BF16) for h1, h3 in up]
        ys = [_dot(h, w2b) for h in hs]
        return [lax.bitcast_convert_type(_pack_bf16_pairs(y), y_buf.dtype) for y in ys]

    def process(blocks, cast_here=False):
        for g in blocks:
            fetch(g).wait()
        for g in blocks:
            @pl.when(g + ahead < total)
            def _():
                fetch(g + ahead).start()
        weights = cast_weights() if cast_here else (w1_bf[...], w3_bf[...], w2_bf[...])
        ys = swiglu([g % n_slot for g in blocks], weights)
        for g in blocks:
            @pl.when(g >= n_slot)
            def _():
                write_back(g - n_slot).wait()
        for g, y in zip(blocks, ys):
            y_buf[g % n_slot] = y
            write_back(g).start()

    n_full = n_block[e] // BLOCK_GROUP

    @pl.when(n_full > 0)
    def _():
        process([first_block[e] + j for j in range(BLOCK_GROUP)], cast_here=True)

    @pl.when((n_full == 0) & (n_block[e] > 0))
    def _():
        cast_weights()

    @pl.loop(1, n_full)
    def _(i):
        process([first_block[e] + i * BLOCK_GROUP + j for j in range(BLOCK_GROUP)])

    done = n_full * BLOCK_GROUP
    size = BLOCK_GROUP // 2
    while size >= 1:
        left = n_block[e] - done
        start = first_block[e] + done

        @pl.when(left >= size)
        def _(start=start, size=size):
            process([start + j for j in range(size)])

        done = done + left // size * size
        size //= 2

    @pl.when(e == pl.num_programs(0) - 1)
    def _():
        for back in range(1, n_slot + 1):
            @pl.when(total >= back)
            def _():
                write_back(total - back).wait()


def _experts(first_block, n_block, n_used, xg, w1, w3, w2, bm):
    n_rows, w = xg.shape
    n_exp, d, de = w1.shape
    weight = lambda e, *_: (e, 0, 0)
    return pl.pallas_call(
        _expert_kernel,
        grid_spec=pltpu.PrefetchScalarGridSpec(
            num_scalar_prefetch=3,
            grid=(n_exp,),
            in_specs=[pl.BlockSpec(memory_space=pl.ANY),
                      pl.BlockSpec((1, d, de), weight),
                      pl.BlockSpec((1, d, de), weight),
                      pl.BlockSpec((1, de, d), weight)],
            out_specs=pl.BlockSpec(memory_space=pl.ANY),
            scratch_shapes=[pltpu.VMEM((ROW_RING, bm, w), xg.dtype), pltpu.VMEM((ROW_RING, bm, w), jnp.int32),
                            pltpu.VMEM((d, de), BF16), pltpu.VMEM((d, de), BF16), pltpu.VMEM((de, d), BF16),
                            pltpu.SemaphoreType.DMA((ROW_RING,)), pltpu.SemaphoreType.DMA((ROW_RING,))]),
        out_shape=jax.ShapeDtypeStruct((n_rows, w), jnp.int32),
        compiler_params=pltpu.CompilerParams(dimension_semantics=("arbitrary",),
                                             vmem_limit_bytes=VMEM_LIMIT),
        name="routed_experts",
    )(first_block, n_block, n_used, xg, w1, w3, w2)


def _combine_kernel(y_ref, x1_ref, sw1_ref, sw3_ref, sw2_ref, g2_ref, b2_ref, out_ref):
    x1 = x1_ref[...]
    xb = x1.astype(BF16)
    h1 = _dot(xb, sw1_ref[...])
    h3 = _dot(xb, sw3_ref[...])
    ffn = _dot((h1 * jax.nn.sigmoid(h1) * h3).astype(BF16), sw2_ref[...])
    ffn = ffn + _unpack_bf16_pairs(lax.bitcast_convert_type(y_ref[...], jnp.uint32))
    out_ref[...] = _layer_norm(DEEPNORM_ALPHA * x1 + ffn, g2_ref[...], b2_ref[...])


def _combine(y_routed, x1, sw1, sw3, sw2, g2, b2, tn=512):
    t, d = x1.shape
    w = y_routed.shape[1]
    row = lambda p: p.reshape(1, -1)
    operands = (y_routed, x1, sw1.astype(BF16), sw3.astype(BF16), sw2.astype(BF16), row(g2), row(b2))
    in_specs = [pl.BlockSpec((tn, w), lambda i: (i, 0)),
                pl.BlockSpec((tn, d), lambda i: (i, 0))]
    in_specs += [_const_spec(op.shape) for op in operands[2:]]
    return pl.pallas_call(
        _combine_kernel,
        grid=(t // tn,),
        in_specs=in_specs,
        out_specs=pl.BlockSpec((tn, d), lambda i: (i, 0)),
        out_shape=jax.ShapeDtypeStruct((t, d), F32),
        compiler_params=pltpu.CompilerParams(dimension_semantics=("arbitrary",),
                                             vmem_limit_bytes=VMEM_LIMIT),
        name="combine_shared_ln",
    )(*operands)


def _block_table(counts, bm):
    n_block = (counts + bm - 1) // bm
    ends = jnp.cumsum(n_block)
    first_block = ends - n_block
    i32 = lambda v: v.astype(jnp.int32)
    return i32(first_block * bm), i32(first_block), i32(n_block), i32(ends[-1:])


def _moe(x1, x1p, logits, router_bias, w1, w3, w2, sw1, sw3, sw2, g2, b2, bm=256):
    t, d = x1.shape
    n_exp = logits.shape[1]
    idx, gate_words, rank, counts = _route(logits, router_bias)
    n_blocks = (t * TOP_K) // bm + n_exp
    starts, first_block, n_block, n_used = _block_table(counts[:, 0], bm)
    xg, slots = _sc_dispatch(idx, rank, starts, x1p, n_blocks * bm)
    yg = _experts(first_block, n_block, n_used, xg, w1, w3, w2, bm)
    y_routed = _sc_combine(slots, gate_words, yg)
    return _combine(y_routed, x1, sw1, sw3, sw2, g2, b2)


def kernel(x, w_in, conv_w, conv_b, lru_wa, lru_ba, lru_wx, lru_bx, lru_lambda, pool_w, pool_scale, w_out, ln1_g, ln1_b, router_w, router_bias, exp_w1, exp_w3, exp_w2, sh_w1, sh_w3, sh_w2, ln2_g, ln2_b):
    nb, s, d = x.shape
    for l in range(DEPTH):
        x1, x1p, logits = _mixer(x, w_in[l], conv_w[l], conv_b[l], lru_wa[l], lru_ba[l], lru_wx[l], lru_bx[l],
                                 lru_lambda[l], pool_w[l], pool_scale[l], w_out[l], ln1_g[l], ln1_b[l],
                                 router_w[l])
        t = nb * s
        x = _moe(x1.reshape(t, d), x1p.reshape(t, d // 2), logits.reshape(t, -1), router_bias[l],
                 exp_w1[l], exp_w3[l], exp_w2[l], sh_w1[l], sh_w3[l], sh_w2[l], ln2_g[l], ln2_b[l])
        x = x.reshape(nb, s, d)
    return x
```

```python
import functools
import math

import jax
import jax.numpy as jnp
from jax import lax
from jax.experimental import pallas as pl
from jax.experimental.pallas import tpu as pltpu
from jax.experimental.pallas import tpu_sc as plsc

LRU_HEADS = 8
CONV_WIDTH = 4
LRU_C = 8.0
POOL_WINDOWS = (2, 4, 8, 16)
N_EXPERT_GROUPS = 8
TOPK_GROUPS = 4
TOP_K = 8
ROUTED_SCALE = 2.5
LN_EPS = 1e-5
DEPTH = 1
DEEPNORM_ALPHA = (2.0 * DEPTH) ** 0.25

MXU_DIM = 256
POOL_GROUP_DIM = 128
CONV_HIST = 8
POOL_HIST = 16
SC_CHUNK = 128
SC_SUB = 8
MIXER_PARTS = 4
BLOCK_GROUP = 4
ROW_RING = 8
VMEM_LIMIT = 56 * 1024 * 1024

F32 = jnp.float32
BF16 = jnp.bfloat16


def _dot(a, b):
    return jnp.dot(a, b, preferred_element_type=F32)


def _layer_norm(z, g, b):
    mu = jnp.mean(z, axis=-1, keepdims=True)
    zc = z - mu
    var = jnp.mean(zc * zc, axis=-1, keepdims=True)
    return zc * lax.rsqrt(var + LN_EPS) * g + b


def _block_diag_dot(xb, w_ref):
    n = w_ref.shape[0]
    return jnp.concatenate(
        [_dot(xb[:, i * MXU_DIM:(i + 1) * MXU_DIM], w_ref[i]) for i in range(n)], axis=1)


def _causal_conv(u, hist_ref, part, conv_w, conv_b):
    pb, ts, c = u.shape
    ext = jnp.concatenate([hist_ref[part], u], axis=1)
    hist_ref[part] = u[:, ts - CONV_HIST:, :]
    uc = jnp.broadcast_to(conv_b.reshape(1, 1, c), (pb, ts, c))
    for k in range(CONV_WIDTH):
        off = CONV_HIST - (CONV_WIDTH - 1) + k
        uc = uc + conv_w[k:k + 1, :].reshape(1, 1, c) * ext[:, off:off + ts, :]
    return uc


def _lru_scan(uc, ga, gx, lam, carry_ref, part, t0):
    pb, ts, c = uc.shape
    r = jax.nn.sigmoid(ga)
    i_gate = jax.nn.sigmoid(gx)
    softplus_neg_lam = jnp.maximum(-lam, 0.0) + jnp.log1p(jnp.exp(-jnp.abs(lam)))
    log_a = (-LRU_C) * r * softplus_neg_lam
    a = jnp.exp(log_a)
    mult = jnp.sqrt(jnp.tanh(-log_a) * (1.0 + a * a))
    t_local = lax.broadcasted_iota(jnp.int32, (pb, ts, c), 1)
    first = (t_local + t0) == 0
    a3 = a.reshape(pb, ts, c)
    x3 = jnp.where(first, 1.0, mult.reshape(pb, ts, c)) * i_gate.reshape(pb, ts, c) * uc

    shift = 1
    while shift < ts:
        valid = t_local >= shift
        a_prev = pltpu.roll(a3, shift, axis=1)
        x_prev = pltpu.roll(x3, shift, axis=1)
        x3 = jnp.where(valid, a3 * x_prev, 0.0) + x3
        a3 = jnp.where(valid, a3 * a_prev, a3)
        shift *= 2
    h = x3 + a3 * carry_ref[part][:, None, :]
    carry_ref[part] = h[:, ts - 1, :]
    return h


def _multiscale_pool(v, hist_ref, part, t0):
    pb, ts, c = v.shape
    ext = jnp.concatenate([hist_ref[part], v], axis=1)
    hist_ref[part] = v[:, ts - POOL_HIST:, :]
    t_glob = lax.broadcasted_iota(jnp.int32, (1, ts, POOL_GROUP_DIM), 1) + t0
    pooled = []
    for g, win in enumerate(POOL_WINDOWS):
        lanes = slice(g * POOL_GROUP_DIM, (g + 1) * POOL_GROUP_DIM)
        s = ext[:, :, lanes]
        shift = 1
        while shift < win:
            s = s + pltpu.roll(s, shift, axis=1)
            shift *= 2
        count = jnp.minimum(t_glob + 1, win).astype(F32)
        pooled.append(s[:, POOL_HIST:, :] / count - v[:, :, lanes])
    return jnp.concatenate(pooled, axis=-1)


def _mixer_kernel(x_ref, w_in_ref, conv_w_ref, conv_b_ref, wa_ref, ba_ref, wx_ref, bx_ref, lam_ref,
                  pool_w_ref, pool_scale_ref, w_out_ref, g1_ref, b1_ref, rw_hi_ref, rw_lo_ref,
                  x1_ref, x1p_ref, logit_ref, hist_u, hist_v, carry):
    j = pl.program_id(0)
    nb, ts, d = x_ref.shape
    c = hist_u.shape[-1]
    pb = nb // MIXER_PARTS
    rows = pb * ts
    t0 = j * ts
    parts = [slice(p * pb, (p + 1) * pb) for p in range(MIXER_PARTS)]

    @pl.when(j == 0)
    def _():
        hist_u[...] = jnp.zeros_like(hist_u)
        hist_v[...] = jnp.zeros_like(hist_v)
        carry[...] = jnp.zeros_like(carry)

    xs = [x_ref[p].reshape(rows, d) for p in parts]
    projs = [_dot(x.astype(BF16), w_in_ref[...]) for x in xs]
    ucs = [_causal_conv(proj[:, :c].reshape(pb, ts, c), hist_u, p, conv_w_ref[...], conv_b_ref[...])
           for proj, p in zip(projs, parts)]
    ucbs = [uc.reshape(rows, c).astype(BF16) for uc in ucs]
    gas = [_block_diag_dot(ucb, wa_ref) + ba_ref[...] for ucb in ucbs]
    gxs = [_block_diag_dot(ucb, wx_ref) + bx_ref[...] for ucb in ucbs]
    hs = [_lru_scan(uc, ga, gx, lam_ref[...], carry, p, t0) for uc, ga, gx, p in zip(ucs, gas, gxs, parts)]
    y_lrus = []
    for h, proj in zip(hs, projs):
        gate = proj[:, c:2 * c]
        gelu_gate = 0.5 * gate * (1.0 + jnp.tanh(math.sqrt(2.0 / math.pi) * (gate + 0.044715 * gate * gate * gate)))
        y_lrus.append(h.reshape(rows, c) * gelu_gate)
    pooled = [_multiscale_pool(proj[:, 2 * c:].reshape(pb, ts, c), hist_v, p, t0).reshape(rows, c).astype(BF16)
              for proj, p in zip(projs, parts)]
    y_pools = [_block_diag_dot(pl_, pool_w_ref) * pool_scale_ref[...] for pl_ in pooled]
    mixes = [_dot(jnp.concatenate([y_lru, y_pool], axis=1).astype(BF16), w_out_ref[...])
             for y_lru, y_pool in zip(y_lrus, y_pools)]
    x1s = [_layer_norm(DEEPNORM_ALPHA * x + mix, g1_ref[...], b1_ref[...]) for x, mix in zip(xs, mixes)]

    for p, x1 in zip(parts, x1s):
        x1_ref[p] = x1.reshape(pb, ts, d)
        hi = x1.astype(BF16)
        hi_f = hi.astype(F32)
        lo = (x1 - hi_f).astype(BF16)
        logits = _dot(hi, rw_hi_ref[...]) + _dot(lo, rw_hi_ref[...]) + _dot(hi, rw_lo_ref[...])
        logit_ref[p] = logits.reshape(pb, ts, logits.shape[-1])
        bits = lax.bitcast_convert_type(hi_f, jnp.uint32)
        packed = bits[:, :d // 2] | (bits[:, d // 2:] >> 16)
        x1p_ref[p] = packed.reshape(pb, ts, d // 2)


def _const_spec(shape):
    return pl.BlockSpec(shape, lambda j: (0,) * len(shape), pipeline_mode=pl.Buffered(1))


def _regroup_block_diag(w, width):
    heads, hd, _ = w.shape
    per = width // hd
    w = w.reshape(heads // per, per, hd, hd)
    eye = jnp.eye(per, dtype=w.dtype)
    return jnp.einsum("gpij,pq->gpiqj", w, eye).reshape(heads // per, width, width)


def _mixer(x, w_in, conv_w, conv_b, wa, ba, wx, bx, lam, pool_w, pool_scale, w_out, g1, b1, router_w, ts=64):
    nb, s, d = x.shape
    c = conv_w.shape[-1]
    n_exp = router_w.shape[-1]
    rw_hi = router_w.astype(BF16)
    rw_lo = (router_w - rw_hi.astype(F32)).astype(BF16)
    row = lambda p: p.reshape(1, -1)
    operands = (
        x, w_in.astype(BF16), conv_w, row(conv_b),
        _regroup_block_diag(wa, MXU_DIM).astype(BF16), row(ba),
        _regroup_block_diag(wx, MXU_DIM).astype(BF16), row(bx), row(lam),
        _regroup_block_diag(pool_w, MXU_DIM).astype(BF16), row(pool_scale),
        w_out.astype(BF16), row(g1), row(b1), rw_hi, rw_lo)
    tile = lambda width: pl.BlockSpec((nb, ts, width), lambda j: (0, j, 0))
    in_specs = [tile(d)] + [_const_spec(op.shape) for op in operands[1:]]
    return pl.pallas_call(
        _mixer_kernel,
        grid=(s // ts,),
        in_specs=in_specs,
        out_specs=[tile(d), tile(d // 2), tile(n_exp)],
        out_shape=[jax.ShapeDtypeStruct((nb, s, d), F32),
                   jax.ShapeDtypeStruct((nb, s, d // 2), jnp.uint32),
                   jax.ShapeDtypeStruct((nb, s, n_exp), F32)],
        scratch_shapes=[pltpu.VMEM((nb, CONV_HIST, c), F32),
                        pltpu.VMEM((nb, POOL_HIST, c), F32),
                        pltpu.VMEM((nb, c), F32)],
        compiler_params=pltpu.CompilerParams(dimension_semantics=("arbitrary",),
                                             vmem_limit_bytes=VMEM_LIMIT),
        name="mixer_ln_router",
    )(*operands)


def _first_row_of(cond, rows_rev, n):
    return n - jnp.max(jnp.where(cond, rows_rev, 0.0), axis=0, keepdims=True)


def _route_kernel(logit_ref, bias_ref, idx_ref, gate_ref, rank_ref, count_ref, count_scr):
    i = pl.program_id(0)

    @pl.when(i == 0)
    def _():
        count_scr[...] = jnp.zeros_like(count_scr)

    scores = jax.nn.sigmoid(logit_ref[...].T)
    n_exp, tn = scores.shape
    gsz = n_exp // N_EXPERT_GROUPS
    neg = -jnp.inf
    biased = scores + bias_ref[...]
    row = lax.broadcasted_iota(jnp.int32, (n_exp, tn), 0).astype(F32)
    row_rev = n_exp - row

    g_row = lax.broadcasted_iota(jnp.int32, (N_EXPERT_GROUPS, tn), 0)
    group_score = jnp.zeros((N_EXPERT_GROUPS, tn), F32)
    grp_row = lax.broadcasted_iota(jnp.int32, (gsz, tn), 0).astype(F32)
    grp_rev = gsz - grp_row
    for g in range(N_EXPERT_GROUPS):
        blk = biased[g * gsz:(g + 1) * gsz]
        m1 = jnp.max(blk, axis=0, keepdims=True)
        i1 = _first_row_of(blk == m1, grp_rev, gsz)
        m2 = jnp.max(jnp.where(grp_row == i1, neg, blk), axis=0, keepdims=True)
        group_score = jnp.where(g_row == g, m1 + m2, group_score)

    beaten_by = jnp.zeros((N_EXPERT_GROUPS, tn), F32)
    for g in range(N_EXPERT_GROUPS):
        other = jnp.max(jnp.where(g_row == g, group_score, neg), axis=0, keepdims=True)
        wins = (other > group_score) | ((other == group_score) & (g < g_row))
        beaten_by = beaten_by + jnp.where(wins, 1.0, 0.0)
    masked = []
    for g in range(N_EXPERT_GROUPS):
        beaten_g = jnp.max(jnp.where(g_row == g, beaten_by, 0.0), axis=0, keepdims=True)
        masked.append(jnp.where(beaten_g < TOPK_GROUPS, biased[g * gsz:(g + 1) * gsz], neg))
    masked = jnp.concatenate(masked, axis=0)

    k_row = lax.broadcasted_iota(jnp.int32, (TOP_K, tn), 0)
    selected = jnp.zeros((n_exp, tn), F32)
    idx = jnp.zeros((TOP_K, tn), F32)
    gate = jnp.zeros((TOP_K, tn), F32)
    hits = []
    for k in range(TOP_K):
        m = jnp.max(masked, axis=0, keepdims=True)
        ik = _first_row_of(masked == m, row_rev, n_exp)
        hit = row == ik
        gk = jnp.sum(jnp.where(hit, scores, 0.0), axis=0, keepdims=True)
        masked = jnp.where(hit, neg, masked)
        selected = jnp.where(hit, 1.0, selected)
        idx = jnp.where(k_row == k, ik, idx)
        gate = jnp.where(k_row == k, gk, gate)
        hits.append(hit)
    gate = gate / jnp.sum(gate, axis=0, keepdims=True) * ROUTED_SCALE

    earlier = (lax.broadcasted_iota(jnp.int32, (tn, tn), 0) < lax.broadcasted_iota(jnp.int32, (tn, tn), 1))
    before = _dot(selected.astype(BF16), jnp.where(earlier, 1.0, 0.0).astype(BF16)) + count_scr[...]
    rank = jnp.zeros((TOP_K, tn), F32)
    for k in range(TOP_K):
        rk = jnp.sum(jnp.where(hits[k], before, 0.0), axis=0, keepdims=True)
        rank = jnp.where(k_row == k, rk, rank)
    count_scr[...] = count_scr[...] + jnp.sum(selected, axis=1, keepdims=True)

    bits = lax.bitcast_convert_type(gate.astype(BF16).astype(F32), jnp.uint32)
    words = lax.bitcast_convert_type(bits | (bits >> 16), jnp.int32)
    chunk = idx_ref.shape[-1]
    for c in range(idx_ref.shape[0]):
        lanes = slice(c * chunk, (c + 1) * chunk)
        idx_ref[c] = idx[:, lanes].astype(jnp.int32)
        rank_ref[c] = rank[:, lanes].astype(jnp.int32)
        gate_ref[c] = words[:, lanes]
    count_ref[...] = count_scr[...].astype(jnp.int32)


def _route(logits, bias, tn=512):
    t, n_exp = logits.shape
    chunked = pl.BlockSpec((tn // SC_CHUNK, TOP_K, SC_CHUNK), lambda i: (i, 0, 0))
    chunked_shape = jax.ShapeDtypeStruct((t // SC_CHUNK, TOP_K, SC_CHUNK), jnp.int32)
    return pl.pallas_call(
        _route_kernel,
        grid=(t // tn,),
        in_specs=[pl.BlockSpec((tn, n_exp), lambda i: (i, 0)),
                  pl.BlockSpec((n_exp, 1), lambda i: (0, 0))],
        out_specs=[chunked, chunked, chunked, pl.BlockSpec((n_exp, 1), lambda i: (0, 0))],
        out_shape=[chunked_shape, chunked_shape, chunked_shape, jax.ShapeDtypeStruct((n_exp, 1), jnp.int32)],
        scratch_shapes=[pltpu.VMEM((n_exp, 1), F32)],
        compiler_params=pltpu.CompilerParams(dimension_semantics=("arbitrary",),
                                             vmem_limit_bytes=VMEM_LIMIT),
        name="route_topk_rank",
    )(logits, bias.reshape(n_exp, 1))


def _unpack_bf16_pairs(p):
    hi = lax.bitcast_convert_type(p & jnp.uint32(0xFFFF0000), F32)
    lo = lax.bitcast_convert_type(p << 16, F32)
    return jnp.concatenate([hi, lo], axis=1)


def _pack_bf16_pairs(y):
    n = y.shape[1] // 2
    bits = lax.bitcast_convert_type(y.astype(BF16).astype(F32), jnp.uint32)
    return bits[:, :n] | (bits[:, n:] >> 16)


def _sc_worker_layout(n_chunks_total):
    info = plsc.get_sparse_core_info()
    n_workers = info.num_cores * info.num_subcores
    return info.num_cores, n_chunks_total // n_workers


def _sc_dispatch(idx, rank, expert_start, x1p, n_rows):
    n_chunks_total, n_k, chunk = idx.shape
    t, w = x1p.shape
    n_cores, n_chunks = _sc_worker_layout(n_chunks_total)
    lanes = plsc.get_sparse_core_info().num_lanes
    mesh = plsc.VectorSubcoreMesh(core_axis_name="c", subcore_axis_name="s")
    table = pltpu.VMEM((n_k, chunk), jnp.int32)

    @functools.partial(
        pl.kernel, mesh=mesh, name="sc_dispatch_rows",
        compiler_params=pltpu.CompilerParams(needs_layout_passes=False),
        out_type=[jax.ShapeDtypeStruct((n_rows, w), x1p.dtype), jax.ShapeDtypeStruct(idx.shape, jnp.int32)],
        scratch_types=[table, table, pltpu.VMEM(expert_start.shape, jnp.int32), pltpu.VMEM((chunk, w), x1p.dtype),
                       pltpu.SemaphoreType.DMA, pltpu.SemaphoreType.DMA])
    def dispatch(idx_hbm, rank_hbm, start_hbm, x_hbm, xg_hbm, slots_hbm, slot_v, rank_v, start_v, rows_v, sem,
                 row_sem):
        wid = lax.axis_index("s") * n_cores + lax.axis_index("c")
        pltpu.sync_copy(start_hbm, start_v)

        @pl.loop(0, n_chunks)
        def _(ci):
            chunk_id = wid * n_chunks + ci
            rows_in = pltpu.make_async_copy(x_hbm.at[pl.ds(chunk_id * chunk, chunk)], rows_v, row_sem)
            rows_in.start()
            pltpu.sync_copy(idx_hbm.at[chunk_id], slot_v)
            pltpu.sync_copy(rank_hbm.at[chunk_id], rank_v)
            for k in range(n_k):
                for j in range(chunk // lanes):
                    part = pl.ds(j * lanes, lanes)
                    slot_v[k, part] = plsc.load_gather(start_v, [slot_v[k, part]]) + rank_v[k, part]
            rows_in.wait()
            copies = [pltpu.make_async_copy(rows_v, xg_hbm.at[slot_v.at[k]], sem) for k in range(n_k)]
            for cp in copies:
                cp.start()
            pltpu.sync_copy(slot_v, slots_hbm.at[chunk_id])
            for cp in copies:
                cp.wait()

    return dispatch(idx, rank, expert_start, x1p)


def _sc_combine(slots, gate_words, yg):
    n_chunks_total, n_k, chunk = slots.shape
    w = yg.shape[1]
    t = n_chunks_total * chunk
    n_cores, n_chunks = _sc_worker_layout(n_chunks_total)
    lanes = plsc.get_sparse_core_info().num_lanes
    n_sub = chunk // SC_SUB
    mesh = plsc.VectorSubcoreMesh(core_axis_name="c", subcore_axis_name="s")
    rows_t = pltpu.VMEM((n_k, SC_SUB, w), jnp.int32)
    out_t = pltpu.VMEM((SC_SUB, w), jnp.int32)
    dma = pltpu.SemaphoreType.DMA

    @functools.partial(
        pl.kernel, mesh=mesh, name="sc_combine_rows",
        compiler_params=pltpu.CompilerParams(needs_layout_passes=False),
        out_type=jax.ShapeDtypeStruct((t, w), jnp.int32),
        scratch_types=[pltpu.VMEM((n_k, chunk), jnp.int32), pltpu.VMEM((n_k, chunk), jnp.int32),
                       rows_t, rows_t, out_t, out_t, dma, dma, dma, dma])
    def combine(slots_hbm, gate_hbm, yg_hbm, out_hbm, idx_v, gate_v, rows0, rows1, out0, out1,
                row_sem0, row_sem1, out_sem0, out_sem1):
        wid = lax.axis_index("s") * n_cores + lax.axis_index("c")
        rows, outs = (rows0, rows1), (out0, out1)
        row_sems, out_sems = (row_sem0, row_sem1), (out_sem0, out_sem1)

        def gathers(sub, b):
            return [pltpu.make_async_copy(yg_hbm.at[idx_v.at[k, pl.ds(sub * SC_SUB, SC_SUB)]], rows[b].at[k],
                                          row_sems[b]) for k in range(n_k)]

        @pl.loop(0, n_chunks)
        def _(ci):
            chunk_id = wid * n_chunks + ci
            tok0 = chunk_id * chunk
            pltpu.sync_copy(slots_hbm.at[chunk_id], idx_v)
            pltpu.sync_copy(gate_hbm.at[chunk_id], gate_v)

            def store(sub, b):
                return pltpu.make_async_copy(outs[b], out_hbm.at[pl.ds(tok0 + sub * SC_SUB, SC_SUB)], out_sems[b])

            for cp in gathers(0, 0):
                cp.start()

            @pl.loop(0, n_sub // 2)
            def _(pair):
                for b in (0, 1):
                    sub = 2 * pair + b

                    @pl.when(sub + 1 < n_sub)
                    def _():
                        for cp in gathers(sub + 1, 1 - b):
                            cp.start()

                    for cp in gathers(sub, b):
                        cp.wait()

                    @pl.when(sub >= 2)
                    def _():
                        store(sub - 2, b).wait()

                    buf, out = rows[b], outs[b]

                    @pl.loop(0, SC_SUB)
                    def _(tt):
                        col = jnp.full((lanes,), sub * SC_SUB + tt, jnp.int32)
                        gates = [plsc.bitcast(plsc.load_gather(gate_v, [jnp.full((lanes,), k, jnp.int32), col]), BF16)
                                 for k in range(n_k)]
                        for v in range(w // lanes):
                            terms = [gates[k] * plsc.bitcast(buf[k, tt, pl.ds(v * lanes, lanes)], BF16)
                                     for k in range(n_k)]
                            while len(terms) > 1:
                                terms = [terms[i] + terms[i + 1] for i in range(0, len(terms), 2)]
                            out[tt, pl.ds(v * lanes, lanes)] = plsc.bitcast(terms[0], jnp.int32)

                    store(sub, b).start()

            store(n_sub - 2, 0).wait()
            store(n_sub - 1, 1).wait()

    return combine(slots, gate_words, yg)


def _expert_kernel(first_block, n_block, n_used, xg_hbm, w1_ref, w3_ref, w2_ref, yg_hbm,
                   x_buf, y_buf, w1_bf, w3_bf, w2_bf, in_sem, out_sem):
    e = pl.program_id(0)
    n_slot, bm, _ = x_buf.shape
    ahead = n_slot - BLOCK_GROUP
    total = n_used[0]

    def fetch(g):
        slot = g % n_slot
        return pltpu.make_async_copy(xg_hbm.at[pl.ds(g * bm, bm)], x_buf.at[slot], in_sem.at[slot])

    def write_back(g):
        slot = g % n_slot
        return pltpu.make_async_copy(y_buf.at[slot], yg_hbm.at[pl.ds(g * bm, bm)], out_sem.at[slot])

    @pl.when(e == 0)
    def _():
        for g in range(ahead):
            @pl.when(g < total)
            def _():
                fetch(g).start()

    def cast_weights():
        ws = [w1_ref[0].astype(BF16), w3_ref[0].astype(BF16), w2_ref[0].astype(BF16)]
        for ref, wb in zip((w1_bf, w3_bf, w2_bf), ws):
            ref[...] = wb
        return ws

    def swiglu(slots, weights):
        w1b, w3b, w2b = weights
        xs = [_unpack_bf16_pairs(x_buf[s]).astype(BF16) for s in slots]
        up = [(_dot(xb, w1b), _dot(xb, w3b)) for xb in xs]
        hs = [(h1 * jax.nn.sigmoid(h1) * h3).astype(BF16) for h1, h3 in up]
        ys = [_dot(h, w2b) for h in hs]
        return [lax.bitcast_convert_type(_pack_bf16_pairs(y), y_buf.dtype) for y in ys]

    def process(blocks, cast_here=False):
        for g in blocks:
            fetch(g).wait()
        for g in blocks:
            @pl.when(g + ahead < total)
            def _():
                fetch(g + ahead).start()
        weights = cast_weights() if cast_here else (w1_bf[...], w3_bf[...], w2_bf[...])
        ys = swiglu([g % n_slot for g in blocks], weights)
        for g in blocks:
            @pl.when(g >= n_slot)
            def _():
                write_back(g - n_slot).wait()
        for g, y in zip(blocks, ys):
            y_buf[g % n_slot] = y
            write_back(g).start()

    n_full = n_block[e] // BLOCK_GROUP

    @pl.when(n_full > 0)
    def _():
        process([first_block[e] + j for j in range(BLOCK_GROUP)], cast_here=True)

    @pl.when((n_full == 0) & (n_block[e] > 0))
    def _():
        cast_weights()

    @pl.loop(1, n_full)
    def _(i):
        process([first_block[e] + i * BLOCK_GROUP + j for j in range(BLOCK_GROUP)])

    done = n_full * BLOCK_GROUP
    size = BLOCK_GROUP // 2
    while size >= 1:
        left = n_block[e] - done
        start = first_block[e] + done

        @pl.when(left >= size)
        def _(start=start, size=size):
            process([start + j for j in range(size)])

        done = done + left // size * size
        size //= 2

    @pl.when(e == pl.num_programs(0) - 1)
    def _():
        for back in range(1, n_slot + 1):
            @pl.when(total >= back)
            def _():
                write_back(total - back).wait()


def _experts(first_block, n_block, n_used, xg, w1, w3, w2, bm):
    n_rows, w = xg.shape
    n_exp, d, de = w1.shape
    weight = lambda e, *_: (e, 0, 0)
    return pl.pallas_call(
        _expert_kernel,
        grid_spec=pltpu.PrefetchScalarGridSpec(
            num_scalar_prefetch=3,
            grid=(n_exp,),
            in_specs=[pl.BlockSpec(memory_space=pl.ANY),
                      pl.BlockSpec((1, d, de), weight),
                      pl.BlockSpec((1, d, de), weight),
                      pl.BlockSpec((1, de, d), weight)],
            out_specs=pl.BlockSpec(memory_space=pl.ANY),
            scratch_shapes=[pltpu.VMEM((ROW_RING, bm, w), xg.dtype), pltpu.VMEM((ROW_RING, bm, w), jnp.int32),
                            pltpu.VMEM((d, de), BF16), pltpu.VMEM((d, de), BF16), pltpu.VMEM((de, d), BF16),
                            pltpu.SemaphoreType.DMA((ROW_RING,)), pltpu.SemaphoreType.DMA((ROW_RING,))]),
        out_shape=jax.ShapeDtypeStruct((n_rows, w), jnp.int32),
        compiler_params=pltpu.CompilerParams(dimension_semantics=("arbitrary",),
                                             vmem_limit_bytes=VMEM_LIMIT),
        name="routed_experts",
    )(first_block, n_block, n_used, xg, w1, w3, w2)


def _combine_kernel(y_ref, x1_ref, sw1_ref, sw3_ref, sw2_ref, g2_ref, b2_ref, out_ref):
    x1 = x1_ref[...]
    xb = x1.astype(BF16)
    h1 = _dot(xb, sw1_ref[...])
    h3 = _dot(xb, sw3_ref[...])
    ffn = _dot((h1 * jax.nn.sigmoid(h1) * h3).astype(BF16), sw2_ref[...])
    ffn = ffn + _unpack_bf16_pairs(lax.bitcast_convert_type(y_ref[...], jnp.uint32))
    out_ref[...] = _layer_norm(DEEPNORM_ALPHA * x1 + ffn, g2_ref[...], b2_ref[...])


def _combine(y_routed, x1, sw1, sw3, sw2, g2, b2, tn=512):
    t, d = x1.shape
    w = y_routed.shape[1]
    row = lambda p: p.reshape(1, -1)
    operands = (y_routed, x1, sw1.astype(BF16), sw3.astype(BF16), sw2.astype(BF16), row(g2), row(b2))
    in_specs = [pl.BlockSpec((tn, w), lambda i: (i, 0)),
                pl.BlockSpec((tn, d), lambda i: (i, 0))]
    in_specs += [_const_spec(op.shape) for op in operands[2:]]
    return pl.pallas_call(
        _combine_kernel,
        grid=(t // tn,),
        in_specs=in_specs,
        out_specs=pl.BlockSpec((tn, d), lambda i: (i, 0)),
        out_shape=jax.ShapeDtypeStruct((t, d), F32),
        compiler_params=pltpu.CompilerParams(dimension_semantics=("arbitrary",),
                                             vmem_limit_bytes=VMEM_LIMIT),
        name="combine_shared_ln",
    )(*operands)


def _block_table(counts, bm):
    n_block = (counts + bm - 1) // bm
    ends = jnp.cumsum(n_block)
    first_block = ends - n_block
    i32 = lambda v: v.astype(jnp.int32)
    return i32(first_block * bm), i32(first_block), i32(n_block), i32(ends[-1:])


def _moe(x1, x1p, logits, router_bias, w1, w3, w2, sw1, sw3, sw2, g2, b2, bm=256):
    t, d = x1.shape
    n_exp = logits.shape[1]
    idx, gate_words, rank, counts = _route(logits, router_bias)
    n_blocks = (t * TOP_K) // bm + n_exp
    starts, first_block, n_block, n_used = _block_table(counts[:, 0], bm)
    xg, slots = _sc_dispatch(idx, rank, starts, x1p, n_blocks * bm)
    yg = _experts(first_block, n_block, n_used, xg, w1, w3, w2, bm)
    y_routed = _sc_combine(slots, gate_words, yg)
    return _combine(y_routed, x1, sw1, sw3, sw2, g2, b2)


def kernel(x, w_in, conv_w, conv_b, lru_wa, lru_ba, lru_wx, lru_bx, lru_lambda, pool_w, pool_scale, w_out, ln1_g, ln1_b, router_w, router_bias, exp_w1, exp_w3, exp_w2, sh_w1, sh_w3, sh_w2, ln2_g, ln2_b):
    nb, s, d = x.shape
    for l in range(DEPTH):
        x1, x1p, logits = _mixer(x, w_in[l], conv_w[l], conv_b[l], lru_wa[l], lru_ba[l], lru_wx[l], lru_bx[l],
                                 lru_lambda[l], pool_w[l], pool_scale[l], w_out[l], ln1_g[l], ln1_b[l],
                                 router_w[l])
        t = nb * s
        x = _moe(x1.reshape(t, d), x1p.reshape(t, d // 2), logits.reshape(t, -1), router_bias[l],
                 exp_w1[l], exp_w3[l], exp_w2[l], sh_w1[l], sh_w3[l], sh_w2[l], ln2_g[l], ln2_b[l])
        x = x.reshape(nb, s, d)
    return x
```

```python
import functools
import math

import jax
import jax.numpy as jnp
from jax import lax
from jax.experimental import pallas as pl
from jax.experimental.pallas import tpu as pltpu
from jax.experimental.pallas import tpu_sc as plsc

LRU_HEADS = 8
CONV_WIDTH = 4
LRU_C = 8.0
POOL_WINDOWS = (2, 4, 8, 16)
N_EXPERT_GROUPS = 8
TOPK_GROUPS = 4
TOP_K = 8
ROUTED_SCALE = 2.5
LN_EPS = 1e-5
DEPTH = 1
DEEPNORM_ALPHA = (2.0 * DEPTH) ** 0.25

MXU_DIM = 256
POOL_GROUP_DIM = 128
CONV_HIST = 8
POOL_HIST = 16
SC_CHUNK = 128
SC_SUB = 8
MIXER_PARTS = 4
BLOCK_GROUP = 4
ROW_RING = 8
VMEM_LIMIT = 56 * 1024 * 1024

F32 = jnp.float32
BF16 = jnp.bfloat16


def _dot(a, b):
    return jnp.dot(a, b, preferred_element_type=F32)


def _layer_norm(z, g, b):
    mu = jnp.mean(z, axis=-1, keepdims=True)
    zc = z - mu
    var = jnp.mean(zc * zc, axis=-1, keepdims=True)
    return zc * lax.rsqrt(var + LN_EPS) * g + b


def _block_diag_dot(xb, w_ref):
    n = w_ref.shape[0]
    return jnp.concatenate(
        [_dot(xb[:, i * MXU_DIM:(i + 1) * MXU_DIM], w_ref[i]) for i in range(n)], axis=1)


def _causal_conv(u, hist_ref, part, conv_w, conv_b):
    pb, ts, c = u.shape
    ext = jnp.concatenate([hist_ref[part], u], axis=1)
    hist_ref[part] = u[:, ts - CONV_HIST:, :]
    uc = jnp.broadcast_to(conv_b.reshape(1, 1, c), (pb, ts, c))
    for k in range(CONV_WIDTH):
        off = CONV_HIST - (CONV_WIDTH - 1) + k
        uc = uc + conv_w[k:k + 1, :].reshape(1, 1, c) * ext[:, off:off + ts, :]
    return uc


def _lru_scan(uc, ga, gx, lam, carry_ref, part, t0):
    pb, ts, c = uc.shape
    r = jax.nn.sigmoid(ga)
    i_gate = jax.nn.sigmoid(gx)
    softplus_neg_lam = jnp.maximum(-lam, 0.0) + jnp.log1p(jnp.exp(-jnp.abs(lam)))
    log_a = (-LRU_C) * r * softplus_neg_lam
    a = jnp.exp(log_a)
    mult = jnp.sqrt(jnp.tanh(-log_a) * (1.0 + a * a))
    t_local = lax.broadcasted_iota(jnp.int32, (pb, ts, c), 1)
    first = (t_local + t0) == 0
    a3 = a.reshape(pb, ts, c)
    x3 = jnp.where(first, 1.0, mult.reshape(pb, ts, c)) * i_gate.reshape(pb, ts, c) * uc

    shift = 1
    while shift < ts:
        valid = t_local >= shift
        a_prev = pltpu.roll(a3, shift, axis=1)
        x_prev = pltpu.roll(x3, shift, axis=1)
        x3 = jnp.where(valid, a3 * x_prev, 0.0) + x3
        a3 = jnp.where(valid, a3 * a_prev, a3)
        shift *= 2
    h = x3 + a3 * carry_ref[part][:, None, :]
    carry_ref[part] = h[:, ts - 1, :]
    return h


def _multiscale_pool(v, hist_ref, part, t0):
    pb, ts, c = v.shape
    ext = jnp.concatenate([hist_ref[part], v], axis=1)
    hist_ref[part] = v[:, ts - POOL_HIST:, :]
    t_glob = lax.broadcasted_iota(jnp.int32, (1, ts, POOL_GROUP_DIM), 1) + t0
    pooled = []
    for g, win in enumerate(POOL_WINDOWS):
        lanes = slice(g * POOL_GROUP_DIM, (g + 1) * POOL_GROUP_DIM)
        s = ext[:, :, lanes]
        shift = 1
        while shift < win:
            s = s + pltpu.roll(s, shift, axis=1)
            shift *= 2
        count = jnp.minimum(t_glob + 1, win).astype(F32)
        pooled.append(s[:, POOL_HIST:, :] / count - v[:, :, lanes])
    return jnp.concatenate(pooled, axis=-1)


def _mixer_kernel(x_ref, w_in_ref, conv_w_ref, conv_b_ref, wa_ref, ba_ref, wx_ref, bx_ref, lam_ref,
                  pool_w_ref, pool_scale_ref, w_out_ref, g1_ref, b1_ref, rw_hi_ref, rw_lo_ref,
                  x1_ref, x1p_ref, logit_ref, hist_u, hist_v, carry):
    j = pl.program_id(0)
    nb, ts, d = x_ref.shape
    c = hist_u.shape[-1]
    pb = nb // MIXER_PARTS
    rows = pb * ts
    t0 = j * ts
    parts = [slice(p * pb, (p + 1) * pb) for p in range(MIXER_PARTS)]

    @pl.when(j == 0)
    def _():
        hist_u[...] = jnp.zeros_like(hist_u)
        hist_v[...] = jnp.zeros_like(hist_v)
        carry[...] = jnp.zeros_like(carry)

    xs = [x_ref[p].reshape(rows, d) for p in parts]
    projs = [_dot(x.astype(BF16), w_in_ref[...]) for x in xs]
    ucs = [_causal_conv(proj[:, :c].reshape(pb, ts, c), hist_u, p, conv_w_ref[...], conv_b_ref[...])
           for proj, p in zip(projs, parts)]
    ucbs = [uc.reshape(rows, c).astype(BF16) for uc in ucs]
    gas = [_block_diag_dot(ucb, wa_ref) + ba_ref[...] for ucb in ucbs]
    gxs = [_block_diag_dot(ucb, wx_ref) + bx_ref[...] for ucb in ucbs]
    hs = [_lru_scan(uc, ga, gx, lam_ref[...], carry, p, t0) for uc, ga, gx, p in zip(ucs, gas, gxs, parts)]
    y_lrus = []
    for h, proj in zip(hs, projs):
        gate = proj[:, c:2 * c]
        gelu_gate = 0.5 * gate * (1.0 + jnp.tanh(math.sqrt(2.0 / math.pi) * (gate + 0.044715 * gate * gate * gate)))
        y_lrus.append(h.reshape(rows, c) * gelu_gate)
    pooled = [_multiscale_pool(proj[:, 2 * c:].reshape(pb, ts, c), hist_v, p, t0).reshape(rows, c).astype(BF16)
              for proj, p in zip(projs, parts)]
    y_pools = [_block_diag_dot(pl_, pool_w_ref) * pool_scale_ref[...] for pl_ in pooled]
    mixes = [_dot(jnp.concatenate([y_lru, y_pool], axis=1).astype(BF16), w_out_ref[...])
             for y_lru, y_pool in zip(y_lrus, y_pools)]
    x1s = [_layer_norm(DEEPNORM_ALPHA * x + mix, g1_ref[...], b1_ref[...]) for x, mix in zip(xs, mixes)]

    for p, x1 in zip(parts, x1s):
        x1_ref[p] = x1.reshape(pb, ts, d)
        hi = x1.astype(BF16)
        hi_f = hi.astype(F32)
        lo = (x1 - hi_f).astype(BF16)
        logits = _dot(hi, rw_hi_ref[...]) + _dot(lo, rw_hi_ref[...]) + _dot(hi, rw_lo_ref[...])
        logit_ref[p] = logits.reshape(pb, ts, logits.shape[-1])
        bits = lax.bitcast_convert_type(hi_f, jnp.uint32)
        packed = bits[:, :d // 2] | (bits[:, d // 2:] >> 16)
        x1p_ref[p] = packed.reshape(pb, ts, d // 2)


def _const_spec(shape):
    return pl.BlockSpec(shape, lambda j: (0,) * len(shape), pipeline_mode=pl.Buffered(1))


def _regroup_block_diag(w, width):
    heads, hd, _ = w.shape
    per = width // hd
    w = w.reshape(heads // per, per, hd, hd)
    eye = jnp.eye(per, dtype=w.dtype)
    return jnp.einsum("gpij,pq->gpiqj", w, eye).reshape(heads // per, width, width)


def _mixer(x, w_in, conv_w, conv_b, wa, ba, wx, bx, lam, pool_w, pool_scale, w_out, g1, b1, router_w, ts=64):
    nb, s, d = x.shape
    c = conv_w.shape[-1]
    n_exp = router_w.shape[-1]
    rw_hi = router_w.astype(BF16)
    rw_lo = (router_w - rw_hi.astype(F32)).astype(BF16)
    row = lambda p: p.reshape(1, -1)
    operands = (
        x, w_in.astype(BF16), conv_w, row(conv_b),
        _regroup_block_diag(wa, MXU_DIM).astype(BF16), row(ba),
        _regroup_block_diag(wx, MXU_DIM).astype(BF16), row(bx), row(lam),
        _regroup_block_diag(pool_w, MXU_DIM).astype(BF16), row(pool_scale),
        w_out.astype(BF16), row(g1), row(b1), rw_hi, rw_lo)
    tile = lambda width: pl.BlockSpec((nb, ts, width), lambda j: (0, j, 0))
    in_specs = [tile(d)] + [_const_spec(op.shape) for op in operands[1:]]
    return pl.pallas_call(
        _mixer_kernel,
        grid=(s // ts,),
        in_specs=in_specs,
        out_specs=[tile(d), tile(d // 2), tile(n_exp)],
        out_shape=[jax.ShapeDtypeStruct((nb, s, d), F32),
                   jax.ShapeDtypeStruct((nb, s, d // 2), jnp.uint32),
                   jax.ShapeDtypeStruct((nb, s, n_exp), F32)],
        scratch_shapes=[pltpu.VMEM((nb, CONV_HIST, c), F32),
                        pltpu.VMEM((nb, POOL_HIST, c), F32),
                        pltpu.VMEM((nb, c), F32)],
        compiler_params=pltpu.CompilerParams(dimension_semantics=("arbitrary",),
                                             vmem_limit_bytes=VMEM_LIMIT),
        name="mixer_ln_router",
    )(*operands)


def _first_row_of(cond, rows_rev, n):
    return n - jnp.max(jnp.where(cond, rows_rev, 0.0), axis=0, keepdims=True)


def _route_kernel(logit_ref, bias_ref, idx_ref, gate_ref, rank_ref, count_ref, count_scr):
    i = pl.program_id(0)

    @pl.when(i == 0)
    def _():
        count_scr[...] = jnp.zeros_like(count_scr)

    scores = jax.nn.sigmoid(logit_ref[...].T)
    n_exp, tn = scores.shape
    gsz = n_exp // N_EXPERT_GROUPS
    neg = -jnp.inf
    biased = scores + bias_ref[...]
    row = lax.broadcasted_iota(jnp.int32, (n_exp, tn), 0).astype(F32)
    row_rev = n_exp - row

    g_row = lax.broadcasted_iota(jnp.int32, (N_EXPERT_GROUPS, tn), 0)
    group_score = jnp.zeros((N_EXPERT_GROUPS, tn), F32)
    grp_row = lax.broadcasted_iota(jnp.int32, (gsz, tn), 0).astype(F32)
    grp_rev = gsz - grp_row
    for g in range(N_EXPERT_GROUPS):
        blk = biased[g * gsz:(g + 1) * gsz]
        m1 = jnp.max(blk, axis=0, keepdims=True)
        i1 = _first_row_of(blk == m1, grp_rev, gsz)
        m2 = jnp.max(jnp.where(grp_row == i1, neg, blk), axis=0, keepdims=True)
        group_score = jnp.where(g_row == g, m1 + m2, group_score)

    beaten_by = jnp.zeros((N_EXPERT_GROUPS, tn), F32)
    for g in range(N_EXPERT_GROUPS):
        other = jnp.max(jnp.where(g_row == g, group_score, neg), axis=0, keepdims=True)
        wins = (other > group_score) | ((other == group_score) & (g < g_row))
        beaten_by = beaten_by + jnp.where(wins, 1.0, 0.0)
    masked = []
    for g in range(N_EXPERT_GROUPS):
        beaten_g = jnp.max(jnp.where(g_row == g, beaten_by, 0.0), axis=0, keepdims=True)
        masked.append(jnp.where(beaten_g < TOPK_GROUPS, biased[g * gsz:(g + 1) * gsz], neg))
    masked = jnp.concatenate(masked, axis=0)

    k_row = lax.broadcasted_iota(jnp.int32, (TOP_K, tn), 0)
    candidates = masked
    idx = jnp.zeros((TOP_K, tn), F32)
    gate = jnp.zeros((TOP_K, tn), F32)
    hits = []
    for k in range(TOP_K):
        m = jnp.max(masked, axis=0, keepdims=True)
        ik = _first_row_of(masked == m, row_rev, n_exp)
        hit = row == ik
        gk = jnp.sum(jnp.where(hit, scores, 0.0), axis=0, keepdims=True)
        masked = jnp.where(hit, neg, masked)
        idx = jnp.where(k_row == k, ik, idx)
        gate = jnp.where(k_row == k, gk, gate)
        hits.append(hit)
    gate = gate / jnp.sum(gate, axis=0, keepdims=True) * ROUTED_SCALE
    selected = jnp.where((masked == neg) & (candidates != neg), 1.0, 0.0)

    earlier = (lax.broadcasted_iota(jnp.int32, (tn, tn), 0) < lax.broadcasted_iota(jnp.int32, (tn, tn), 1))
    before = _dot(selected.astype(BF16), jnp.where(earlier, 1.0, 0.0).astype(BF16)) + count_scr[...]
    rank = jnp.zeros((TOP_K, tn), F32)
    for k in range(TOP_K):
        rk = jnp.sum(jnp.where(hits[k], before, 0.0), axis=0, keepdims=True)
        rank = jnp.where(k_row == k, rk, rank)
    count_scr[...] = count_scr[...] + jnp.sum(selected, axis=1, keepdims=True)

    bits = lax.bitcast_convert_type(gate.astype(BF16).astype(F32), jnp.uint32)
    words = lax.bitcast_convert_type(bits | (bits >> 16), jnp.int32)
    chunk = idx_ref.shape[-1]
    for c in range(idx_ref.shape[0]):
        lanes = slice(c * chunk, (c + 1) * chunk)
        idx_ref[c] = idx[:, lanes].astype(jnp.int32)
        rank_ref[c] = rank[:, lanes].astype(jnp.int32)
        gate_ref[c] = words[:, lanes]
    count_ref[...] = count_scr[...].astype(jnp.int32)


def _route(logits, bias, tn=512):
    t, n_exp = logits.shape
    chunked = pl.BlockSpec((tn // SC_CHUNK, TOP_K, SC_CHUNK), lambda i: (i, 0, 0))
    chunked_shape = jax.ShapeDtypeStruct((t // SC_CHUNK, TOP_K, SC_CHUNK), jnp.int32)
    return pl.pallas_call(
        _route_kernel,
        grid=(t // tn,),
        in_specs=[pl.BlockSpec((tn, n_exp), lambda i: (i, 0)),
                  pl.BlockSpec((n_exp, 1), lambda i: (0, 0))],
        out_specs=[chunked, chunked, chunked, pl.BlockSpec((n_exp, 1), lambda i: (0, 0))],
        out_shape=[chunked_shape, chunked_shape, chunked_shape, jax.ShapeDtypeStruct((n_exp, 1), jnp.int32)],
        scratch_shapes=[pltpu.VMEM((n_exp, 1), F32)],
        compiler_params=pltpu.CompilerParams(dimension_semantics=("arbitrary",),
                                             vmem_limit_bytes=VMEM_LIMIT),
        name="route_topk_rank",
    )(logits, bias.reshape(n_exp, 1))


def _unpack_bf16_pairs(p):
    hi = lax.bitcast_convert_type(p & jnp.uint32(0xFFFF0000), F32)
    lo = lax.bitcast_convert_type(p << 16, F32)
    return jnp.concatenate([hi, lo], axis=1)


def _pack_bf16_pairs(y):
    n = y.shape[1] // 2
    bits = lax.bitcast_convert_type(y.astype(BF16).astype(F32), jnp.uint32)
    return bits[:, :n] | (bits[:, n:] >> 16)


def _sc_worker_layout(n_chunks_total):
    info = plsc.get_sparse_core_info()
    n_workers = info.num_cores * info.num_subcores
    return info.num_cores, n_chunks_total // n_workers


def _sc_dispatch(idx, rank, expert_start, x1p, n_rows):
    n_chunks_total, n_k, chunk = idx.shape
    t, w = x1p.shape
    n_cores, n_chunks = _sc_worker_layout(n_chunks_total)
    lanes = plsc.get_sparse_core_info().num_lanes
    mesh = plsc.VectorSubcoreMesh(core_axis_name="c", subcore_axis_name="s")
    table = pltpu.VMEM((n_k, chunk), jnp.int32)

    @functools.partial(
        pl.kernel, mesh=mesh, name="sc_dispatch_rows",
        compiler_params=pltpu.CompilerParams(needs_layout_passes=False),
        out_type=[jax.ShapeDtypeStruct((n_rows, w), x1p.dtype), jax.ShapeDtypeStruct(idx.shape, jnp.int32)],
        scratch_types=[table, table, pltpu.VMEM(expert_start.shape, jnp.int32), pltpu.VMEM((chunk, w), x1p.dtype),
                       pltpu.SemaphoreType.DMA, pltpu.SemaphoreType.DMA])
    def dispatch(idx_hbm, rank_hbm, start_hbm, x_hbm, xg_hbm, slots_hbm, slot_v, rank_v, start_v, rows_v, sem,
                 row_sem):
        wid = lax.axis_index("s") * n_cores + lax.axis_index("c")
        pltpu.sync_copy(start_hbm, start_v)

        @pl.loop(0, n_chunks)
        def _(ci):
            chunk_id = wid * n_chunks + ci
            rows_in = pltpu.make_async_copy(x_hbm.at[pl.ds(chunk_id * chunk, chunk)], rows_v, row_sem)
            rows_in.start()
            pltpu.sync_copy(idx_hbm.at[chunk_id], slot_v)
            pltpu.sync_copy(rank_hbm.at[chunk_id], rank_v)
            for k in range(n_k):
                for j in range(chunk // lanes):
                    part = pl.ds(j * lanes, lanes)
                    slot_v[k, part] = plsc.load_gather(start_v, [slot_v[k, part]]) + rank_v[k, part]
            rows_in.wait()
            copies = [pltpu.make_async_copy(rows_v, xg_hbm.at[slot_v.at[k]], sem) for k in range(n_k)]
            for cp in copies:
                cp.start()
            pltpu.sync_copy(slot_v, slots_hbm.at[chunk_id])
            for cp in copies:
                cp.wait()

    return dispatch(idx, rank, expert_start, x1p)


def _sc_combine(slots, gate_words, yg):
    n_chunks_total, n_k, chunk = slots.shape
    w = yg.shape[1]
    t = n_chunks_total * chunk
    n_cores, n_chunks = _sc_worker_layout(n_chunks_total)
    lanes = plsc.get_sparse_core_info().num_lanes
    n_sub = chunk // SC_SUB
    mesh = plsc.VectorSubcoreMesh(core_axis_name="c", subcore_axis_name="s")
    rows_t = pltpu.VMEM((n_k, SC_SUB, w), jnp.int32)
    out_t = pltpu.VMEM((SC_SUB, w), jnp.int32)
    dma = pltpu.SemaphoreType.DMA

    @functools.partial(
        pl.kernel, mesh=mesh, name="sc_combine_rows",
        compiler_params=pltpu.CompilerParams(needs_layout_passes=False),
        out_type=jax.ShapeDtypeStruct((t, w), jnp.int32),
        scratch_types=[pltpu.VMEM((n_k, chunk), jnp.int32), pltpu.VMEM((n_k, chunk), jnp.int32),
                       rows_t, rows_t, out_t, out_t, dma, dma, dma, dma])
    def combine(slots_hbm, gate_hbm, yg_hbm, out_hbm, idx_v, gate_v, rows0, rows1, out0, out1,
                row_sem0, row_sem1, out_sem0, out_sem1):
        wid = lax.axis_index("s") * n_cores + lax.axis_index("c")
        rows, outs = (rows0, rows1), (out0, out1)
        row_sems, out_sems = (row_sem0, row_sem1), (out_sem0, out_sem1)

        def gathers(sub, b):
            return [pltpu.make_async_copy(yg_hbm.at[idx_v.at[k, pl.ds(sub * SC_SUB, SC_SUB)]], rows[b].at[k],
                                          row_sems[b]) for k in range(n_k)]

        @pl.loop(0, n_chunks)
        def _(ci):
            chunk_id = wid * n_chunks + ci
            tok0 = chunk_id * chunk
            pltpu.sync_copy(slots_hbm.at[chunk_id], idx_v)
            pltpu.sync_copy(gate_hbm.at[chunk_id], gate_v)

            def store(sub, b):
                return pltpu.make_async_copy(outs[b], out_hbm.at[pl.ds(tok0 + sub * SC_SUB, SC_SUB)], out_sems[b])

            for cp in gathers(0, 0):
                cp.start()

            @pl.loop(0, n_sub // 2)
            def _(pair):
                for b in (0, 1):
                    sub = 2 * pair + b

                    @pl.when(sub + 1 < n_sub)
                    def _():
                        for cp in gathers(sub + 1, 1 - b):
                            cp.start()

                    for cp in gathers(sub, b):
                        cp.wait()

                    @pl.when(sub >= 2)
                    def _():
                        store(sub - 2, b).wait()

                    buf, out = rows[b], outs[b]

                    @pl.loop(0, SC_SUB)
                    def _(tt):
                        col = jnp.full((lanes,), sub * SC_SUB + tt, jnp.int32)
                        gates = [plsc.bitcast(plsc.load_gather(gate_v, [jnp.full((lanes,), k, jnp.int32), col]), BF16)
                                 for k in range(n_k)]
                        for v in range(w // lanes):
                            terms = [gates[k] * plsc.bitcast(buf[k, tt, pl.ds(v * lanes, lanes)], BF16)
                                     for k in range(n_k)]
                            while len(terms) > 1:
                                terms = [terms[i] + terms[i + 1] for i in range(0, len(terms), 2)]
                            out[tt, pl.ds(v * lanes, lanes)] = plsc.bitcast(terms[0], jnp.int32)

                    store(sub, b).start()

            store(n_sub - 2, 0).wait()
            store(n_sub - 1, 1).wait()

    return combine(slots, gate_words, yg)


def _expert_kernel(first_block, n_block, n_used, xg_hbm, w1_ref, w3_ref, w2_ref, yg_hbm,
                   x_buf, y_buf, w1_bf, w3_bf, w2_bf, in_sem, out_sem):
    e = pl.program_id(0)
    n_slot, bm, _ = x_buf.shape
    ahead = n_slot - BLOCK_GROUP
    total = n_used[0]

    def fetch(g):
        slot = g % n_slot
        return pltpu.make_async_copy(xg_hbm.at[pl.ds(g * bm, bm)], x_buf.at[slot], in_sem.at[slot])

    def write_back(g):
        slot = g % n_slot
        return pltpu.make_async_copy(y_buf.at[slot], yg_hbm.at[pl.ds(g * bm, bm)], out_sem.at[slot])

    @pl.when(e == 0)
    def _():
        for g in range(ahead):
            @pl.when(g < total)
            def _():
                fetch(g).start()

    def cast_weights():
        ws = [w1_ref[0].astype(BF16), w3_ref[0].astype(BF16), w2_ref[0].astype(BF16)]
        for ref, wb in zip((w1_bf, w3_bf, w2_bf), ws):
            ref[...] = wb
        return ws

    def swiglu(slots, weights):
        w1b, w3b, w2b = weights
        xs = [_unpack_bf16_pairs(x_buf[s]).astype(BF16) for s in slots]
        up = [(_dot(xb, w1b), _dot(xb, w3b)) for xb in xs]
        hs = [(h1 * jax.nn.sigmoid(h1) * h3).astype(BF16) for h1, h3 in up]
        ys = [_dot(h, w2b) for h in hs]
        return [lax.bitcast_convert_type(_pack_bf16_pairs(y), y_buf.dtype) for y in ys]

    def process(blocks, cast_here=False):
        for g in blocks:
            fetch(g).wait()
        for g in blocks:
            @pl.when(g + ahead < total)
            def _():
                fetch(g + ahead).start()
        weights = cast_weights() if cast_here else (w1_bf[...], w3_bf[...], w2_bf[...])
        ys = swiglu([g % n_slot for g in blocks], weights)
        for g in blocks:
            @pl.when(g >= n_slot)
            def _():
                write_back(g - n_slot).wait()
        for g, y in zip(blocks, ys):
            y_buf[g % n_slot] = y
            write_back(g).start()

    n_full = n_block[e] // BLOCK_GROUP

    @pl.when(n_full > 0)
    def _():
        process([first_block[e] + j for j in range(BLOCK_GROUP)], cast_here=True)

    @pl.when((n_full == 0) & (n_block[e] > 0))
    def _():
        cast_weights()

    @pl.loop(1, n_full)
    def _(i):
        process([first_block[e] + i * BLOCK_GROUP + j for j in range(BLOCK_GROUP)])

    done = n_full * BLOCK_GROUP
    size = BLOCK_GROUP // 2
    while size >= 1:
        left = n_block[e] - done
        start = first_block[e] + done

        @pl.when(left >= size)
        def _(start=start, size=size):
            process([start + j for j in range(size)])

        done = done + left // size * size
        size //= 2

    @pl.when(e == pl.num_programs(0) - 1)
    def _():
        for back in range(1, n_slot + 1):
            @pl.when(total >= back)
            def _():
                write_back(total - back).wait()


def _experts(first_block, n_block, n_used, xg, w1, w3, w2, bm):
    n_rows, w = xg.shape
    n_exp, d, de = w1.shape
    weight = lambda e, *_: (e, 0, 0)
    return pl.pallas_call(
        _expert_kernel,
        grid_spec=pltpu.PrefetchScalarGridSpec(
            num_scalar_prefetch=3,
            grid=(n_exp,),
            in_specs=[pl.BlockSpec(memory_space=pl.ANY),
                      pl.BlockSpec((1, d, de), weight),
                      pl.BlockSpec((1, d, de), weight),
                      pl.BlockSpec((1, de, d), weight)],
            out_specs=pl.BlockSpec(memory_space=pl.ANY),
            scratch_shapes=[pltpu.VMEM((ROW_RING, bm, w), xg.dtype), pltpu.VMEM((ROW_RING, bm, w), jnp.int32),
                            pltpu.VMEM((d, de), BF16), pltpu.VMEM((d, de), BF16), pltpu.VMEM((de, d), BF16),
                            pltpu.SemaphoreType.DMA((ROW_RING,)), pltpu.SemaphoreType.DMA((ROW_RING,))]),
        out_shape=jax.ShapeDtypeStruct((n_rows, w), jnp.int32),
        compiler_params=pltpu.CompilerParams(dimension_semantics=("arbitrary",),
                                             vmem_limit_bytes=VMEM_LIMIT),
        name="routed_experts",
    )(first_block, n_block, n_used, xg, w1, w3, w2)


def _combine_kernel(y_ref, x1_ref, sw1_ref, sw3_ref, sw2_ref, g2_ref, b2_ref, out_ref):
    x1 = x1_ref[...]
    xb = x1.astype(BF16)
    h1 = _dot(xb, sw1_ref[...])
    h3 = _dot(xb, sw3_ref[...])
    ffn = _dot((h1 * jax.nn.sigmoid(h1) * h3).astype(BF16), sw2_ref[...])
    ffn = ffn + _unpack_bf16_pairs(lax.bitcast_convert_type(y_ref[...], jnp.uint32))
    out_ref[...] = _layer_norm(DEEPNORM_ALPHA * x1 + ffn, g2_ref[...], b2_ref[...])


def _combine(y_routed, x1, sw1, sw3, sw2, g2, b2, tn=512):
    t, d = x1.shape
    w = y_routed.shape[1]
    row = lambda p: p.reshape(1, -1)
    operands = (y_routed, x1, sw1.astype(BF16), sw3.astype(BF16), sw2.astype(BF16), row(g2), row(b2))
    in_specs = [pl.BlockSpec((tn, w), lambda i: (i, 0)),
                pl.BlockSpec((tn, d), lambda i: (i, 0))]
    in_specs += [_const_spec(op.shape) for op in operands[2:]]
    return pl.pallas_call(
        _combine_kernel,
        grid=(t // tn,),
        in_specs=in_specs,
        out_specs=pl.BlockSpec((tn, d), lambda i: (i, 0)),
        out_shape=jax.ShapeDtypeStruct((t, d), F32),
        compiler_params=pltpu.CompilerParams(dimension_semantics=("arbitrary",),
                                             vmem_limit_bytes=VMEM_LIMIT),
        name="combine_shared_ln",
    )(*operands)


def _block_table(counts, bm):
    n_block = (counts + bm - 1) // bm
    ends = jnp.cumsum(n_block)
    first_block = ends - n_block
    i32 = lambda v: v.astype(jnp.int32)
    return i32(first_block * bm), i32(first_block), i32(n_block), i32(ends[-1:])


def _moe(x1, x1p, logits, router_bias, w1, w3, w2, sw1, sw3, sw2, g2, b2, bm=256):
    t, d = x1.shape
    n_exp = logits.shape[1]
    idx, gate_words, rank, counts = _route(logits, router_bias)
    n_blocks = (t * TOP_K) // bm + n_exp
    starts, first_block, n_block, n_used = _block_table(counts[:, 0], bm)
    xg, slots = _sc_dispatch(idx, rank, starts, x1p, n_blocks * bm)
    yg = _experts(first_block, n_block, n_used, xg, w1, w3, w2, bm)
    y_routed = _sc_combine(slots, gate_words, yg)
    return _combine(y_routed, x1, sw1, sw3, sw2, g2, b2)


def kernel(x, w_in, conv_w, conv_b, lru_wa, lru_ba, lru_wx, lru_bx, lru_lambda, pool_w, pool_scale, w_out, ln1_g, ln1_b, router_w, router_bias, exp_w1, exp_w3, exp_w2, sh_w1, sh_w3, sh_w2, ln2_g, ln2_b):
    nb, s, d = x.shape
    for l in range(DEPTH):
        x1, x1p, logits = _mixer(x, w_in[l], conv_w[l], conv_b[l], lru_wa[l], lru_ba[l], lru_wx[l], lru_bx[l],
                                 lru_lambda[l], pool_w[l], pool_scale[l], w_out[l], ln1_g[l], ln1_b[l],
                                 router_w[l])
        t = nb * s
        x = _moe(x1.reshape(t, d), x1p.reshape(t, d // 2), logits.reshape(t, -1), router_bias[l],
                 exp_w1[l], exp_w3[l], exp_w2[l], sh_w1[l], sh_w3[l], sh_w2[l], ln2_g[l], ln2_b[l])
        x = x.reshape(nb, s, d)
    return x
```

```python
import functools
import math

import jax
import jax.numpy as jnp
from jax import lax
from jax.experimental import pallas as pl
from jax.experimental.pallas import tpu as pltpu
from jax.experimental.pallas import tpu_sc as plsc

LRU_HEADS = 8
CONV_WIDTH = 4
LRU_C = 8.0
POOL_WINDOWS = (2, 4, 8, 16)
N_EXPERT_GROUPS = 8
TOPK_GROUPS = 4
TOP_K = 8
ROUTED_SCALE = 2.5
LN_EPS = 1e-5
DEPTH = 1
DEEPNORM_ALPHA = (2.0 * DEPTH) ** 0.25

MXU_DIM = 256
POOL_GROUP_DIM = 128
CONV_HIST = 8
POOL_HIST = 16
SC_CHUNK = 128
SC_SUB = 8
MIXER_PARTS = 4
BLOCK_GROUP = 4
ROW_RING = 8
VMEM_LIMIT = 56 * 1024 * 1024

F32 = jnp.float32
BF16 = jnp.bfloat16


def _dot(a, b):
    return jnp.dot(a, b, preferred_element_type=F32)


def _layer_norm(z, g, b):
    mu = jnp.mean(z, axis=-1, keepdims=True)
    zc = z - mu
    var = jnp.mean(zc * zc, axis=-1, keepdims=True)
    return zc * lax.rsqrt(var + LN_EPS) * g + b


def _block_diag_dot(xb, w_ref):
    n = w_ref.shape[0]
    return jnp.concatenate(
        [_dot(xb[:, i * MXU_DIM:(i + 1) * MXU_DIM], w_ref[i]) for i in range(n)], axis=1)


def _causal_conv(u, hist_ref, part, conv_w, conv_b):
    pb, ts, c = u.shape
    ext = jnp.concatenate([hist_ref[part], u], axis=1)
    hist_ref[part] = u[:, ts - CONV_HIST:, :]
    uc = jnp.broadcast_to(conv_b.reshape(1, 1, c), (pb, ts, c))
    for k in range(CONV_WIDTH):
        off = CONV_HIST - (CONV_WIDTH - 1) + k
        uc = uc + conv_w[k:k + 1, :].reshape(1, 1, c) * ext[:, off:off + ts, :]
    return uc


def _lru_scan(uc, ga, gx, lam, carry_ref, part, t0):
    pb, ts, c = uc.shape
    r = jax.nn.sigmoid(ga)
    i_gate = jax.nn.sigmoid(gx)
    softplus_neg_lam = jnp.maximum(-lam, 0.0) + jnp.log1p(jnp.exp(-jnp.abs(lam)))
    log_a = (-LRU_C) * r * softplus_neg_lam
    a = jnp.exp(log_a)
    mult = jnp.sqrt(jnp.tanh(-log_a) * (1.0 + a * a))
    t_local = lax.broadcasted_iota(jnp.int32, (pb, ts, c), 1)
    first = (t_local + t0) == 0
    a3 = a.reshape(pb, ts, c)
    x3 = jnp.where(first, 1.0, mult.reshape(pb, ts, c)) * i_gate.reshape(pb, ts, c) * uc

    shift = 1
    while shift < ts:
        valid = t_local >= shift
        a_prev = pltpu.roll(a3, shift, axis=1)
        x_prev = pltpu.roll(x3, shift, axis=1)
        x3 = jnp.where(valid, a3 * x_prev, 0.0) + x3
        a3 = jnp.where(valid, a3 * a_prev, a3)
        shift *= 2
    h = x3 + a3 * carry_ref[part][:, None, :]
    carry_ref[part] = h[:, ts - 1, :]
    return h


def _multiscale_pool(v, hist_ref, part, t0):
    pb, ts, c = v.shape
    ext = jnp.concatenate([hist_ref[part], v], axis=1)
    hist_ref[part] = v[:, ts - POOL_HIST:, :]
    t_glob = lax.broadcasted_iota(jnp.int32, (1, ts, POOL_GROUP_DIM), 1) + t0
    pooled = []
    for g, win in enumerate(POOL_WINDOWS):
        lanes = slice(g * POOL_GROUP_DIM, (g + 1) * POOL_GROUP_DIM)
        s = ext[:, :, lanes]
        shift = 1
        while shift < win:
            s = s + pltpu.roll(s, shift, axis=1)
            shift *= 2
        count = jnp.minimum(t_glob + 1, win).astype(F32)
        pooled.append(s[:, POOL_HIST:, :] / count - v[:, :, lanes])
    return jnp.concatenate(pooled, axis=-1)


def _mixer_kernel(x_ref, w_in_ref, conv_w_ref, conv_b_ref, wa_ref, ba_ref, wx_ref, bx_ref, lam_ref,
                  pool_w_ref, pool_scale_ref, w_out_ref, g1_ref, b1_ref, rw_hi_ref, rw_lo_ref,
                  x1_ref, x1p_ref, logit_ref, hist_u, hist_v, carry):
    j = pl.program_id(0)
    nb, ts, d = x_ref.shape
    c = hist_u.shape[-1]
    pb = nb // MIXER_PARTS
    rows = pb * ts
    t0 = j * ts
    parts = [slice(p * pb, (p + 1) * pb) for p in range(MIXER_PARTS)]

    @pl.when(j == 0)
    def _():
        hist_u[...] = jnp.zeros_like(hist_u)
        hist_v[...] = jnp.zeros_like(hist_v)
        carry[...] = jnp.zeros_like(carry)

    xs = [x_ref[p].reshape(rows, d) for p in parts]
    projs = [_dot(x.astype(BF16), w_in_ref[...]) for x in xs]
    ucs = [_causal_conv(proj[:, :c].reshape(pb, ts, c), hist_u, p, conv_w_ref[...], conv_b_ref[...])
           for proj, p in zip(projs, parts)]
    ucbs = [uc.reshape(rows, c).astype(BF16) for uc in ucs]
    gas = [_block_diag_dot(ucb, wa_ref) + ba_ref[...] for ucb in ucbs]
    gxs = [_block_diag_dot(ucb, wx_ref) + bx_ref[...] for ucb in ucbs]
    hs = [_lru_scan(uc, ga, gx, lam_ref[...], carry, p, t0) for uc, ga, gx, p in zip(ucs, gas, gxs, parts)]
    y_lrus = []
    for h, proj in zip(hs, projs):
        gate = proj[:, c:2 * c]
        gelu_gate = 0.5 * gate * (1.0 + jnp.tanh(math.sqrt(2.0 / math.pi) * (gate + 0.044715 * gate * gate * gate)))
        y_lrus.append(h.reshape(rows, c) * gelu_gate)
    pooled = [_multiscale_pool(proj[:, 2 * c:].reshape(pb, ts, c), hist_v, p, t0).reshape(rows, c).astype(BF16)
              for proj, p in zip(projs, parts)]
    y_pools = [_block_diag_dot(pl_, pool_w_ref) * pool_scale_ref[...] for pl_ in pooled]
    mixes = [_dot(jnp.concatenate([y_lru, y_pool], axis=1).astype(BF16), w_out_ref[...])
             for y_lru, y_pool in zip(y_lrus, y_pools)]
    x1s = [_layer_norm(DEEPNORM_ALPHA * x + mix, g1_ref[...], b1_ref[...]) for x, mix in zip(xs, mixes)]

    for p, x1 in zip(parts, x1s):
        x1_ref[p] = x1.reshape(pb, ts, d)
        hi = x1.astype(BF16)
        hi_f = hi.astype(F32)
        lo = (x1 - hi_f).astype(BF16)
        logits = _dot(hi, rw_hi_ref[...]) + _dot(lo, rw_hi_ref[...]) + _dot(hi, rw_lo_ref[...])
        logit_ref[p] = logits.reshape(pb, ts, logits.shape[-1])
        bits = lax.bitcast_convert_type(hi_f, jnp.uint32)
        packed = bits[:, :d // 2] | (bits[:, d // 2:] >> 16)
        x1p_ref[p] = packed.reshape(pb, ts, d // 2)


def _const_spec(shape):
    return pl.BlockSpec(shape, lambda j: (0,) * len(shape), pipeline_mode=pl.Buffered(1))


def _regroup_block_diag(w, width):
    heads, hd, _ = w.shape
    per = width // hd
    w = w.reshape(heads // per, per, hd, hd)
    eye = jnp.eye(per, dtype=w.dtype)
    return jnp.einsum("gpij,pq->gpiqj", w, eye).reshape(heads // per, width, width)


def _mixer(x, w_in, conv_w, conv_b, wa, ba, wx, bx, lam, pool_w, pool_scale, w_out, g1, b1, router_w, ts=64):
    nb, s, d = x.shape
    c = conv_w.shape[-1]
    n_exp = router_w.shape[-1]
    rw_hi = router_w.astype(BF16)
    rw_lo = (router_w - rw_hi.astype(F32)).astype(BF16)
    row = lambda p: p.reshape(1, -1)
    operands = (
        x, w_in.astype(BF16), conv_w, row(conv_b),
        _regroup_block_diag(wa, MXU_DIM).astype(BF16), row(ba),
        _regroup_block_diag(wx, MXU_DIM).astype(BF16), row(bx), row(lam),
        _regroup_block_diag(pool_w, MXU_DIM).astype(BF16), row(pool_scale),
        w_out.astype(BF16), row(g1), row(b1), rw_hi, rw_lo)
    tile = lambda width: pl.BlockSpec((nb, ts, width), lambda j: (0, j, 0))
    in_specs = [tile(d)] + [_const_spec(op.shape) for op in operands[1:]]
    return pl.pallas_call(
        _mixer_kernel,
        grid=(s // ts,),
        in_specs=in_specs,
        out_specs=[tile(d), tile(d // 2), tile(n_exp)],
        out_shape=[jax.ShapeDtypeStruct((nb, s, d), F32),
                   jax.ShapeDtypeStruct((nb, s, d // 2), jnp.uint32),
                   jax.ShapeDtypeStruct((nb, s, n_exp), F32)],
        scratch_shapes=[pltpu.VMEM((nb, CONV_HIST, c), F32),
                        pltpu.VMEM((nb, POOL_HIST, c), F32),
                        pltpu.VMEM((nb, c), F32)],
        compiler_params=pltpu.CompilerParams(dimension_semantics=("arbitrary",),
                                             vmem_limit_bytes=VMEM_LIMIT),
        name="mixer_ln_router",
    )(*operands)


def _first_row_of(cond, rows_rev, n):
    return n - jnp.max(jnp.where(cond, rows_rev, 0.0), axis=0, keepdims=True)


def _route_kernel(logit_ref, bias_ref, idx_ref, gate_ref, rank_ref, count_ref, count_scr):
    i = pl.program_id(0)

    @pl.when(i == 0)
    def _():
        count_scr[...] = jnp.zeros_like(count_scr)

    scores = jax.nn.sigmoid(logit_ref[...].T)
    n_exp, tn = scores.shape
    gsz = n_exp // N_EXPERT_GROUPS
    neg = -jnp.inf
    biased = scores + bias_ref[...]
    row = lax.broadcasted_iota(jnp.int32, (n_exp, tn), 0).astype(F32)
    row_rev = n_exp - row

    g_row = lax.broadcasted_iota(jnp.int32, (N_EXPERT_GROUPS, tn), 0)
    group_score = jnp.zeros((N_EXPERT_GROUPS, tn), F32)
    grp_row = lax.broadcasted_iota(jnp.int32, (gsz, tn), 0).astype(F32)
    grp_rev = gsz - grp_row
    for g in range(N_EXPERT_GROUPS):
        blk = biased[g * gsz:(g + 1) * gsz]
        m1 = jnp.max(blk, axis=0, keepdims=True)
        i1 = _first_row_of(blk == m1, grp_rev, gsz)
        m2 = jnp.max(jnp.where(grp_row == i1, neg, blk), axis=0, keepdims=True)
        group_score = jnp.where(g_row == g, m1 + m2, group_score)

    beaten_by = jnp.zeros((N_EXPERT_GROUPS, tn), F32)
    for g in range(N_EXPERT_GROUPS):
        other = jnp.max(jnp.where(g_row == g, group_score, neg), axis=0, keepdims=True)
        wins = (other > group_score) | ((other == group_score) & (g < g_row))
        beaten_by = beaten_by + jnp.where(wins, 1.0, 0.0)
    masked = []
    for g in range(N_EXPERT_GROUPS):
        beaten_g = jnp.max(jnp.where(g_row == g, beaten_by, 0.0), axis=0, keepdims=True)
        masked.append(jnp.where(beaten_g < TOPK_GROUPS, biased[g * gsz:(g + 1) * gsz], neg))
    masked = jnp.concatenate(masked, axis=0)

    k_row = lax.broadcasted_iota(jnp.int32, (TOP_K, tn), 0)
    candidates = masked
    idx = jnp.zeros((TOP_K, tn), F32)
    gate = jnp.zeros((TOP_K, tn), F32)
    hits = []
    for k in range(TOP_K):
        m = jnp.max(masked, axis=0, keepdims=True)
        ik = _first_row_of(masked == m, row_rev, n_exp)
        hit = row == ik
        gk = jnp.sum(jnp.where(hit, scores, 0.0), axis=0, keepdims=True)
        masked = jnp.where(hit, neg, masked)
        idx = jnp.where(k_row == k, ik, idx)
        gate = jnp.where(k_row == k, gk, gate)
        hits.append(hit)
    gate = gate / jnp.sum(gate, axis=0, keepdims=True) * ROUTED_SCALE
    selected = jnp.where((masked == neg) & (candidates != neg), 1.0, 0.0)

    earlier = (lax.broadcasted_iota(jnp.int32, (tn, tn), 0) < lax.broadcasted_iota(jnp.int32, (tn, tn), 1))
    before = _dot(selected.astype(BF16), jnp.where(earlier, 1.0, 0.0).astype(BF16)) + count_scr[...]
    rank = jnp.zeros((TOP_K, tn), F32)
    for k in range(TOP_K):
        rk = jnp.sum(jnp.where(hits[k], before, 0.0), axis=0, keepdims=True)
        rank = jnp.where(k_row == k, rk, rank)
    count_scr[...] = count_scr[...] + jnp.sum(selected, axis=1, keepdims=True)

    bits = lax.bitcast_convert_type(gate.astype(BF16).astype(F32), jnp.uint32)
    words = lax.bitcast_convert_type(bits | (bits >> 16), jnp.int32)
    chunk = idx_ref.shape[-1]
    for c in range(idx_ref.shape[0]):
        lanes = slice(c * chunk, (c + 1) * chunk)
        idx_ref[c] = idx[:, lanes].astype(jnp.int32)
        rank_ref[c] = rank[:, lanes].astype(jnp.int32)
        gate_ref[c] = words[:, lanes]
    count_ref[...] = count_scr[...].astype(jnp.int32)


def _route(logits, bias, tn=512):
    t, n_exp = logits.shape
    chunked = pl.BlockSpec((tn // SC_CHUNK, TOP_K, SC_CHUNK), lambda i: (i, 0, 0))
    chunked_shape = jax.ShapeDtypeStruct((t // SC_CHUNK, TOP_K, SC_CHUNK), jnp.int32)
    return pl.pallas_call(
        _route_kernel,
        grid=(t // tn,),
        in_specs=[pl.BlockSpec((tn, n_exp), lambda i: (i, 0)),
                  pl.BlockSpec((n_exp, 1), lambda i: (0, 0))],
        out_specs=[chunked, chunked, chunked, pl.BlockSpec((n_exp, 1), lambda i: (0, 0))],
        out_shape=[chunked_shape, chunked_shape, chunked_shape, jax.ShapeDtypeStruct((n_exp, 1), jnp.int32)],
        scratch_shapes=[pltpu.VMEM((n_exp, 1), F32)],
        compiler_params=pltpu.CompilerParams(dimension_semantics=("arbitrary",),
                                             vmem_limit_bytes=VMEM_LIMIT),
        name="route_topk_rank",
    )(logits, bias.reshape(n_exp, 1))


def _unpack_bf16_pairs(p):
    hi = lax.bitcast_convert_type(p & jnp.uint32(0xFFFF0000), F32)
    lo = lax.bitcast_convert_type(p << 16, F32)
    return jnp.concatenate([hi, lo], axis=1)


def _pack_bf16_pairs(y):
    n = y.shape[1] // 2
    bits = lax.bitcast_convert_type(y.astype(BF16).astype(F32), jnp.uint32)
    return bits[:, :n] | (bits[:, n:] >> 16)


def _sc_worker_layout(n_chunks_total):
    info = plsc.get_sparse_core_info()
    n_workers = info.num_cores * info.num_subcores
    return info.num_cores, n_chunks_total // n_workers


def _sc_dispatch(idx, rank, expert_start, x1p, n_rows):
    n_chunks_total, n_k, chunk = idx.shape
    t, w = x1p.shape
    n_cores, n_chunks = _sc_worker_layout(n_chunks_total)
    lanes = plsc.get_sparse_core_info().num_lanes
    mesh = plsc.VectorSubcoreMesh(core_axis_name="c", subcore_axis_name="s")
    table = pltpu.VMEM((n_k, chunk), jnp.int32)

    @functools.partial(
        pl.kernel, mesh=mesh, name="sc_dispatch_rows",
        compiler_params=pltpu.CompilerParams(needs_layout_passes=False),
        out_type=[jax.ShapeDtypeStruct((n_rows, w), x1p.dtype), jax.ShapeDtypeStruct(idx.shape, jnp.int32)],
        scratch_types=[table, table, pltpu.VMEM(expert_start.shape, jnp.int32), pltpu.VMEM((chunk, w), x1p.dtype),
                       pltpu.SemaphoreType.DMA, pltpu.SemaphoreType.DMA])
    def dispatch(idx_hbm, rank_hbm, start_hbm, x_hbm, xg_hbm, slots_hbm, slot_v, rank_v, start_v, rows_v, sem,
                 row_sem):
        wid = lax.axis_index("s") * n_cores + lax.axis_index("c")
        pltpu.sync_copy(start_hbm, start_v)

        @pl.loop(0, n_chunks)
        def _(ci):
            chunk_id = wid * n_chunks + ci
            rows_in = pltpu.make_async_copy(x_hbm.at[pl.ds(chunk_id * chunk, chunk)], rows_v, row_sem)
            rows_in.start()
            pltpu.sync_copy(idx_hbm.at[chunk_id], slot_v)
            pltpu.sync_copy(rank_hbm.at[chunk_id], rank_v)
            for k in range(n_k):
                for j in range(chunk // lanes):
                    part = pl.ds(j * lanes, lanes)
                    slot_v[k, part] = plsc.load_gather(start_v, [slot_v[k, part]]) + rank_v[k, part]
            rows_in.wait()
            copies = [pltpu.make_async_copy(rows_v, xg_hbm.at[slot_v.at[k]], sem) for k in range(n_k)]
            for cp in copies:
                cp.start()
            pltpu.sync_copy(slot_v, slots_hbm.at[chunk_id])
            for cp in copies:
                cp.wait()

    return dispatch(idx, rank, expert_start, x1p)


def _sc_combine(slots, gate_words, yg):
    n_chunks_total, n_k, chunk = slots.shape
    w = yg.shape[1]
    t = n_chunks_total * chunk
    n_cores, n_chunks = _sc_worker_layout(n_chunks_total)
    lanes = plsc.get_sparse_core_info().num_lanes
    n_sub = chunk // SC_SUB
    mesh = plsc.VectorSubcoreMesh(core_axis_name="c", subcore_axis_name="s")
    rows_t = pltpu.VMEM((n_k, SC_SUB, w), jnp.int32)
    out_t = pltpu.VMEM((SC_SUB, w), jnp.int32)
    dma = pltpu.SemaphoreType.DMA

    @functools.partial(
        pl.kernel, mesh=mesh, name="sc_combine_rows",
        compiler_params=pltpu.CompilerParams(needs_layout_passes=False),
        out_type=jax.ShapeDtypeStruct((t, w), jnp.int32),
        scratch_types=[pltpu.VMEM((n_k, chunk), jnp.int32), pltpu.VMEM((n_k, chunk), jnp.int32),
                       rows_t, rows_t, out_t, out_t, dma, dma, dma, dma])
    def combine(slots_hbm, gate_hbm, yg_hbm, out_hbm, idx_v, gate_v, rows0, rows1, out0, out1,
                row_sem0, row_sem1, out_sem0, out_sem1):
        wid = lax.axis_index("s") * n_cores + lax.axis_index("c")
        rows, outs = (rows0, rows1), (out0, out1)
        row_sems, out_sems = (row_sem0, row_sem1), (out_sem0, out_sem1)

        def gathers(sub, b):
            return [pltpu.make_async_copy(yg_hbm.at[idx_v.at[k, pl.ds(sub * SC_SUB, SC_SUB)]], rows[b].at[k],
                                          row_sems[b]) for k in range(n_k)]

        @pl.loop(0, n_chunks)
        def _(ci):
            chunk_id = wid * n_chunks + ci
            tok0 = chunk_id * chunk
            pltpu.sync_copy(slots_hbm.at[chunk_id], idx_v)
            pltpu.sync_copy(gate_hbm.at[chunk_id], gate_v)

            def store(sub, b):
                return pltpu.make_async_copy(outs[b], out_hbm.at[pl.ds(tok0 + sub * SC_SUB, SC_SUB)], out_sems[b])

            for cp in gathers(0, 0):
                cp.start()

            @pl.loop(0, n_sub // 2)
            def _(pair):
                for b in (0, 1):
                    sub = 2 * pair + b

                    @pl.when(sub + 1 < n_sub)
                    def _():
                        for cp in gathers(sub + 1, 1 - b):
                            cp.start()

                    for cp in gathers(sub, b):
                        cp.wait()

                    @pl.when(sub >= 2)
                    def _():
                        store(sub - 2, b).wait()

                    buf, out = rows[b], outs[b]

                    @pl.loop(0, SC_SUB)
                    def _(tt):
                        col = jnp.full((lanes,), sub * SC_SUB + tt, jnp.int32)
                        gates = [plsc.bitcast(plsc.load_gather(gate_v, [jnp.full((lanes,), k, jnp.int32), col]), BF16)
                                 for k in range(n_k)]
                        for v in range(w // lanes):
                            terms = [gates[k] * plsc.bitcast(buf[k, tt, pl.ds(v * lanes, lanes)], BF16)
                                     for k in range(n_k)]
                            while len(terms) > 1:
                                terms = [terms[i] + terms[i + 1] for i in range(0, len(terms), 2)]
                            out[tt, pl.ds(v * lanes, lanes)] = plsc.bitcast(terms[0], jnp.int32)

                    store(sub, b).start()

            store(n_sub - 2, 0).wait()
            store(n_sub - 1, 1).wait()

    return combine(slots, gate_words, yg)


def _expert_kernel(first_block, n_block, n_used, xg_hbm, w1_ref, w3_ref, w2_ref, yg_hbm,
                   x_buf, y_buf, w1_bf, w3_bf, w2_bf, in_sem, out_sem):
    e = pl.program_id(0)
    n_slot, bm, _ = x_buf.shape
    ahead = n_slot - BLOCK_GROUP
    total = n_used[0]

    def fetch(g):
        slot = g % n_slot
        return pltpu.make_async_copy(xg_hbm.at[pl.ds(g * bm, bm)], x_buf.at[slot], in_sem.at[slot])

    def write_back(g):
        slot = g % n_slot
        return pltpu.make_async_copy(y_buf.at[slot], yg_hbm.at[pl.ds(g * bm, bm)], out_sem.at[slot])

    @pl.when(e == 0)
    def _():
        for g in range(ahead):
            @pl.when(g < total)
            def _():
                fetch(g).start()

    def cast_weights():
        ws = [w1_ref[0].astype(BF16), w3_ref[0].astype(BF16), w2_ref[0].astype(BF16)]
        for ref, wb in zip((w1_bf, w3_bf, w2_bf), ws):
            ref[...] = wb
        return ws

    def swiglu(slots, weights):
        w1b, w3b, w2b = weights
        xs = [_unpack_bf16_pairs(x_buf[s]).astype(BF16) for s in slots]
        up = [(_dot(xb, w1b), _dot(xb, w3b)) for xb in xs]
        hs = [(h1 * jax.nn.sigmoid(h1) * h3).astype(BF16) for h1, h3 in up]
        ys = [_dot(h, w2b) for h in hs]
        return [lax.bitcast_convert_type(_pack_bf16_pairs(y), y_buf.dtype) for y in ys]

    def process(blocks, cast_here=False):
        for g in blocks:
            fetch(g).wait()
        for g in blocks:
            @pl.when(g + ahead < total)
            def _():
                fetch(g + ahead).start()
        weights = cast_weights() if cast_here else (w1_bf[...], w3_bf[...], w2_bf[...])
        ys = swiglu([g % n_slot for g in blocks], weights)
        for g in blocks:
            @pl.when(g >= n_slot)
            def _():
                write_back(g - n_slot).wait()
        for g, y in zip(blocks, ys):
            y_buf[g % n_slot] = y
            write_back(g).start()

    n_full = n_block[e] // BLOCK_GROUP

    @pl.when(n_full > 0)
    def _():
        process([first_block[e] + j for j in range(BLOCK_GROUP)], cast_here=True)

    @pl.when((n_full == 0) & (n_block[e] > 0))
    def _():
        cast_weights()

    @pl.loop(1, n_full)
    def _(i):
        process([first_block[e] + i * BLOCK_GROUP + j for j in range(BLOCK_GROUP)])

    done = n_full * BLOCK_GROUP
    size = BLOCK_GROUP // 2
    while size >= 1:
        left = n_block[e] - done
        start = first_block[e] + done

        @pl.when(left >= size)
        def _(start=start, size=size):
            process([start + j for j in range(size)])

        done = done + left // size * size
        size //= 2

    @pl.when(e == pl.num_programs(0) - 1)
    def _():
        for back in range(1, n_slot + 1):
            @pl.when(total >= back)
            def _():
                write_back(total - back).wait()


def _experts(first_block, n_block, n_used, xg, w1, w3, w2, bm):
    n_rows, w = xg.shape
    n_exp, d, de = w1.shape
    weight = lambda e, *_: (e, 0, 0)
    return pl.pallas_call(
        _expert_kernel,
        grid_spec=pltpu.PrefetchScalarGridSpec(
            num_scalar_prefetch=3,
            grid=(n_exp,),
            in_specs=[pl.BlockSpec(memory_space=pl.ANY),
                      pl.BlockSpec((1, d, de), weight),
                      pl.BlockSpec((1, d, de), weight),
                      pl.BlockSpec((1, de, d), weight)],
            out_specs=pl.BlockSpec(memory_space=pl.ANY),
            scratch_shapes=[pltpu.VMEM((ROW_RING, bm, w), xg.dtype), pltpu.VMEM((ROW_RING, bm, w), jnp.int32),
                            pltpu.VMEM((d, de), BF16), pltpu.VMEM((d, de), BF16), pltpu.VMEM((de, d), BF16),
                            pltpu.SemaphoreType.DMA((ROW_RING,)), pltpu.SemaphoreType.DMA((ROW_RING,))]),
        out_shape=jax.ShapeDtypeStruct((n_rows, w), jnp.int32),
        compiler_params=pltpu.CompilerParams(dimension_semantics=("arbitrary",),
                                             vmem_limit_bytes=VMEM_LIMIT),
        name="routed_experts",
    )(first_block, n_block, n_used, xg, w1, w3, w2)


def _combine_kernel(y_ref, x1_ref, sw1_ref, sw3_ref, sw2_ref, g2_ref, b2_ref, out_ref):
    x1 = x1_ref[...]
    xb = x1.astype(BF16)
    h1 = _dot(xb, sw1_ref[...])
    h3 = _dot(xb, sw3_ref[...])
    ffn = _dot((h1 * jax.nn.sigmoid(h1) * h3).astype(BF16), sw2_ref[...])
    ffn = ffn + _unpack_bf16_pairs(lax.bitcast_convert_type(y_ref[...], jnp.uint32))
    out_ref[...] = _layer_norm(DEEPNORM_ALPHA * x1 + ffn, g2_ref[...], b2_ref[...])


def _combine(y_routed, x1, sw1, sw3, sw2, g2, b2, tn=1024):
    t, d = x1.shape
    w = y_routed.shape[1]
    row = lambda p: p.reshape(1, -1)
    operands = (y_routed, x1, sw1.astype(BF16), sw3.astype(BF16), sw2.astype(BF16), row(g2), row(b2))
    in_specs = [pl.BlockSpec((tn, w), lambda i: (i, 0)),
                pl.BlockSpec((tn, d), lambda i: (i, 0))]
    in_specs += [_const_spec(op.shape) for op in operands[2:]]
    return pl.pallas_call(
        _combine_kernel,
        grid=(t // tn,),
        in_specs=in_specs,
        out_specs=pl.BlockSpec((tn, d), lambda i: (i, 0)),
        out_shape=jax.ShapeDtypeStruct((t, d), F32),
        compiler_params=pltpu.CompilerParams(dimension_semantics=("arbitrary",),
                                             vmem_limit_bytes=VMEM_LIMIT),
        name="combine_shared_ln",
    )(*operands)


def _block_table(counts, bm):
    n_block = (counts + bm - 1) // bm
    ends = jnp.cumsum(n_block)
    first_block = ends - n_block
    i32 = lambda v: v.astype(jnp.int32)
    return i32(first_block * bm), i32(first_block), i32(n_block), i32(ends[-1:])


def _moe(x1, x1p, logits, router_bias, w1, w3, w2, sw1, sw3, sw2, g2, b2, bm=256):
    t, d = x1.shape
    n_exp = logits.shape[1]
    idx, gate_words, rank, counts = _route(logits, router_bias)
    n_blocks = (t * TOP_K) // bm + n_exp
    starts, first_block, n_block, n_used = _block_table(counts[:, 0], bm)
    xg, slots = _sc_dispatch(idx, rank, starts, x1p, n_blocks * bm)
    yg = _experts(first_block, n_block, n_used, xg, w1, w3, w2, bm)
    y_routed = _sc_combine(slots, gate_words, yg)
    return _combine(y_routed, x1, sw1, sw3, sw2, g2, b2)


def kernel(x, w_in, conv_w, conv_b, lru_wa, lru_ba, lru_wx, lru_bx, lru_lambda, pool_w, pool_scale, w_out, ln1_g, ln1_b, router_w, router_bias, exp_w1, exp_w3, exp_w2, sh_w1, sh_w3, sh_w2, ln2_g, ln2_b):
    nb, s, d = x.shape
    for l in range(DEPTH):
        x1, x1p, logits = _mixer(x, w_in[l], conv_w[l], conv_b[l], lru_wa[l], lru_ba[l], lru_wx[l], lru_bx[l],
                                 lru_lambda[l], pool_w[l], pool_scale[l], w_out[l], ln1_g[l], ln1_b[l],
                                 router_w[l])
        t = nb * s
        x = _moe(x1.reshape(t, d), x1p.reshape(t, d // 2), logits.reshape(t, -1), router_bias[l],
                 exp_w1[l], exp_w3[l], exp_w2[l], sh_w1[l], sh_w3[l], sh_w2[l], ln2_g[l], ln2_b[l])
        x = x.reshape(nb, s, d)
    return x
```

```python
import functools
import math

import jax
import jax.numpy as jnp
from jax import lax
from jax.experimental import pallas as pl
from jax.experimental.pallas import tpu as pltpu
from jax.experimental.pallas import tpu_sc as plsc

LRU_HEADS = 8
CONV_WIDTH = 4
LRU_C = 8.0
POOL_WINDOWS = (2, 4, 8, 16)
N_EXPERT_GROUPS = 8
TOPK_GROUPS = 4
TOP_K = 8
ROUTED_SCALE = 2.5
LN_EPS = 1e-5
DEPTH = 1
DEEPNORM_ALPHA = (2.0 * DEPTH) ** 0.25

MXU_DIM = 256
POOL_GROUP_DIM = 128
CONV_HIST = 8
POOL_HIST = 16
SC_CHUNK = 128
SC_SUB = 8
MIXER_PARTS = 4
BLOCK_GROUP = 4
ROW_RING = 8
VMEM_LIMIT = 56 * 1024 * 1024

F32 = jnp.float32
BF16 = jnp.bfloat16


def _dot(a, b):
    return jnp.dot(a, b, preferred_element_type=F32)


def _layer_norm(z, g, b):
    mu = jnp.mean(z, axis=-1, keepdims=True)
    zc = z - mu
    var = jnp.mean(zc * zc, axis=-1, keepdims=True)
    return zc * lax.rsqrt(var + LN_EPS) * g + b


def _block_diag_dot(xb, w_ref):
    n = w_ref.shape[0]
    return jnp.concatenate(
        [_dot(xb[:, i * MXU_DIM:(i + 1) * MXU_DIM], w_ref[i]) for i in range(n)], axis=1)


def _causal_conv(u, hist_ref, part, conv_w, conv_b):
    pb, ts, c = u.shape
    ext = jnp.concatenate([hist_ref[part], u], axis=1)
    hist_ref[part] = u[:, ts - CONV_HIST:, :]
    uc = jnp.broadcast_to(conv_b.reshape(1, 1, c), (pb, ts, c))
    for k in range(CONV_WIDTH):
        off = CONV_HIST - (CONV_WIDTH - 1) + k
        uc = uc + conv_w[k:k + 1, :].reshape(1, 1, c) * ext[:, off:off + ts, :]
    return uc


def _lru_scan(uc, ga, gx, lam, carry_ref, part, t0):
    pb, ts, c = uc.shape
    r = jax.nn.sigmoid(ga)
    i_gate = jax.nn.sigmoid(gx)
    softplus_neg_lam = jnp.maximum(-lam, 0.0) + jnp.log1p(jnp.exp(-jnp.abs(lam)))
    log_a = (-LRU_C) * r * softplus_neg_lam
    a = jnp.exp(log_a)
    mult = jnp.sqrt(jnp.tanh(-log_a) * (1.0 + a * a))
    t_local = lax.broadcasted_iota(jnp.int32, (pb, ts, c), 1)
    first = (t_local + t0) == 0
    a3 = a.reshape(pb, ts, c)
    x3 = jnp.where(first, 1.0, mult.reshape(pb, ts, c)) * i_gate.reshape(pb, ts, c) * uc

    shift = 1
    while shift < ts:
        valid = t_local >= shift
        a_prev = pltpu.roll(a3, shift, axis=1)
        x_prev = pltpu.roll(x3, shift, axis=1)
        x3 = jnp.where(valid, a3 * x_prev, 0.0) + x3
        a3 = jnp.where(valid, a3 * a_prev, a3)
        shift *= 2
    h = x3 + a3 * carry_ref[part][:, None, :]
    carry_ref[part] = h[:, ts - 1, :]
    return h


def _multiscale_pool(v, hist_ref, part, t0):
    pb, ts, c = v.shape
    ext = jnp.concatenate([hist_ref[part], v], axis=1)
    hist_ref[part] = v[:, ts - POOL_HIST:, :]
    t_glob = lax.broadcasted_iota(jnp.int32, (1, ts, POOL_GROUP_DIM), 1) + t0
    pooled = []
    for g, win in enumerate(POOL_WINDOWS):
        lanes = slice(g * POOL_GROUP_DIM, (g + 1) * POOL_GROUP_DIM)
        s = ext[:, :, lanes]
        shift = 1
        while shift < win:
            s = s + pltpu.roll(s, shift, axis=1)
            shift *= 2
        count = jnp.minimum(t_glob + 1, win).astype(F32)
        pooled.append(s[:, POOL_HIST:, :] / count - v[:, :, lanes])
    return jnp.concatenate(pooled, axis=-1)


def _mixer_kernel(x_ref, w_in_ref, conv_w_ref, conv_b_ref, wa_ref, ba_ref, wx_ref, bx_ref, lam_ref,
                  pool_w_ref, pool_scale_ref, w_out_ref, g1_ref, b1_ref, rw_hi_ref, rw_lo_ref,
                  x1_ref, x1p_ref, logit_ref, hist_u, hist_v, carry):
    j = pl.program_id(0)
    nb, ts, d = x_ref.shape
    c = hist_u.shape[-1]
    pb = nb // MIXER_PARTS
    rows = pb * ts
    t0 = j * ts
    parts = [slice(p * pb, (p + 1) * pb) for p in range(MIXER_PARTS)]

    @pl.when(j == 0)
    def _():
        hist_u[...] = jnp.zeros_like(hist_u)
        hist_v[...] = jnp.zeros_like(hist_v)
        carry[...] = jnp.zeros_like(carry)

    xs = [x_ref[p].reshape(rows, d) for p in parts]
    projs = [_dot(x.astype(BF16), w_in_ref[...]) for x in xs]
    ucs = [_causal_conv(proj[:, :c].reshape(pb, ts, c), hist_u, p, conv_w_ref[...], conv_b_ref[...])
           for proj, p in zip(projs, parts)]
    ucbs = [uc.reshape(rows, c).astype(BF16) for uc in ucs]
    gas = [_block_diag_dot(ucb, wa_ref) + ba_ref[...] for ucb in ucbs]
    gxs = [_block_diag_dot(ucb, wx_ref) + bx_ref[...] for ucb in ucbs]
    hs = [_lru_scan(uc, ga, gx, lam_ref[...], carry, p, t0) for uc, ga, gx, p in zip(ucs, gas, gxs, parts)]
    y_lrus = []
    for h, proj in zip(hs, projs):
        gate = proj[:, c:2 * c]
        gelu_gate = 0.5 * gate * (1.0 + jnp.tanh(math.sqrt(2.0 / math.pi) * (gate + 0.044715 * gate * gate * gate)))
        y_lrus.append(h.reshape(rows, c) * gelu_gate)
    pooled = [_multiscale_pool(proj[:, 2 * c:].reshape(pb, ts, c), hist_v, p, t0).reshape(rows, c).astype(BF16)
              for proj, p in zip(projs, parts)]
    y_pools = [_block_diag_dot(pl_, pool_w_ref) * pool_scale_ref[...] for pl_ in pooled]
    mixes = [_dot(jnp.concatenate([y_lru, y_pool], axis=1).astype(BF16), w_out_ref[...])
             for y_lru, y_pool in zip(y_lrus, y_pools)]
    x1s = [_layer_norm(DEEPNORM_ALPHA * x + mix, g1_ref[...], b1_ref[...]) for x, mix in zip(xs, mixes)]

    for p, x1 in zip(parts, x1s):
        x1_ref[p] = x1.reshape(pb, ts, d)
        hi = x1.astype(BF16)
        hi_f = hi.astype(F32)
        lo = (x1 - hi_f).astype(BF16)
        logits = _dot(hi, rw_hi_ref[...]) + _dot(lo, rw_hi_ref[...]) + _dot(hi, rw_lo_ref[...])
        logit_ref[p] = logits.reshape(pb, ts, logits.shape[-1])
        bits = lax.bitcast_convert_type(hi_f, jnp.uint32)
        packed = bits[:, :d // 2] | (bits[:, d // 2:] >> 16)
        x1p_ref[p] = packed.reshape(pb, ts, d // 2)


def _const_spec(shape):
    return pl.BlockSpec(shape, lambda j: (0,) * len(shape), pipeline_mode=pl.Buffered(1))


def _regroup_block_diag(w, width):
    heads, hd, _ = w.shape
    per = width // hd
    w = w.reshape(heads // per, per, hd, hd)
    eye = jnp.eye(per, dtype=w.dtype)
    return jnp.einsum("gpij,pq->gpiqj", w, eye).reshape(heads // per, width, width)


def _mixer(x, w_in, conv_w, conv_b, wa, ba, wx, bx, lam, pool_w, pool_scale, w_out, g1, b1, router_w, ts=64):
    nb, s, d = x.shape
    c = conv_w.shape[-1]
    n_exp = router_w.shape[-1]
    rw_hi = router_w.astype(BF16)
    rw_lo = (router_w - rw_hi.astype(F32)).astype(BF16)
    row = lambda p: p.reshape(1, -1)
    operands = (
        x, w_in.astype(BF16), conv_w, row(conv_b),
        _regroup_block_diag(wa, MXU_DIM).astype(BF16), row(ba),
        _regroup_block_diag(wx, MXU_DIM).astype(BF16), row(bx), row(lam),
        _regroup_block_diag(pool_w, MXU_DIM).astype(BF16), row(pool_scale),
        w_out.astype(BF16), row(g1), row(b1), rw_hi, rw_lo)
    tile = lambda width: pl.BlockSpec((nb, ts, width), lambda j: (0, j, 0))
    in_specs = [tile(d)] + [_const_spec(op.shape) for op in operands[1:]]
    return pl.pallas_call(
        _mixer_kernel,
        grid=(s // ts,),
        in_specs=in_specs,
        out_specs=[tile(d), tile(d // 2), tile(n_exp)],
        out_shape=[jax.ShapeDtypeStruct((nb, s, d), F32),
                   jax.ShapeDtypeStruct((nb, s, d // 2), jnp.uint32),
                   jax.ShapeDtypeStruct((nb, s, n_exp), F32)],
        scratch_shapes=[pltpu.VMEM((nb, CONV_HIST, c), F32),
                        pltpu.VMEM((nb, POOL_HIST, c), F32),
                        pltpu.VMEM((nb, c), F32)],
        compiler_params=pltpu.CompilerParams(dimension_semantics=("arbitrary",),
                                             vmem_limit_bytes=VMEM_LIMIT),
        name="mixer_ln_router",
    )(*operands)


def _first_row_of(cond, rows_rev, n):
    return n - jnp.max(jnp.where(cond, rows_rev, 0.0), axis=0, keepdims=True)


def _route_kernel(logit_ref, bias_ref, idx_ref, gate_ref, rank_ref, count_ref, count_scr):
    i = pl.program_id(0)

    @pl.when(i == 0)
    def _():
        count_scr[...] = jnp.zeros_like(count_scr)

    scores = jax.nn.sigmoid(logit_ref[...].T)
    n_exp, tn = scores.shape
    gsz = n_exp // N_EXPERT_GROUPS
    neg = -jnp.inf
    biased = scores + bias_ref[...]
    row = lax.broadcasted_iota(jnp.int32, (n_exp, tn), 0).astype(F32)
    row_rev = n_exp - row

    g_row = lax.broadcasted_iota(jnp.int32, (N_EXPERT_GROUPS, tn), 0)
    group_score = jnp.zeros((N_EXPERT_GROUPS, tn), F32)
    grp_row = lax.broadcasted_iota(jnp.int32, (gsz, tn), 0).astype(F32)
    grp_rev = gsz - grp_row
    for g in range(N_EXPERT_GROUPS):
        blk = biased[g * gsz:(g + 1) * gsz]
        m1 = jnp.max(blk, axis=0, keepdims=True)
        i1 = _first_row_of(blk == m1, grp_rev, gsz)
        m2 = jnp.max(jnp.where(grp_row == i1, neg, blk), axis=0, keepdims=True)
        group_score = jnp.where(g_row == g, m1 + m2, group_score)

    beaten_by = jnp.zeros((N_EXPERT_GROUPS, tn), F32)
    for g in range(N_EXPERT_GROUPS):
        other = jnp.max(jnp.where(g_row == g, group_score, neg), axis=0, keepdims=True)
        wins = (other > group_score) | ((other == group_score) & (g < g_row))
        beaten_by = beaten_by + jnp.where(wins, 1.0, 0.0)
    masked = []
    for g in range(N_EXPERT_GROUPS):
        beaten_g = jnp.max(jnp.where(g_row == g, beaten_by, 0.0), axis=0, keepdims=True)
        masked.append(jnp.where(beaten_g < TOPK_GROUPS, biased[g * gsz:(g + 1) * gsz], neg))
    masked = jnp.concatenate(masked, axis=0)

    k_row = lax.broadcasted_iota(jnp.int32, (TOP_K, tn), 0)
    candidates = masked
    idx = jnp.zeros((TOP_K, tn), F32)
    gate = jnp.zeros((TOP_K, tn), F32)
    hits = []
    for k in range(TOP_K):
        m = jnp.max(masked, axis=0, keepdims=True)
        ik = _first_row_of(masked == m, row_rev, n_exp)
        hit = row == ik
        gk = jnp.sum(jnp.where(hit, scores, 0.0), axis=0, keepdims=True)
        masked = jnp.where(hit, neg, masked)
        idx = jnp.where(k_row == k, ik, idx)
        gate = jnp.where(k_row == k, gk, gate)
        hits.append(hit)
    gate = gate / jnp.sum(gate, axis=0, keepdims=True) * ROUTED_SCALE
    selected = jnp.where((masked == neg) & (candidates != neg), 1.0, 0.0)

    earlier = (lax.broadcasted_iota(jnp.int32, (tn, tn), 0) < lax.broadcasted_iota(jnp.int32, (tn, tn), 1))
    before = _dot(selected.astype(BF16), jnp.where(earlier, 1.0, 0.0).astype(BF16)) + count_scr[...]
    rank = jnp.zeros((TOP_K, tn), F32)
    for k in range(TOP_K):
        rk = jnp.sum(jnp.where(hits[k], before, 0.0), axis=0, keepdims=True)
        rank = jnp.where(k_row == k, rk, rank)
    count_scr[...] = count_scr[...] + jnp.sum(selected, axis=1, keepdims=True)

    bits = lax.bitcast_convert_type(gate.astype(BF16).astype(F32), jnp.uint32)
    words = lax.bitcast_convert_type(bits | (bits >> 16), jnp.int32)
    chunk = idx_ref.shape[-1]
    for c in range(idx_ref.shape[0]):
        lanes = slice(c * chunk, (c + 1) * chunk)
        idx_ref[c] = idx[:, lanes].astype(jnp.int32)
        rank_ref[c] = rank[:, lanes].astype(jnp.int32)
        gate_ref[c] = words[:, lanes]
    count_ref[...] = count_scr[...].astype(jnp.int32)


def _route(logits, bias, tn=512):
    t, n_exp = logits.shape
    chunked = pl.BlockSpec((tn // SC_CHUNK, TOP_K, SC_CHUNK), lambda i: (i, 0, 0))
    chunked_shape = jax.ShapeDtypeStruct((t // SC_CHUNK, TOP_K, SC_CHUNK), jnp.int32)
    return pl.pallas_call(
        _route_kernel,
        grid=(t // tn,),
        in_specs=[pl.BlockSpec((tn, n_exp), lambda i: (i, 0)),
                  pl.BlockSpec((n_exp, 1), lambda i: (0, 0))],
        out_specs=[chunked, chunked, chunked, pl.BlockSpec((n_exp, 1), lambda i: (0, 0))],
        out_shape=[chunked_shape, chunked_shape, chunked_shape, jax.ShapeDtypeStruct((n_exp, 1), jnp.int32)],
        scratch_shapes=[pltpu.VMEM((n_exp, 1), F32)],
        compiler_params=pltpu.CompilerParams(dimension_semantics=("arbitrary",),
                                             vmem_limit_bytes=VMEM_LIMIT),
        name="route_topk_rank",
    )(logits, bias.reshape(n_exp, 1))


def _unpack_bf16_pairs(p):
    hi = lax.bitcast_convert_type(p & jnp.uint32(0xFFFF0000), F32)
    lo = lax.bitcast_convert_type(p << 16, F32)
    return jnp.concatenate([hi, lo], axis=1)


def _pack_bf16_pairs(y):
    n = y.shape[1] // 2
    bits = lax.bitcast_convert_type(y.astype(BF16).astype(F32), jnp.uint32)
    return bits[:, :n] | (bits[:, n:] >> 16)


def _sc_worker_layout(n_chunks_total):
    info = plsc.get_sparse_core_info()
    n_workers = info.num_cores * info.num_subcores
    return info.num_cores, n_chunks_total // n_workers


def _sc_dispatch(idx, rank, expert_start, x1p, n_rows):
    n_chunks_total, n_k, chunk = idx.shape
    t, w = x1p.shape
    n_cores, n_chunks = _sc_worker_layout(n_chunks_total)
    lanes = plsc.get_sparse_core_info().num_lanes
    mesh = plsc.VectorSubcoreMesh(core_axis_name="c", subcore_axis_name="s")
    table = pltpu.VMEM((n_k, chunk), jnp.int32)

    @functools.partial(
        pl.kernel, mesh=mesh, name="sc_dispatch_rows",
        compiler_params=pltpu.CompilerParams(needs_layout_passes=False),
        out_type=[jax.ShapeDtypeStruct((n_rows, w), x1p.dtype), jax.ShapeDtypeStruct(idx.shape, jnp.int32)],
        scratch_types=[table, table, pltpu.VMEM(expert_start.shape, jnp.int32), pltpu.VMEM((chunk, w), x1p.dtype),
                       pltpu.SemaphoreType.DMA, pltpu.SemaphoreType.DMA])
    def dispatch(idx_hbm, rank_hbm, start_hbm, x_hbm, xg_hbm, slots_hbm, slot_v, rank_v, start_v, rows_v, sem,
                 row_sem):
        wid = lax.axis_index("s") * n_cores + lax.axis_index("c")
        pltpu.sync_copy(start_hbm, start_v)

        @pl.loop(0, n_chunks)
        def _(ci):
            chunk_id = wid * n_chunks + ci
            rows_in = pltpu.make_async_copy(x_hbm.at[pl.ds(chunk_id * chunk, chunk)], rows_v, row_sem)
            rows_in.start()
            pltpu.sync_copy(idx_hbm.at[chunk_id], slot_v)
            pltpu.sync_copy(rank_hbm.at[chunk_id], rank_v)
            for k in range(n_k):
                for j in range(chunk // lanes):
                    part = pl.ds(j * lanes, lanes)
                    slot_v[k, part] = plsc.load_gather(start_v, [slot_v[k, part]]) + rank_v[k, part]
            rows_in.wait()
            copies = [pltpu.make_async_copy(rows_v, xg_hbm.at[slot_v.at[k]], sem) for k in range(n_k)]
            for cp in copies:
                cp.start()
            pltpu.sync_copy(slot_v, slots_hbm.at[chunk_id])
            for cp in copies:
                cp.wait()

    return dispatch(idx, rank, expert_start, x1p)


def _sc_combine(slots, gate_words, yg):
    n_chunks_total, n_k, chunk = slots.shape
    w = yg.shape[1]
    t = n_chunks_total * chunk
    n_cores, n_chunks = _sc_worker_layout(n_chunks_total)
    lanes = plsc.get_sparse_core_info().num_lanes
    n_sub = chunk // SC_SUB
    mesh = plsc.VectorSubcoreMesh(core_axis_name="c", subcore_axis_name="s")
    rows_t = pltpu.VMEM((n_k, SC_SUB, w), jnp.int32)
    out_t = pltpu.VMEM((SC_SUB, w), jnp.int32)
    dma = pltpu.SemaphoreType.DMA

    @functools.partial(
        pl.kernel, mesh=mesh, name="sc_combine_rows",
        compiler_params=pltpu.CompilerParams(needs_layout_passes=False),
        out_type=jax.ShapeDtypeStruct((t, w), jnp.int32),
        scratch_types=[pltpu.VMEM((n_k, chunk), jnp.int32), pltpu.VMEM((n_k, chunk), jnp.int32),
                       rows_t, rows_t, out_t, out_t, dma, dma, dma, dma])
    def combine(slots_hbm, gate_hbm, yg_hbm, out_hbm, idx_v, gate_v, rows0, rows1, out0, out1,
                row_sem0, row_sem1, out_sem0, out_sem1):
        wid = lax.axis_index("s") * n_cores + lax.axis_index("c")
        rows, outs = (rows0, rows1), (out0, out1)
        row_sems, out_sems = (row_sem0, row_sem1), (out_sem0, out_sem1)

        def gathers(sub, b):
            return [pltpu.make_async_copy(yg_hbm.at[idx_v.at[k, pl.ds(sub * SC_SUB, SC_SUB)]], rows[b].at[k],
                                          row_sems[b]) for k in range(n_k)]

        @pl.loop(0, n_chunks)
        def _(ci):
            chunk_id = wid * n_chunks + ci
            tok0 = chunk_id * chunk
            pltpu.sync_copy(slots_hbm.at[chunk_id], idx_v)
            pltpu.sync_copy(gate_hbm.at[chunk_id], gate_v)

            def store(sub, b):
                return pltpu.make_async_copy(outs[b], out_hbm.at[pl.ds(tok0 + sub * SC_SUB, SC_SUB)], out_sems[b])

            for cp in gathers(0, 0):
                cp.start()

            @pl.loop(0, n_sub // 2)
            def _(pair):
                for b in (0, 1):
                    sub = 2 * pair + b

                    @pl.when(sub + 1 < n_sub)
                    def _():
                        for cp in gathers(sub + 1, 1 - b):
                            cp.start()

                    for cp in gathers(sub, b):
                        cp.wait()

                    @pl.when(sub >= 2)
                    def _():
                        store(sub - 2, b).wait()

                    buf, out = rows[b], outs[b]

                    @pl.loop(0, SC_SUB)
                    def _(tt):
                        col = jnp.full((lanes,), sub * SC_SUB + tt, jnp.int32)
                        gates = [plsc.bitcast(plsc.load_gather(gate_v, [jnp.full((lanes,), k, jnp.int32), col]), BF16)
                                 for k in range(n_k)]
                        for v in range(w // lanes):
                            terms = [gates[k] * plsc.bitcast(buf[k, tt, pl.ds(v * lanes, lanes)], BF16)
                                     for k in range(n_k)]
                            while len(terms) > 1:
                                terms = [terms[i] + terms[i + 1] for i in range(0, len(terms), 2)]
                            out[tt, pl.ds(v * lanes, lanes)] = plsc.bitcast(terms[0], jnp.int32)

                    store(sub, b).start()

            store(n_sub - 2, 0).wait()
            store(n_sub - 1, 1).wait()

    return combine(slots, gate_words, yg)


def _expert_kernel(first_block, n_block, n_used, xg_hbm, w1_ref, w3_ref, w2_ref, yg_hbm,
                   x_buf, y_buf, w1_bf, w3_bf, w2_bf, in_sem, out_sem):
    e = pl.program_id(0)
    n_slot, bm, _ = x_buf.shape
    ahead = n_slot - BLOCK_GROUP
    total = n_used[0]

    def fetch(g):
        slot = g % n_slot
        return pltpu.make_async_copy(xg_hbm.at[pl.ds(g * bm, bm)], x_buf.at[slot], in_sem.at[slot])

    def write_back(g):
        slot = g % n_slot
        return pltpu.make_async_copy(y_buf.at[slot], yg_hbm.at[pl.ds(g * bm, bm)], out_sem.at[slot])

    @pl.when(e == 0)
    def _():
        for g in range(ahead):
            @pl.when(g < total)
            def _():
                fetch(g).start()

    def cast_weights():
        ws = [w1_ref[0].astype(BF16), w3_ref[0].astype(BF16), w2_ref[0].astype(BF16)]
        for ref, wb in zip((w1_bf, w3_bf, w2_bf), ws):
            ref[...] = wb
        return ws

    def swiglu(slots, weights):
        w1b, w3b, w2b = weights
        xs = [_unpack_bf16_pairs(x_buf[s]).astype(BF16) for s in slots]
        up = [(_dot(xb, w1b), _dot(xb, w3b)) for xb in xs]
        hs = [(h1 * jax.nn.sigmoid(h1) * h3).astype(BF16) for h1, h3 in up]
        ys = [_dot(h, w2b) for h in hs]
        return [lax.bitcast_convert_type(_pack_bf16_pairs(y), y_buf.dtype) for y in ys]

    def process(blocks, cast_here=False):
        for g in blocks:
            fetch(g).wait()
        for g in blocks:
            @pl.when(g + ahead < total)
            def _():
                fetch(g + ahead).start()
        weights = cast_weights() if cast_here else (w1_bf[...], w3_bf[...], w2_bf[...])
        ys = swiglu([g % n_slot for g in blocks], weights)
        for g in blocks:
            @pl.when(g >= n_slot)
            def _():
                write_back(g - n_slot).wait()
        for g, y in zip(blocks, ys):
            y_buf[g % n_slot] = y
            write_back(g).start()

    n_full = n_block[e] // BLOCK_GROUP

    @pl.when(n_full > 0)
    def _():
        process([first_block[e] + j for j in range(BLOCK_GROUP)], cast_here=True)

    @pl.when((n_full == 0) & (n_block[e] > 0))
    def _():
        cast_weights()

    @pl.loop(1, n_full)
    def _(i):
        process([first_block[e] + i * BLOCK_GROUP + j for j in range(BLOCK_GROUP)])

    done = n_full * BLOCK_GROUP
    size = BLOCK_GROUP // 2
    while size >= 1:
        left = n_block[e] - done
        start = first_block[e] + done

        @pl.when(left >= size)
        def _(start=start, size=size):
            process([start + j for j in range(size)])

        done = done + left // size * size
        size //= 2

    @pl.when(e == pl.num_programs(0) - 1)
    def _():
        for back in range(1, n_slot + 1):
            @pl.when(total >= back)
            def _():
                write_back(total - back).wait()


def _experts(first_block, n_block, n_used, xg, w1, w3, w2, bm):
    n_rows, w = xg.shape
    n_exp, d, de = w1.shape
    weight = lambda e, *_: (e, 0, 0)
    return pl.pallas_call(
        _expert_kernel,
        grid_spec=pltpu.PrefetchScalarGridSpec(
            num_scalar_prefetch=3,
            grid=(n_exp,),
            in_specs=[pl.BlockSpec(memory_space=pl.ANY),
                      pl.BlockSpec((1, d, de), weight),
                      pl.BlockSpec((1, d, de), weight),
                      pl.BlockSpec((1, de, d), weight)],
            out_specs=pl.BlockSpec(memory_space=pl.ANY),
            scratch_shapes=[pltpu.VMEM((ROW_RING, bm, w), xg.dtype), pltpu.VMEM((ROW_RING, bm, w), jnp.int32),
                            pltpu.VMEM((d, de), BF16), pltpu.VMEM((d, de), BF16), pltpu.VMEM((de, d), BF16),
                            pltpu.SemaphoreType.DMA((ROW_RING,)), pltpu.SemaphoreType.DMA((ROW_RING,))]),
        out_shape=jax.ShapeDtypeStruct((n_rows, w), jnp.int32),
        compiler_params=pltpu.CompilerParams(dimension_semantics=("arbitrary",),
                                             vmem_limit_bytes=VMEM_LIMIT),
        name="routed_experts",
    )(first_block, n_block, n_used, xg, w1, w3, w2)


def _combine_kernel(y_ref, x1_ref, sw1_ref, sw3_ref, sw2_ref, g2_ref, b2_ref, out_ref):
    x1 = x1_ref[...]
    xb = x1.astype(BF16)
    h1 = _dot(xb, sw1_ref[...])
    h3 = _dot(xb, sw3_ref[...])
    ffn = _dot((h1 * jax.nn.sigmoid(h1) * h3).astype(BF16), sw2_ref[...])
    ffn = ffn + _unpack_bf16_pairs(lax.bitcast_convert_type(y_ref[...], jnp.uint32))
    out_ref[...] = _layer_norm(DEEPNORM_ALPHA * x1 + ffn, g2_ref[...], b2_ref[...])


def _combine(y_routed, x1, sw1, sw3, sw2, g2, b2, tn=2048):
    t, d = x1.shape
    w = y_routed.shape[1]
    row = lambda p: p.reshape(1, -1)
    operands = (y_routed, x1, sw1.astype(BF16), sw3.astype(BF16), sw2.astype(BF16), row(g2), row(b2))
    in_specs = [pl.BlockSpec((tn, w), lambda i: (i, 0)),
                pl.BlockSpec((tn, d), lambda i: (i, 0))]
    in_specs += [_const_spec(op.shape) for op in operands[2:]]
    return pl.pallas_call(
        _combine_kernel,
        grid=(t // tn,),
        in_specs=in_specs,
        out_specs=pl.BlockSpec((tn, d), lambda i: (i, 0)),
        out_shape=jax.ShapeDtypeStruct((t, d), F32),
        compiler_params=pltpu.CompilerParams(dimension_semantics=("arbitrary",),
                                             vmem_limit_bytes=VMEM_LIMIT),
        name="combine_shared_ln",
    )(*operands)


def _block_table(counts, bm):
    n_block = (counts + bm - 1) // bm
    ends = jnp.cumsum(n_block)
    first_block = ends - n_block
    i32 = lambda v: v.astype(jnp.int32)
    return i32(first_block * bm), i32(first_block), i32(n_block), i32(ends[-1:])


def _moe(x1, x1p, logits, router_bias, w1, w3, w2, sw1, sw3, sw2, g2, b2, bm=256):
    t, d = x1.shape
    n_exp = logits.shape[1]
    idx, gate_words, rank, counts = _route(logits, router_bias)
    n_blocks = (t * TOP_K) // bm + n_exp
    starts, first_block, n_block, n_used = _block_table(counts[:, 0], bm)
    xg, slots = _sc_dispatch(idx, rank, starts, x1p, n_blocks * bm)
    yg = _experts(first_block, n_block, n_used, xg, w1, w3, w2, bm)
    y_routed = _sc_combine(slots, gate_words, yg)
    return _combine(y_routed, x1, sw1, sw3, sw2, g2, b2)


def kernel(x, w_in, conv_w, conv_b, lru_wa, lru_ba, lru_wx, lru_bx, lru_lambda, pool_w, pool_scale, w_out, ln1_g, ln1_b, router_w, router_bias, exp_w1, exp_w3, exp_w2, sh_w1, sh_w3, sh_w2, ln2_g, ln2_b):
    nb, s, d = x.shape
    for l in range(DEPTH):
        x1, x1p, logits = _mixer(x, w_in[l], conv_w[l], conv_b[l], lru_wa[l], lru_ba[l], lru_wx[l], lru_bx[l],
                                 lru_lambda[l], pool_w[l], pool_scale[l], w_out[l], ln1_g[l], ln1_b[l],
                                 router_w[l])
        t = nb * s
        x = _moe(x1.reshape(t, d), x1p.reshape(t, d // 2), logits.reshape(t, -1), router_bias[l],
                 exp_w1[l], exp_w3[l], exp_w2[l], sh_w1[l], sh_w3[l], sh_w2[l], ln2_g[l], ln2_b[l])
        x = x.reshape(nb, s, d)
    return x
```
